```python
import jax, jax.numpy as jnp
from jax import lax
import numpy as np

D_MODEL = 1024
BATCH = 16
SEQ = 2048
DEPTH = 2
DEC_BATCH = 128
DEC_SEQ = 1
PAST_LEN = 16384
PAGE_SIZE = 128

D_MIX = D_MODEL
D_POOL = D_MIX // 4
POOL_WINDOWS = (2, 4, 8, 16)
POOL_GROUPS = len(POOL_WINDOWS)
POOL_CG = D_POOL // POOL_GROUPS
POOL_BUF = max(POOL_WINDOWS) - 1
HEAD_DIM = 64
D_ATTN = D_MIX // 2
N_HEADS = D_ATTN // HEAD_DIM
N_KV = max(1, N_HEADS // 4)
Q_PER_KV = N_HEADS // N_KV
D_KV = N_KV * HEAD_DIM
WINDOW = 128
BLOCK = 128
ROPE_THETA = 10000.0
QK_EPS = 1e-6
D_RWKV = D_MIX - D_POOL - D_ATTN
RWKV_HEAD = 64
RWKV_HEADS = D_RWKV // RWKV_HEAD
LORA_W = 32
LORA_A = 32
D_SHIFT = 3 * D_RWKV + LORA_W + LORA_A
GN_EPS = 64e-5
NORM_EPS = 1e-6
D_IN = 2 * D_POOL + 2 * D_ATTN + 2 * D_KV + D_SHIFT + D_RWKV

kernel_name = "hymba_pool_swa_rwkv7_step"


def _split(z, sizes):
    idx = [int(i) for i in np.cumsum(sizes)[:-1]]
    return jnp.split(z, idx, axis=-1)


def _rmsnorm(x, g, eps):
    xf = x.astype(jnp.float32)
    y = xf * lax.rsqrt(jnp.mean(xf * xf, axis=-1, keepdims=True) + eps)
    return (y * g.astype(jnp.float32)).astype(x.dtype)


def _rope(x, pos):
    half = HEAD_DIM // 2
    freqs = ROPE_THETA ** (-jnp.arange(half, dtype=jnp.float32) / half)
    ang = pos.astype(jnp.float32)[:, None] * freqs[None, :]
    c = jnp.cos(ang)[None, :, None, :]
    s = jnp.sin(ang)[None, :, None, :]
    xf = x.astype(jnp.float32)
    x1, x2 = xf[..., :half], xf[..., half:]
    return jnp.concatenate([x1 * c - x2 * s, x1 * s + x2 * c], axis=-1).astype(x.dtype)


def _sink_softmax(s, mask, sinks):
    s = jnp.where(mask, s, -jnp.inf)
    sink = sinks.astype(jnp.float32)[:, :, None, None]
    m = jnp.maximum(jnp.max(s, axis=-1, keepdims=True), sink)
    p = jnp.exp(s - m)
    denom = jnp.sum(p, axis=-1, keepdims=True) + jnp.exp(sink - m)
    return p / denom


def _swa_prompt(q, k, v, sinks):
    B, T = q.shape[:2]
    nb = T // BLOCK
    scale = HEAD_DIM ** -0.5
    qb = q.reshape(B, nb, BLOCK, N_KV, Q_PER_KV, HEAD_DIM)

    def with_prev(t):
        tb = t.reshape(B, nb, BLOCK, N_KV, HEAD_DIM)
        prev = jnp.pad(tb, ((0, 0), (1, 0), (0, 0), (0, 0), (0, 0)))[:, :-1]
        return jnp.concatenate([prev, tb], axis=2)

    kx, vx = with_prev(k), with_prev(v)
    s = jnp.einsum('bnqkgd,bnskd->bnkgqs', qb, kx, preferred_element_type=jnp.float32) * scale
    qi = jnp.arange(BLOCK)[:, None] + BLOCK
    ki = jnp.arange(2 * BLOCK)[None, :]
    band = (ki <= qi) & (ki > qi - WINDOW)
    kabs = jnp.arange(nb)[:, None] * BLOCK - BLOCK + jnp.arange(2 * BLOCK)[None, :]
    mask = band[None, :, :] & (kabs >= 0)[:, None, :]
    mask = mask[None, :, None, None]
    p = _sink_softmax(s, mask, sinks.reshape(N_KV, Q_PER_KV))
    o = jnp.einsum('bnkgqs,bnskd->bnqkgd', p.astype(vx.dtype), vx)
    return o.reshape(B, T, N_HEADS * HEAD_DIM)


def _swa_cached(q, k, v, k_buf, v_buf, pos0, sinks):
    B, T = q.shape[:2]
    scale = HEAD_DIM ** -0.5
    kx = jnp.concatenate([k_buf.astype(k.dtype), k], axis=1)
    vx = jnp.concatenate([v_buf.astype(v.dtype), v], axis=1)
    L = kx.shape[1]
    kpos = pos0 - k_buf.shape[1] + jnp.arange(L)
    qpos = pos0 + jnp.arange(T)
    mask = (kpos[None, :] <= qpos[:, None]) & (kpos[None, :] > qpos[:, None] - WINDOW) & (kpos[None, :] >= 0)
    qg = q.reshape(B, T, N_KV, Q_PER_KV, HEAD_DIM)
    s = jnp.einsum('btkgd,bskd->bkgts', qg, kx, preferred_element_type=jnp.float32) * scale
    p = _sink_softmax(s, mask, sinks.reshape(N_KV, Q_PER_KV))
    o = jnp.einsum('bkgts,bskd->btkgd', p.astype(vx.dtype), vx).reshape(B, T, N_HEADS * HEAD_DIM)
    return o, kx[:, -WINDOW:], vx[:, -WINDOW:]


def _pool_mixer(u, buf, pos0, pool_w, pool_scale):
    B, T, C = u.shape
    P = buf.shape[1]
    ext = jnp.concatenate([buf.astype(u.dtype), u], axis=1)
    cs = jnp.cumsum(ext.astype(jnp.float32), axis=1)
    cs = jnp.pad(cs, ((0, 0), (1, 0), (0, 0)))
    pos = pos0 + jnp.arange(T)
    uf = u.astype(jnp.float32)
    outs = []
    for g, w in enumerate(POOL_WINDOWS):
        sl = slice(g * POOL_CG, (g + 1) * POOL_CG)
        win_sum = cs[:, P + 1:P + 1 + T, sl] - cs[:, P + 1 - w:P + 1 - w + T, sl]
        cnt = jnp.minimum(pos + 1, w).astype(jnp.float32)[None, :, None]
        outs.append(win_sum / cnt - uf[..., sl])
    pooled = jnp.concatenate(outs, axis=-1).reshape(B, T, POOL_GROUPS, POOL_CG)
    mixed = jnp.einsum('btgc,gcd->btgd', pooled, pool_w.astype(jnp.float32)).reshape(B, T, C)
    mixed = mixed * pool_scale.astype(jnp.float32)
    return mixed.astype(u.dtype), ext[:, -P:]


def _rwkv_mix(xin, shift_prev, wkv0, p):
    B, T, _ = xin.shape
    f32 = jnp.float32
    prev = jnp.concatenate([shift_prev[:, None].astype(xin.dtype), xin[:, :-1]], axis=1)
    xf = xin.astype(f32)
    xs = xf + (prev.astype(f32) - xf) * p['mu'].astype(f32)
    r, k, v, wd, ad = _split(xs, (D_RWKV, D_RWKV, D_RWKV, LORA_W, LORA_A))
    w_log = -jax.nn.softplus(-(p['w0'].astype(f32) + jnp.tanh(wd) @ p['w_up'].astype(f32))) - 0.5
    decay = jnp.exp(-jnp.exp(w_log))
    a = jax.nn.sigmoid(p['a0'].astype(f32) + ad @ p['a_up'].astype(f32))
    heads = lambda t: t.reshape(B, T, RWKV_HEADS, RWKV_HEAD)
    kk = heads(k * p['k_k'].astype(f32))
    kk = kk / jnp.maximum(jnp.sqrt(jnp.sum(kk * kk, axis=-1, keepdims=True)), 1e-12)
    k = k * (1.0 + (a - 1.0) * p['k_a'].astype(f32))
    r, k, v, decay, a = heads(r), heads(k), heads(v), heads(decay), heads(a)

    def step(S, inp):
        r_t, w_t, k_t, v_t, kk_t, a_t = inp
        sa = jnp.einsum('bhvk,bhk->bhv', S, kk_t)
        S = S * w_t[:, :, None, :] - sa[..., None] * (kk_t * a_t)[:, :, None, :] + v_t[..., None] * k_t[:, :, None, :]
        o = jnp.einsum('bhvk,bhk->bhv', S, r_t)
        return S, o

    seq = tuple(jnp.moveaxis(t, 1, 0) for t in (r, decay, k, v, kk, a))
    S, o = lax.scan(step, wkv0.astype(f32), seq)
    o = jnp.moveaxis(o, 0, 1)
    mu_o = jnp.mean(o, axis=-1, keepdims=True)
    var_o = jnp.mean(jnp.square(o - mu_o), axis=-1, keepdims=True)
    o = (o - mu_o) * lax.rsqrt(var_o + GN_EPS)
    o = o * p['ln_g'].astype(f32).reshape(RWKV_HEADS, RWKV_HEAD) + p['ln_b'].astype(f32).reshape(RWKV_HEADS, RWKV_HEAD)
    bonus = jnp.sum(r * k * p['r_k'].astype(f32).reshape(RWKV_HEADS, RWKV_HEAD), axis=-1, keepdims=True) * v
    o = (o + bonus).reshape(B, T, D_RWKV)
    return o.astype(xin.dtype), xin[:, -1], S.astype(wkv0.dtype)


def _layer(x, pos0, pool_buf, k_buf, v_buf, shift_prev, wkv0, p):
    B, T, _ = x.shape
    dt = x.dtype
    h = _rmsnorm(x, p['norm_g'], NORM_EPS)
    z = h @ p['w_in']
    u, g_pool, q, k, v, g_attn, rin, g_rwkv = _split(
        z, (D_POOL, D_POOL, D_ATTN, D_KV, D_KV, D_ATTN, D_SHIFT, D_RWKV))
    a_out, new_pool = _pool_mixer(u, pool_buf, pos0, p['pool_w'], p['pool_scale'])
    pos = pos0 + jnp.arange(T)
    q = _rope(_rmsnorm(q.reshape(B, T, N_HEADS, HEAD_DIM), p['q_g'], QK_EPS), pos)
    k = _rope(_rmsnorm(k.reshape(B, T, N_KV, HEAD_DIM), p['k_g'], QK_EPS), pos)
    v = v.reshape(B, T, N_KV, HEAD_DIM)
    if k_buf is None:
        b_out = _swa_prompt(q, k, v, p['sinks'])
        new_k, new_v = k[:, -WINDOW:], v[:, -WINDOW:]
    else:
        b_out, new_k, new_v = _swa_cached(q, k, v, k_buf, v_buf, pos0, p['sinks'])
    c_out, new_shift, new_wkv = _rwkv_mix(rin, shift_prev, wkv0, p)
    mix = jnp.concatenate([
        a_out.astype(dt) * jax.nn.silu(g_pool),
        b_out.astype(dt) * jax.nn.silu(g_attn),
        c_out.astype(dt) * jax.nn.silu(g_rwkv)], axis=-1)
    y = x + mix @ p['w_out']
    return y, (new_pool, new_k, new_v, new_shift, new_wkv)


def setup_inputs(seed: int = 0) -> dict:
    key = jax.random.key(seed)
    ks = jax.random.split(key, 32)
    f = jnp.float32
    nrm = lambda kk, shape, sc: jax.random.normal(kk, shape, f) * sc
    uni = lambda kk, shape, lo, hi: jax.random.uniform(kk, shape, f, lo, hi)
    return {
        'x_prompt': nrm(ks[0], (BATCH, SEQ, D_MODEL), 1.0),
        'x_sample': nrm(ks[1], (DEC_BATCH, DEC_SEQ, D_MODEL), 1.0),
        'state_pool': nrm(ks[2], (DEPTH, DEC_BATCH, POOL_BUF, D_POOL), 1.0),
        'cache_swa_k': nrm(ks[3], (DEPTH, DEC_BATCH, WINDOW, N_KV, HEAD_DIM), 1.0),
        'cache_swa_v': nrm(ks[4], (DEPTH, DEC_BATCH, WINDOW, N_KV, HEAD_DIM), 1.0),
        'state_rwkv_shift': nrm(ks[5], (DEPTH, DEC_BATCH, D_SHIFT), 1.0),
        'state_rwkv_wkv': nrm(ks[6], (DEPTH, DEC_BATCH, RWKV_HEADS, RWKV_HEAD, RWKV_HEAD), 0.3),
        'norm_g': 1.0 + nrm(ks[7], (DEPTH, D_MODEL), 0.05),
        'w_in': nrm(ks[8], (DEPTH, D_MODEL, D_IN), D_MODEL ** -0.5),
        'w_out': nrm(ks[9], (DEPTH, D_MIX, D_MODEL), D_MIX ** -0.5),
        'pool_w': nrm(ks[10], (DEPTH, POOL_GROUPS, POOL_CG, POOL_CG), POOL_CG ** -0.5),
        'pool_scale': 1.0 + nrm(ks[11], (DEPTH, D_POOL), 0.1),
        'q_norm_g': 1.0 + nrm(ks[12], (DEPTH, HEAD_DIM), 0.05),
        'k_norm_g': 1.0 + nrm(ks[13], (DEPTH, HEAD_DIM), 0.05),
        'attn_sinks': nrm(ks[14], (DEPTH, N_HEADS), 0.5),
        'rwkv_mu': uni(ks[15], (DEPTH, D_SHIFT), 0.0, 1.0),
        'rwkv_w0': uni(ks[16], (DEPTH, D_RWKV), -5.0, 1.0),
        'rwkv_w_up': nrm(ks[17], (DEPTH, LORA_W, D_RWKV), 0.5 * LORA_W ** -0.5),
        'rwkv_a0': nrm(ks[18], (DEPTH, D_RWKV), 0.5),
        'rwkv_a_up': nrm(ks[19], (DEPTH, LORA_A, D_RWKV), 0.5 * LORA_A ** -0.5),
        'rwkv_k_k': 0.85 + nrm(ks[20], (DEPTH, D_RWKV), 0.05),
        'rwkv_k_a': 1.0 + nrm(ks[21], (DEPTH, D_RWKV), 0.05),
        'rwkv_r_k': nrm(ks[22], (DEPTH, D_RWKV), 0.1),
        'rwkv_ln_g': 1.0 + nrm(ks[23], (DEPTH, D_RWKV), 0.05),
        'rwkv_ln_b': nrm(ks[24], (DEPTH, D_RWKV), 0.02),
    }


def reference(x_prompt, x_sample, state_pool, cache_swa_k, cache_swa_v, state_rwkv_shift, state_rwkv_wkv,
              norm_g, w_in, w_out, pool_w, pool_scale, q_norm_g, k_norm_g, attn_sinks,
              rwkv_mu, rwkv_w0, rwkv_w_up, rwkv_a0, rwkv_a_up, rwkv_k_k, rwkv_k_a, rwkv_r_k,
              rwkv_ln_g, rwkv_ln_b):
    Bp = x_prompt.shape[0]
    dt = x_prompt.dtype
    yp, ys = x_prompt, x_sample
    pool_p, pool_s, kp, ksmp, vp, vsmp, shp, shs, wkp, wks = [], [], [], [], [], [], [], [], [], []
    for l in range(DEPTH):
        p = {
            'norm_g': norm_g[l], 'w_in': w_in[l], 'w_out': w_out[l],
            'pool_w': pool_w[l], 'pool_scale': pool_scale[l],
            'q_g': q_norm_g[l], 'k_g': k_norm_g[l], 'sinks': attn_sinks[l],
            'mu': rwkv_mu[l], 'w0': rwkv_w0[l], 'w_up': rwkv_w_up[l], 'a0': rwkv_a0[l],
            'a_up': rwkv_a_up[l], 'k_k': rwkv_k_k[l], 'k_a': rwkv_k_a[l], 'r_k': rwkv_r_k[l],
            'ln_g': rwkv_ln_g[l], 'ln_b': rwkv_ln_b[l],
        }
        yp, (a1, b1, c1, d1, e1) = _layer(
            yp, 0,
            jnp.zeros((Bp, POOL_BUF, D_POOL), dt), None, None,
            jnp.zeros((Bp, D_SHIFT), dt),
            jnp.zeros((Bp, RWKV_HEADS, RWKV_HEAD, RWKV_HEAD), dt), p)
        ys, (a2, b2, c2, d2, e2) = _layer(
            ys, PAST_LEN, state_pool[l], cache_swa_k[l], cache_swa_v[l],
            state_rwkv_shift[l], state_rwkv_wkv[l], p)
        pool_p.append(a1); pool_s.append(a2)
        kp.append(b1); ksmp.append(b2)
        vp.append(c1); vsmp.append(c2)
        shp.append(d1); shs.append(d2)
        wkp.append(e1); wks.append(e2)
    return (yp, ys,
            jnp.stack(pool_p), jnp.stack(pool_s),
            jnp.stack(kp), jnp.stack(ksmp),
            jnp.stack(vp), jnp.stack(vsmp),
            jnp.stack(shp), jnp.stack(shs),
            jnp.stack(wkp), jnp.stack(wks))
```

```python
import functools

import numpy as np
import jax
import jax.numpy as jnp
from jax import lax
from jax.experimental import pallas as pl
from jax.experimental.pallas import tpu as pltpu

D_MODEL = 1024
D_POOL = 256
POOL_WINDOWS = (2, 4, 8, 16)
POOL_CG = 64
POOL_BUF = 15
HEAD_DIM = 64
D_ATTN = 512
N_HEADS = 8
N_KV = 2
Q_PER_KV = 4
D_KV = 128
WINDOW = 128
BLOCK = 128
ROPE_THETA = 10000.0
QK_EPS = 1e-6
D_RWKV = 256
RWKV_HEAD = 64
RWKV_HEADS = 4
LORA = 32
D_SHIFT = 832
D_SHIFT_PAD = 896
GN_EPS = 64e-5
NORM_EPS = 1e-6
PAST_LEN = 16384
ATTN_SCALE = HEAD_DIM ** -0.5

OFF_POOL = 0
OFF_QKV = 512
OFF_GATTN = 1280
OFF_RIN = 1792
OFF_GRWKV = 2688
D_IN_PAD = 2944

CHUNK = 64
NEG = -1e30
F32 = jnp.float32
BF16 = jnp.bfloat16

VMEM_LIMIT_BYTES = 56 * 1024 * 1024


def _bf(x):
    return x.astype(BF16)


def _mm(x, y):
    return jnp.dot(_bf(x), _bf(y), preferred_element_type=F32)


def _bnt(x, y):
    return jnp.einsum('bik,bjk->bij', _bf(x), _bf(y), preferred_element_type=F32)


def _bnn(x, y):
    return jnp.einsum('bik,bkj->bij', _bf(x), _bf(y), preferred_element_type=F32)


def _btn(x, y):
    return jnp.einsum('bli,blj->bij', _bf(x), _bf(y), preferred_element_type=F32)


def _sigmoid(x):
    return 1.0 / (1.0 + jnp.exp(-x))


def _silu(x):
    return x * _sigmoid(x)


def _softplus(x):
    return jnp.maximum(x, 0.0) + jnp.log(1.0 + jnp.exp(-jnp.abs(x)))


def _split2(x):
    hi = _bf(x)
    lo = _bf(x - hi.astype(F32))
    return hi, lo


def _split3(x):
    h1 = _bf(x)
    r1 = x - h1.astype(F32)
    h2 = _bf(r1)
    h3 = _bf(r1 - h2.astype(F32))
    return h1, h2, h3


def _block_ones(n, seg):
    r = lax.broadcasted_iota(jnp.int32, (n, n), 0) // seg
    c = lax.broadcasted_iota(jnp.int32, (n, n), 1) // seg
    return jnp.where(r == c, 1.0, 0.0).astype(BF16)


def _seg_sum(x, ones_bd):
    w = x.shape[-1]
    step = min(w, ones_bd.shape[0])
    outs = []
    for c0 in range(0, w, step):
        hi, lo = _split2(x[:, c0:c0 + step])
        bd = ones_bd[:step, :step]
        outs.append(jnp.dot(hi, bd, preferred_element_type=F32)
                    + jnp.dot(lo, bd, preferred_element_type=F32))
    return outs[0] if len(outs) == 1 else jnp.concatenate(outs, axis=-1)


def _rmsnorm_rows(x, g):
    ms = jnp.mean(x * x, axis=-1, keepdims=True)
    return x * lax.rsqrt(ms + NORM_EPS) * g


def _swap_halves(x):
    w = x.shape[-1]
    lane = lax.broadcasted_iota(jnp.int32, x.shape, x.ndim - 1)
    fwd = pltpu.roll(x, w - HEAD_DIM // 2, x.ndim - 1)
    bwd = pltpu.roll(x, HEAD_DIM // 2, x.ndim - 1)
    return jnp.where(lane % HEAD_DIM < HEAD_DIM // 2, fwd, bwd)


def _qk_norm_rope(x, g, cos, sin_signed, ones_bd):
    ms = _seg_sum(x * x, ones_bd) * (1.0 / HEAD_DIM)
    xn = x * lax.rsqrt(ms + QK_EPS) * g
    reps = x.shape[-1] // cos.shape[-1]
    if reps > 1:
        cos = jnp.concatenate([cos] * reps, axis=-1)
        sin_signed = jnp.concatenate([sin_signed] * reps, axis=-1)
    return xn * cos + _swap_halves(xn) * sin_signed


def _rwkv_token_params(xs, rwv, lora_w, ones_bd):
    r = xs[:, 0:256]
    k = xs[:, 256:512]
    v = xs[:, 512:768]
    lo = xs[:, 768:896]
    lane = lax.broadcasted_iota(jnp.int32, lo.shape, 1)
    lo_in = jnp.where(lane < LORA, jnp.tanh(lo), lo)
    pre = _mm(lo_in, lora_w)
    w_pre = rwv[0:1, :] + pre[:, 0:256]
    a_pre = rwv[1:2, :] + pre[:, 256:512]
    w_log = -_softplus(-w_pre) - 0.5
    logw = -jnp.exp(w_log)
    a = _sigmoid(a_pre)
    kk = k * rwv[2:3, :]
    nrm = jnp.sqrt(_seg_sum(kk * kk, ones_bd))
    kk = kk / jnp.maximum(nrm, 1e-12)
    k2 = k * (1.0 + (a - 1.0) * rwv[3:4, :])
    return r, k2, v, kk, a, logw


def _rwkv_finish(o, r, k2, v, rwv, ones_bd):
    inv_n = 1.0 / RWKV_HEAD
    mu_o = _seg_sum(o, ones_bd) * inv_n
    d = o - mu_o
    var = _seg_sum(d * d, ones_bd) * inv_n
    on = d * lax.rsqrt(var + GN_EPS) * rwv[5:6, :] + rwv[6:7, :]
    bonus = _seg_sum(r * k2 * rwv[4:5, :], ones_bd) * v
    return on + bonus


def _prompt_kernel(sinks_ref, x_ref, cos_ref, sin_ref, ng_ref, win_ref, wout_ref, pwbd_ref,
                   lora_ref, pscale_ref, qg_ref, kg_ref, mu_ref, rwv_ref,
                   y_ref, pool_ref, ko_ref, vo_ref, sh_ref, wkv_ref,
                   pext, kprev, vprev, rext, s_ref, *, tb):
    t = pl.program_id(1)
    last = pl.num_programs(1) - 1
    nqb = tb // BLOCK
    nch = tb // CHUNK

    @pl.when(t == 0)
    def _init():
        pext[...] = jnp.zeros(pext.shape, F32)
        kprev[...] = jnp.zeros(kprev.shape, BF16)
        vprev[...] = jnp.zeros(vprev.shape, BF16)
        rext[pl.ds(0, 8), :] = jnp.zeros((8, D_SHIFT_PAD), F32)
        s_ref[...] = jnp.zeros(s_ref.shape, F32)

    ones_bd = _block_ones(256, HEAD_DIM)
    x = x_ref[0]
    hb = _bf(_rmsnorm_rows(x, ng_ref[...]))

    zp = jnp.dot(hb, win_ref[:, OFF_POOL:OFF_POOL + 512], preferred_element_type=F32)
    u = zp[:, 0:256]
    g_pool = zp[:, 256:512]
    pext[pl.ds(24, tb), :] = u
    n_ext = tb + 16
    wins = []
    for sh in (1, 2, 4, 8):
        cur = pext[pl.ds(8, n_ext), :] + pext[pl.ds(8 - sh, n_ext), :]
        pext[pl.ds(8, n_ext), :] = cur
        wins.append(cur[16:16 + tb, :])
    pext[pl.ds(8, 16), :] = u[tb - 16:tb, :]
    lane = lax.broadcasted_iota(jnp.int32, (tb, D_POOL), 1)
    row = lax.broadcasted_iota(jnp.int32, (tb, D_POOL), 0)
    grp = lane // POOL_CG
    win_sum = jnp.where(grp == 0, wins[0], jnp.where(grp == 1, wins[1],
                        jnp.where(grp == 2, wins[2], wins[3])))
    wlen = jnp.where(grp == 0, 2, jnp.where(grp == 1, 4, jnp.where(grp == 2, 8, 16)))
    cnt = jnp.minimum(t * tb + row + 1, wlen).astype(F32)
    pooled = win_sum / cnt - u
    a_out = _mm(pooled, pwbd_ref[...]) * pscale_ref[...] * _silu(g_pool)

    @pl.when(t == last)
    def _pool_state():
        pool_ref[0] = u[tb - POOL_BUF:tb, :]

    zq = jnp.dot(hb, win_ref[:, OFF_QKV:OFF_QKV + 768], preferred_element_type=F32)
    cos = cos_ref[...]
    sin = sin_ref[...]
    q = _qk_norm_rope(zq[:, 0:512], qg_ref[...], cos, sin, ones_bd) * ATTN_SCALE
    k = _qk_norm_rope(zq[:, 512:640], kg_ref[...], cos, sin, ones_bd)
    v = zq[:, 640:768]
    qb = _bf(q)
    kb = _bf(k)
    vb = _bf(v)

    nrow = Q_PER_KV * BLOCK
    qi = lax.broadcasted_iota(jnp.int32, (nrow, 2 * BLOCK), 0) % BLOCK
    kj = lax.broadcasted_iota(jnp.int32, (nrow, 2 * BLOCK), 1)
    band = jnp.where(kj > qi, jnp.where(kj <= qi + BLOCK, 0.0, NEG), NEG)
    first_lo = jnp.where(t == 0, BLOCK, 0)
    band_first = jnp.where(kj >= first_lo, band, NEG)
    hrow = lax.broadcasted_iota(jnp.int32, (nrow, 1), 0) // BLOCK

    b_blocks = []
    for j in range(nqb):
        rs = slice(j * BLOCK, (j + 1) * BLOCK)
        if j == 0:
            kx = jnp.concatenate([kprev[...], kb[rs]], axis=0)
            vx = jnp.concatenate([vprev[...], vb[rs]], axis=0)
            valid = band_first
        else:
            kx = kb[(j - 1) * BLOCK:(j + 1) * BLOCK]
            vx = vb[(j - 1) * BLOCK:(j + 1) * BLOCK]
            valid = band
        heads = []
        for g in range(N_KV):
            kxg = kx[:, g * HEAD_DIM:(g + 1) * HEAD_DIM]
            vxg = vx[:, g * HEAD_DIM:(g + 1) * HEAD_DIM]
            qs = jnp.concatenate(
                [qb[rs, (g * Q_PER_KV + i) * HEAD_DIM:(g * Q_PER_KV + i + 1) * HEAD_DIM]
                 for i in range(Q_PER_KV)], axis=0)
            s = lax.dot_general(qs, kxg, (((1,), (1,)), ((), ())),
                                preferred_element_type=F32)
            s = s + valid
            sink = jnp.where(hrow == 0, sinks_ref[g * Q_PER_KV],
                             jnp.where(hrow == 1, sinks_ref[g * Q_PER_KV + 1],
                                       jnp.where(hrow == 2, sinks_ref[g * Q_PER_KV + 2],
                                                 sinks_ref[g * Q_PER_KV + 3])))
            m = jnp.maximum(jnp.max(s, axis=-1, keepdims=True), sink)
            p = jnp.exp(s - m)
            denom = jnp.sum(p, axis=-1, keepdims=True) + jnp.exp(sink - m)
            o = jnp.dot(_bf(p), vxg, preferred_element_type=F32) / denom
            for i in range(Q_PER_KV):
                heads.append(o[i * BLOCK:(i + 1) * BLOCK, :])
        b_blocks.append(jnp.concatenate(heads, axis=-1))
    b_att = b_blocks[0] if nqb == 1 else jnp.concatenate(b_blocks, axis=0)
    kprev[...] = kb[tb - BLOCK:tb, :]
    vprev[...] = vb[tb - BLOCK:tb, :]

    @pl.when(t == last)
    def _kv_state():
        ko_ref[0] = k[tb - WINDOW:tb, :]
        vo_ref[0] = v[tb - WINDOW:tb, :]

    g_attn = jnp.dot(hb, win_ref[:, OFF_GATTN:OFF_GATTN + 512], preferred_element_type=F32)
    b_out = b_att * _silu(g_attn)

    rin = jnp.dot(hb, win_ref[:, OFF_RIN:OFF_RIN + D_SHIFT_PAD], preferred_element_type=F32)
    rext[pl.ds(8, tb), :] = rin
    prev = rext[pl.ds(7, tb), :]
    rext[pl.ds(7, 1), :] = rin[tb - 1:tb, :]
    xs = rin + (prev - rin) * mu_ref[...]

    @pl.when(t == last)
    def _shift_state():
        sh_ref[0] = rin[tb - 1:tb, 0:D_SHIFT]

    rwv = rwv_ref[...]
    r, k2, vv, kk, a, logw = _rwkv_token_params(xs, rwv, lora_ref[...], ones_bd)

    ti = lax.broadcasted_iota(jnp.int32, (tb, tb), 0)
    tj = lax.broadcasted_iota(jnp.int32, (tb, tb), 1)
    tril_blk = jnp.where(ti // CHUNK == tj // CHUNK, jnp.where(tj <= ti, 1.0, 0.0), 0.0).astype(BF16)
    cum = None
    for piece in _split3(logw):
        c = jnp.dot(tril_blk, piece, preferred_element_type=F32)
        cum = c if cum is None else cum + c
    cum3 = cum.reshape(nch, CHUNK, D_RWKV)
    cum_last = jnp.broadcast_to(cum3[:, CHUNK - 1:CHUNK, :], cum3.shape).reshape(tb, D_RWKV)
    w_inc = jnp.exp(cum)
    w_exc = jnp.exp(cum - logw)
    w_inv = jnp.exp(-cum)
    rel = jnp.exp(cum_last - cum)
    w_last = jnp.exp(cum_last)
    bvec = kk * a

    def stack(z):
        parts = [z[:, h * RWKV_HEAD:(h + 1) * RWKV_HEAD].reshape(nch, 1, CHUNK, RWKV_HEAD)
                 for h in range(RWKV_HEADS)]
        return jnp.concatenate(parts, axis=1).reshape(nch * RWKV_HEADS, CHUNK, RWKV_HEAD)

    kap = stack(kk * w_exc)
    bt = stack(bvec * w_inv)
    kt = stack(k2 * w_inv)
    rt = stack(r * w_inc)
    bt_l = stack(bvec * rel)
    kt_l = stack(k2 * rel)
    vs = stack(vv)
    wl = stack(w_last)

    li = lax.broadcasted_iota(jnp.int32, (1, CHUNK, CHUNK), 1)
    lj = lax.broadcasted_iota(jnp.int32, (1, CHUNK, CHUNK), 2)
    strict = lj < li
    incl = lj <= li
    a_b = jnp.where(strict, _bnt(kap, bt), 0.0)
    a_k = jnp.where(strict, _bnt(kap, kt), 0.0)
    p_b = jnp.where(incl, _bnt(rt, bt), 0.0)
    p_k = jnp.where(incl, _bnt(rt, kt), 0.0)
    xcat = jnp.concatenate([kap, -_bnn(a_k, vs)], axis=2)
    pw = -a_b
    span = 1
    while True:
        xcat = xcat + _bnn(pw, xcat)
        span *= 2
        if span >= CHUNK:
            break
        pw = _bnn(pw, pw)
    kap2 = xcat[:, :, 0:RWKV_HEAD]
    u0 = xcat[:, :, RWKV_HEAD:2 * RWKV_HEAD]
    o0 = _bnn(p_b, u0) + _bnn(p_k, vs)
    rp = rt - _bnn(p_b, kap2)
    eye = jnp.where(li == lj, 1.0, 0.0)
    mmat = eye * wl - _btn(kap2, bt_l)
    cmat = _btn(u0, bt_l) + _btn(vs, kt_l)

    s_cur = s_ref[...]
    o_chunks = []
    for c in range(nch):
        sl = slice(c * RWKV_HEADS, (c + 1) * RWKV_HEADS)
        o_chunks.append(_bnt(rp[sl], s_cur) + o0[sl])
        s_cur = _bnn(s_cur, mmat[sl]) + cmat[sl]
    s_ref[...] = s_cur

    @pl.when(t == last)
    def _wkv_state():
        wkv_ref[0] = s_cur.reshape(RWKV_HEADS * RWKV_HEAD, RWKV_HEAD)

    o_heads = []
    for h in range(RWKV_HEADS):
        o_heads.append(jnp.concatenate([oc[h] for oc in o_chunks], axis=0))
    o_wkv = jnp.concatenate(o_heads, axis=-1)
    g_rwkv = jnp.dot(hb, win_ref[:, OFF_GRWKV:OFF_GRWKV + 256], preferred_element_type=F32)
    c_out = _rwkv_finish(o_wkv, r, k2, vv, rwv, ones_bd) * _silu(g_rwkv)

    mix = jnp.concatenate([_bf(a_out), _bf(b_out), _bf(c_out)], axis=-1)
    y_ref[0] = x + jnp.dot(mix, wout_ref[...], preferred_element_type=F32)


def _const_spec(shape):
    nd = len(shape)
    return pl.BlockSpec(shape, lambda *_: (0,) * nd)


def _prompt_layer(x, cos, sin, p, tb):
    bsz, seq, _ = x.shape
    nt = seq // tb
    grid = (bsz, nt)
    in_specs = [
        pl.BlockSpec(memory_space=pltpu.SMEM),
        pl.BlockSpec((1, tb, D_MODEL), lambda b, t: (b, t, 0)),
        pl.BlockSpec((tb, 128), lambda b, t: (t, 0)),
        pl.BlockSpec((tb, 128), lambda b, t: (t, 0)),
        _const_spec((1, D_MODEL)),
        _const_spec((D_MODEL, D_IN_PAD)),
        _const_spec((D_MODEL, D_MODEL)),
        _const_spec((D_POOL, D_POOL)),
        _const_spec((128, 512)),
        _const_spec((1, D_POOL)),
        _const_spec((1, D_ATTN)),
        _const_spec((1, D_KV)),
        _const_spec((1, D_SHIFT_PAD)),
        _const_spec((8, D_RWKV)),
    ]
    out_shape = (
        jax.ShapeDtypeStruct((bsz, seq, D_MODEL), F32),
        jax.ShapeDtypeStruct((bsz, POOL_BUF, D_POOL), F32),
        jax.ShapeDtypeStruct((bsz, WINDOW, D_KV), F32),
        jax.ShapeDtypeStruct((bsz, WINDOW, D_KV), F32),
        jax.ShapeDtypeStruct((bsz, 1, D_SHIFT), F32),
        jax.ShapeDtypeStruct((bsz, RWKV_HEADS * RWKV_HEAD, RWKV_HEAD), F32),
    )
    out_specs = (
        pl.BlockSpec((1, tb, D_MODEL), lambda b, t: (b, t, 0)),
        pl.BlockSpec((1, POOL_BUF, D_POOL), lambda b, t: (b, 0, 0)),
        pl.BlockSpec((1, WINDOW, D_KV), lambda b, t: (b, 0, 0)),
        pl.BlockSpec((1, WINDOW, D_KV), lambda b, t: (b, 0, 0)),
        pl.BlockSpec((1, 1, D_SHIFT), lambda b, t: (b, 0, 0)),
        pl.BlockSpec((1, RWKV_HEADS * RWKV_HEAD, RWKV_HEAD), lambda b, t: (b, 0, 0)),
    )
    scratch = [
        pltpu.VMEM((tb + 24, D_POOL), F32),
        pltpu.VMEM((BLOCK, D_KV), BF16),
        pltpu.VMEM((BLOCK, D_KV), BF16),
        pltpu.VMEM((tb + 8, D_SHIFT_PAD), F32),
        pltpu.VMEM((RWKV_HEADS, RWKV_HEAD, RWKV_HEAD), F32),
    ]
    return pl.pallas_call(
        functools.partial(_prompt_kernel, tb=tb),
        grid=grid,
        in_specs=in_specs,
        out_specs=out_specs,
        out_shape=out_shape,
        scratch_shapes=scratch,
        compiler_params=pltpu.CompilerParams(
            dimension_semantics=("parallel", "arbitrary"),
            vmem_limit_bytes=VMEM_LIMIT_BYTES),
        name="prompt_layer",
    )(p['sinks'], x, cos, sin, p['norm_g'], p['w_in'], p['w_out'], p['pool_wbd'], p['lora_w'],
      p['pool_scale'], p['q_g'], p['k_g'], p['mu'], p['rwv'])


def _heads_rows(z):
    return jnp.stack([z[:, h * RWKV_HEAD:(h + 1) * RWKV_HEAD] for h in range(RWKV_HEADS)], axis=1)


def _sample_kernel(sinks_ref, x_ref, pool_ref, kc_ref, vc_ref, shp_ref, s_ref, cos_ref, sin_ref,
                   ng_ref, win_ref, wout_ref, pwbd_ref, lora_ref, pscale_ref, qg_ref, kg_ref,
                   mu_ref, rwv_ref,
                   y_ref, poolo_ref, ko_ref, vo_ref, sho_ref, so_ref, *, bt):
    ones_bd = _block_ones(256, HEAD_DIM)
    x = x_ref[...]
    hb = _bf(_rmsnorm_rows(x, ng_ref[...]))

    zp = jnp.dot(hb, win_ref[:, OFF_POOL:OFF_POOL + 512], preferred_element_type=F32)
    u = zp[:, 0:256]
    g_pool = zp[:, 256:512]
    buf = pool_ref[...]
    ri = lax.broadcasted_iota(jnp.int32, (1, POOL_BUF, D_POOL), 1)
    grp3 = lax.broadcasted_iota(jnp.int32, (1, POOL_BUF, D_POOL), 2) // POOL_CG
    wlen3 = jnp.where(grp3 == 0, 2, jnp.where(grp3 == 1, 4, jnp.where(grp3 == 2, 8, 16)))
    tail = jnp.sum(jnp.where(ri >= POOL_BUF + 1 - wlen3, buf, 0.0), axis=1)
    grp = lax.broadcasted_iota(jnp.int32, (bt, D_POOL), 1) // POOL_CG
    wlen = jnp.where(grp == 0, 2, jnp.where(grp == 1, 4, jnp.where(grp == 2, 8, 16)))
    cnt = jnp.minimum(PAST_LEN + 1, wlen).astype(F32)
    pooled = (tail + u) / cnt - u
    a_out = _mm(pooled, pwbd_ref[...]) * pscale_ref[...] * _silu(g_pool)
    poolo_ref[:, pl.ds(0, POOL_BUF - 1), :] = pool_ref[:, pl.ds(1, POOL_BUF - 1), :]
    poolo_ref[:, pl.ds(POOL_BUF - 1, 1), :] = u[:, None, :]

    zq = jnp.dot(hb, win_ref[:, OFF_QKV:OFF_QKV + 768], preferred_element_type=F32)
    cos = cos_ref[...]
    sin = sin_ref[...]
    q = _qk_norm_rope(zq[:, 0:512], qg_ref[...], cos, sin, ones_bd) * ATTN_SCALE
    k_new = _qk_norm_rope(zq[:, 512:640], kg_ref[...], cos, sin, ones_bd)
    v_new = zq[:, 640:768]
    si = lax.broadcasted_iota(jnp.int32, (1, WINDOW, D_KV), 1)
    k_win = jnp.where(si == WINDOW - 1, k_new[:, None, :], pltpu.roll(kc_ref[...], WINDOW - 1, 1))
    v_win = jnp.where(si == WINDOW - 1, v_new[:, None, :], pltpu.roll(vc_ref[...], WINDOW - 1, 1))
    ko_ref[...] = k_win
    vo_ref[...] = v_win
    lane_g = lax.broadcasted_iota(jnp.int32, (bt, D_KV), 1) // HEAD_DIM
    q_rows = []
    for h in range(N_HEADS):
        qh = q[:, h * HEAD_DIM:(h + 1) * HEAD_DIM]
        q_rows.append(jnp.where(lane_g == h // Q_PER_KV, jnp.concatenate([qh, qh], axis=-1), 0.0))
    qe = jnp.stack(q_rows, axis=1)
    s = jnp.einsum('bhl,bsl->bhs', _bf(qe), _bf(k_win), preferred_element_type=F32)
    hi = lax.broadcasted_iota(jnp.int32, (1, N_HEADS, 1), 1)
    sink = jnp.zeros((1, N_HEADS, 1), F32)
    for h in range(N_HEADS):
        sink = jnp.where(hi == h, sinks_ref[h], sink)
    m = jnp.maximum(jnp.max(s, axis=-1, keepdims=True), sink)
    p = jnp.exp(s - m)
    denom = jnp.sum(p, axis=-1, keepdims=True) + jnp.exp(sink - m)
    o = jnp.einsum('bhs,bsl->bhl', _bf(p), _bf(v_win), preferred_element_type=F32) / denom
    b_att = jnp.concatenate(
        [o[:, h, (h // Q_PER_KV) * HEAD_DIM:(h // Q_PER_KV + 1) * HEAD_DIM] for h in range(N_HEADS)],
        axis=-1)
    g_attn = jnp.dot(hb, win_ref[:, OFF_GATTN:OFF_GATTN + 512], preferred_element_type=F32)
    b_out = b_att * _silu(g_attn)

    rin = jnp.dot(hb, win_ref[:, OFF_RIN:OFF_RIN + D_SHIFT_PAD], preferred_element_type=F32)
    xs = rin + (shp_ref[...] - rin) * mu_ref[...]
    sho_ref[...] = rin[:, 0:D_SHIFT]
    rwv = rwv_ref[...]
    r, k2, vv, kk, a, logw = _rwkv_token_params(xs, rwv, lora_ref[...], ones_bd)
    st = s_ref[...]
    kk4 = _heads_rows(kk)[:, :, None, :]
    w4 = _heads_rows(jnp.exp(logw))[:, :, None, :]
    b4 = _heads_rows(kk * a)[:, :, None, :]
    k4 = _heads_rows(k2)[:, :, None, :]
    r4 = _heads_rows(r)[:, :, None, :]
    v_col = _heads_rows(vv)[:, :, :, None]
    sa = jnp.sum(st * kk4, axis=-1, keepdims=True)
    st = st * w4 - sa * b4 + v_col * k4
    so_ref[...] = st
    o4 = jnp.sum(st * r4, axis=-1)
    o_wkv = jnp.concatenate([o4[:, h, :] for h in range(RWKV_HEADS)], axis=-1)
    g_rwkv = jnp.dot(hb, win_ref[:, OFF_GRWKV:OFF_GRWKV + 256], preferred_element_type=F32)
    c_out = _rwkv_finish(o_wkv, r, k2, vv, rwv, ones_bd) * _silu(g_rwkv)

    mix = jnp.concatenate([_bf(a_out), _bf(b_out), _bf(c_out)], axis=-1)
    y_ref[...] = x + jnp.dot(mix, wout_ref[...], preferred_element_type=F32)


def _sample_layer(x, pool, kc, vc, shift, wkv, cos, sin, p, bt):
    nb = x.shape[0]
    assert PAST_LEN >= WINDOW and nb % bt == 0
    row2 = lambda i: (i, 0)
    row3 = lambda i: (i, 0, 0)
    row4 = lambda i: (i, 0, 0, 0)
    in_specs = [
        pl.BlockSpec(memory_space=pltpu.SMEM),
        pl.BlockSpec((bt, D_MODEL), row2),
        pl.BlockSpec((bt, POOL_BUF, D_POOL), row3),
        pl.BlockSpec((bt, WINDOW, D_KV), row3),
        pl.BlockSpec((bt, WINDOW, D_KV), row3),
        pl.BlockSpec((bt, D_SHIFT_PAD), row2),
        pl.BlockSpec((bt, RWKV_HEADS, RWKV_HEAD, RWKV_HEAD), row4),
        _const_spec((1, 128)),
        _const_spec((1, 128)),
        _const_spec((1, D_MODEL)),
        _const_spec((D_MODEL, D_IN_PAD)),
        _const_spec((D_MODEL, D_MODEL)),
        _const_spec((D_POOL, D_POOL)),
        _const_spec((128, 512)),
        _const_spec((1, D_POOL)),
        _const_spec((1, D_ATTN)),
        _const_spec((1, D_KV)),
        _const_spec((1, D_SHIFT_PAD)),
        _const_spec((8, D_RWKV)),
    ]
    out_shape = (
        jax.ShapeDtypeStruct((nb, D_MODEL), F32),
        jax.ShapeDtypeStruct((nb, POOL_BUF, D_POOL), F32),
        jax.ShapeDtypeStruct((nb, WINDOW, D_KV), F32),
        jax.ShapeDtypeStruct((nb, WINDOW, D_KV), F32),
        jax.ShapeDtypeStruct((nb, D_SHIFT), F32),
        jax.ShapeDtypeStruct((nb, RWKV_HEADS, RWKV_HEAD, RWKV_HEAD), F32),
    )
    out_specs = (
        pl.BlockSpec((bt, D_MODEL), row2),
        pl.BlockSpec((bt, POOL_BUF, D_POOL), row3),
        pl.BlockSpec((bt, WINDOW, D_KV), row3),
        pl.BlockSpec((bt, WINDOW, D_KV), row3),
        pl.BlockSpec((bt, D_SHIFT), row2),
        pl.BlockSpec((bt, RWKV_HEADS, RWKV_HEAD, RWKV_HEAD), row4),
    )
    return pl.pallas_call(
        functools.partial(_sample_kernel, bt=bt),
        grid=(nb // bt,),
        in_specs=in_specs,
        out_specs=out_specs,
        out_shape=out_shape,
        compiler_params=pltpu.CompilerParams(
            dimension_semantics=("parallel",),
            vmem_limit_bytes=VMEM_LIMIT_BYTES),
        name="sample_layer",
    )(p['sinks'], x, pool, kc, vc, shift, wkv, cos, sin, p['norm_g'], p['w_in'], p['w_out'],
      p['pool_wbd'], p['lora_w'], p['pool_scale'], p['q_g'], p['k_g'], p['mu'], p['rwv'])


def _rope_tables(pos):
    half = HEAD_DIM // 2
    freqs = ROPE_THETA ** (-jnp.arange(half, dtype=F32) / half)
    ang = pos.astype(F32)[:, None] * freqs[None, :]
    c = jnp.cos(ang)
    s = jnp.sin(ang)
    cos = jnp.concatenate([c, c, c, c], axis=-1)
    sin = jnp.concatenate([-s, s, -s, s], axis=-1)
    return cos, sin


def _layer_params(l, norm_g, w_in, w_out, pool_w, pool_scale, q_norm_g, k_norm_g, attn_sinks,
                  rwkv_mu, rwkv_w0, rwkv_w_up, rwkv_a0, rwkv_a_up, rwkv_k_k, rwkv_k_a, rwkv_r_k,
                  rwkv_ln_g, rwkv_ln_b):
    d_rin_end = OFF_RIN + D_SHIFT
    wi = w_in[l]
    w_in_p = jnp.concatenate(
        [wi[:, :d_rin_end], jnp.zeros((D_MODEL, D_SHIFT_PAD - D_SHIFT), wi.dtype), wi[:, d_rin_end:]],
        axis=1).astype(BF16)
    pw = pool_w[l]
    bd = jnp.zeros((D_POOL, D_POOL), F32)
    for g in range(len(POOL_WINDOWS)):
        bd = bd.at[g * POOL_CG:(g + 1) * POOL_CG, g * POOL_CG:(g + 1) * POOL_CG].set(pw[g])
    lora = jnp.zeros((128, 512), F32)
    lora = lora.at[0:LORA, 0:256].set(rwkv_w_up[l])
    lora = lora.at[LORA:2 * LORA, 256:512].set(rwkv_a_up[l])
    rwv = jnp.stack([rwkv_w0[l], rwkv_a0[l], rwkv_k_k[l], rwkv_k_a[l], rwkv_r_k[l],
                     rwkv_ln_g[l], rwkv_ln_b[l], jnp.zeros((D_RWKV,), F32)], axis=0)
    return {
        'sinks': attn_sinks[l].astype(F32),
        'norm_g': norm_g[l][None, :],
        'w_in': w_in_p,
        'w_out': w_out[l].astype(BF16),
        'pool_wbd': bd.astype(BF16),
        'lora_w': lora.astype(BF16),
        'pool_scale': pool_scale[l][None, :],
        'q_g': jnp.tile(q_norm_g[l], N_HEADS)[None, :],
        'k_g': jnp.tile(k_norm_g[l], N_KV)[None, :],
        'mu': jnp.pad(rwkv_mu[l], (0, D_SHIFT_PAD - D_SHIFT))[None, :],
        'rwv': rwv,
    }


def kernel(x_prompt, x_sample, state_pool, cache_swa_k, cache_swa_v, state_rwkv_shift, state_rwkv_wkv, norm_g, w_in, w_out, pool_w, pool_scale, q_norm_g, k_norm_g, attn_sinks, rwkv_mu, rwkv_w0, rwkv_w_up, rwkv_a0, rwkv_a_up, rwkv_k_k, rwkv_k_a, rwkv_r_k, rwkv_ln_g, rwkv_ln_b):
    depth = w_in.shape[0]
    bsz, seq, _ = x_prompt.shape
    tb = 256 if seq % 256 == 0 else BLOCK
    nb = x_sample.shape[0]
    bt = 16 if nb % 16 == 0 else nb
    cos_p, sin_p = _rope_tables(jnp.arange(seq))
    cos_s, sin_s = _rope_tables(jnp.full((1,), PAST_LEN))
    yp = x_prompt
    ys = x_sample.reshape(nb, D_MODEL)
    outs_p = [[] for _ in range(5)]
    outs_s = [[] for _ in range(5)]
    for l in range(depth):
        p = _layer_params(l, norm_g, w_in, w_out, pool_w, pool_scale, q_norm_g, k_norm_g,
                          attn_sinks, rwkv_mu, rwkv_w0, rwkv_w_up, rwkv_a0, rwkv_a_up,
                          rwkv_k_k, rwkv_k_a, rwkv_r_k, rwkv_ln_g, rwkv_ln_b)
        yp, pool_n, k_n, v_n, sh_n, wkv_n = _prompt_layer(yp, cos_p, sin_p, p, tb)
        outs_p[0].append(pool_n)
        outs_p[1].append(k_n.reshape(bsz, WINDOW, N_KV, HEAD_DIM))
        outs_p[2].append(v_n.reshape(bsz, WINDOW, N_KV, HEAD_DIM))
        outs_p[3].append(sh_n.reshape(bsz, D_SHIFT))
        outs_p[4].append(wkv_n.reshape(bsz, RWKV_HEADS, RWKV_HEAD, RWKV_HEAD))
        ys, pool_n, k_n, v_n, sh_n, wkv_n = _sample_layer(
            ys, state_pool[l],
            cache_swa_k[l].reshape(nb, WINDOW, D_KV), cache_swa_v[l].reshape(nb, WINDOW, D_KV),
            jnp.pad(state_rwkv_shift[l], ((0, 0), (0, D_SHIFT_PAD - D_SHIFT))),
            state_rwkv_wkv[l], cos_s, sin_s, p, bt)
        outs_s[0].append(pool_n)
        outs_s[1].append(k_n.reshape(nb, WINDOW, N_KV, HEAD_DIM))
        outs_s[2].append(v_n.reshape(nb, WINDOW, N_KV, HEAD_DIM))
        outs_s[3].append(sh_n)
        outs_s[4].append(wkv_n)
    res = [yp, ys.reshape(nb, 1, D_MODEL)]
    for a, b in zip(outs_p, outs_s):
        res.append(jnp.stack(a))
        res.append(jnp.stack(b))
    return tuple(res)
```

```python
import functools

import numpy as np
import jax
import jax.numpy as jnp
from jax import lax
from jax.experimental import pallas as pl
from jax.experimental.pallas import tpu as pltpu

D_MODEL = 1024
D_POOL = 256
POOL_WINDOWS = (2, 4, 8, 16)
POOL_CG = 64
POOL_BUF = 15
HEAD_DIM = 64
D_ATTN = 512
N_HEADS = 8
N_KV = 2
Q_PER_KV = 4
D_KV = 128
WINDOW = 128
BLOCK = 128
ROPE_THETA = 10000.0
QK_EPS = 1e-6
D_RWKV = 256
RWKV_HEAD = 64
RWKV_HEADS = 4
LORA = 32
D_SHIFT = 832
D_SHIFT_PAD = 896
GN_EPS = 64e-5
NORM_EPS = 1e-6
PAST_LEN = 16384
ATTN_SCALE = HEAD_DIM ** -0.5

OFF_POOL = 0
OFF_QKV = 512
OFF_GATTN = 1280
OFF_RIN = 1792
OFF_GRWKV = 2688
D_IN_PAD = 2944

CHUNK = 64
PROMPT_LEAD = 8
NEG = -1e30
F32 = jnp.float32
BF16 = jnp.bfloat16

VMEM_LIMIT_BYTES = 56 * 1024 * 1024


def _bf(x):
    return x.astype(BF16)


def _mm(x, y):
    return jnp.dot(_bf(x), _bf(y), preferred_element_type=F32)


def _bnt(x, y):
    return jnp.einsum('bik,bjk->bij', _bf(x), _bf(y), preferred_element_type=F32)


def _bnn(x, y):
    return jnp.einsum('bik,bkj->bij', _bf(x), _bf(y), preferred_element_type=F32)


def _btn(x, y):
    return jnp.einsum('bli,blj->bij', _bf(x), _bf(y), preferred_element_type=F32)


def _sigmoid(x):
    return 1.0 / (1.0 + jnp.exp(-x))


def _silu(x):
    return x * _sigmoid(x)


def _softplus(x):
    return jnp.maximum(x, 0.0) + jnp.log(1.0 + jnp.exp(-jnp.abs(x)))


def _split2(x):
    hi = _bf(x)
    lo = _bf(x - hi.astype(F32))
    return hi, lo


def _split3(x):
    h1 = _bf(x)
    r1 = x - h1.astype(F32)
    h2 = _bf(r1)
    h3 = _bf(r1 - h2.astype(F32))
    return h1, h2, h3


def _block_ones(n, seg):
    r = lax.broadcasted_iota(jnp.int32, (n, n), 0) // seg
    c = lax.broadcasted_iota(jnp.int32, (n, n), 1) // seg
    return jnp.where(r == c, 1.0, 0.0).astype(BF16)


def _seg_sum(x, ones_bd):
    w = x.shape[-1]
    step = min(w, ones_bd.shape[0])
    outs = []
    for c0 in range(0, w, step):
        hi, lo = _split2(x[:, c0:c0 + step])
        bd = ones_bd[:step, :step]
        outs.append(jnp.dot(hi, bd, preferred_element_type=F32)
                    + jnp.dot(lo, bd, preferred_element_type=F32))
    return outs[0] if len(outs) == 1 else jnp.concatenate(outs, axis=-1)


def _rmsnorm_rows(x, g):
    ms = jnp.mean(x * x, axis=-1, keepdims=True)
    return x * lax.rsqrt(ms + NORM_EPS) * g


def _swap_halves(x):
    w = x.shape[-1]
    lane = lax.broadcasted_iota(jnp.int32, x.shape, x.ndim - 1)
    fwd = pltpu.roll(x, w - HEAD_DIM // 2, x.ndim - 1)
    bwd = pltpu.roll(x, HEAD_DIM // 2, x.ndim - 1)
    return jnp.where(lane % HEAD_DIM < HEAD_DIM // 2, fwd, bwd)


def _qk_norm_rope(x, g, cos, sin_signed, ones_bd):
    ms = _seg_sum(x * x, ones_bd) * (1.0 / HEAD_DIM)
    xn = x * lax.rsqrt(ms + QK_EPS) * g
    reps = x.shape[-1] // cos.shape[-1]
    if reps > 1:
        cos = jnp.concatenate([cos] * reps, axis=-1)
        sin_signed = jnp.concatenate([sin_signed] * reps, axis=-1)
    return xn * cos + _swap_halves(xn) * sin_signed


def _rwkv_token_params(xs, rwv, lora_w, ones_bd):
    r = xs[:, 0:256]
    k = xs[:, 256:512]
    v = xs[:, 512:768]
    lo = xs[:, 768:896]
    lane = lax.broadcasted_iota(jnp.int32, lo.shape, 1)
    lo_in = jnp.where(lane < LORA, jnp.tanh(lo), lo)
    pre = _mm(lo_in, lora_w)
    w_pre = rwv[0:1, :] + pre[:, 0:256]
    a_pre = rwv[1:2, :] + pre[:, 256:512]
    w_log = -_softplus(-w_pre) - 0.5
    logw = -jnp.exp(w_log)
    a = _sigmoid(a_pre)
    kk = k * rwv[2:3, :]
    nrm = jnp.sqrt(_seg_sum(kk * kk, ones_bd))
    kk = kk / jnp.maximum(nrm, 1e-12)
    k2 = k * (1.0 + (a - 1.0) * rwv[3:4, :])
    return r, k2, v, kk, a, logw


def _rwkv_finish(o, r, k2, v, rwv, ones_bd):
    inv_n = 1.0 / RWKV_HEAD
    mu_o = _seg_sum(o, ones_bd) * inv_n
    d = o - mu_o
    var = _seg_sum(d * d, ones_bd) * inv_n
    on = d * lax.rsqrt(var + GN_EPS) * rwv[5:6, :] + rwv[6:7, :]
    bonus = _seg_sum(r * k2 * rwv[4:5, :], ones_bd) * v
    return on + bonus


def _run_interleaved(programs, lead):
    live = list(programs)
    started = 0
    step = 0
    while live:
        if started < len(live) and step >= started * lead:
            started += 1
        for prog in list(live[:started]):
            try:
                next(prog)
            except StopIteration:
                live.remove(prog)
                started -= 1
        step += 1


def _prompt_kernel(sinks_ref, x_ref, cos_ref, sin_ref, ng_ref, win_ref, wout_ref, pwbd_ref,
                   lora_ref, pscale_ref, qg_ref, kg_ref, mu_ref, rwv_ref,
                   y_ref, pool_ref, ko_ref, vo_ref, sh_ref, wkv_ref,
                   pext_all, kprev_all, vprev_all, rext_all, s_all, *, tb, rows, lead):
    t = pl.program_id(1)

    @pl.when(t == 0)
    def _init():
        pext_all[...] = jnp.zeros(pext_all.shape, F32)
        kprev_all[...] = jnp.zeros(kprev_all.shape, BF16)
        vprev_all[...] = jnp.zeros(vprev_all.shape, BF16)
        rext_all[:, pl.ds(0, 8), :] = jnp.zeros((rows, 8, D_SHIFT_PAD), F32)
        s_all[...] = jnp.zeros(s_all.shape, F32)

    ones_bd = _block_ones(256, HEAD_DIM)
    programs = [
        _prompt_tile(i, t, ones_bd, sinks_ref, x_ref, cos_ref, sin_ref, ng_ref, win_ref, wout_ref,
                     pwbd_ref, lora_ref, pscale_ref, qg_ref, kg_ref, mu_ref, rwv_ref,
                     y_ref, pool_ref, ko_ref, vo_ref, sh_ref, wkv_ref,
                     pext_all.at[i], kprev_all.at[i], vprev_all.at[i], rext_all.at[i], s_all.at[i],
                     tb=tb)
        for i in range(rows)]
    _run_interleaved(programs, lead)


def _prompt_tile(i, t, ones_bd, sinks_ref, x_ref, cos_ref, sin_ref, ng_ref, win_ref, wout_ref,
                 pwbd_ref, lora_ref, pscale_ref, qg_ref, kg_ref, mu_ref, rwv_ref,
                 y_ref, pool_ref, ko_ref, vo_ref, sh_ref, wkv_ref,
                 pext, kprev, vprev, rext, s_ref, *, tb):
    nqb = tb // BLOCK
    nch = tb // CHUNK
    x = x_ref[i]
    hb = _bf(_rmsnorm_rows(x, ng_ref[...]))
    yield

    zp = jnp.dot(hb, win_ref[:, OFF_POOL:OFF_POOL + 512], preferred_element_type=F32)
    yield
    u = zp[:, 0:256]
    g_pool = zp[:, 256:512]
    pext[pl.ds(24, tb), :] = u
    n_ext = tb + 16
    wins = []
    for sh in (1, 2, 4, 8):
        cur = pext[pl.ds(8, n_ext), :] + pext[pl.ds(8 - sh, n_ext), :]
        pext[pl.ds(8, n_ext), :] = cur
        wins.append(cur[16:16 + tb, :])
    pext[pl.ds(8, 16), :] = u[tb - 16:tb, :]
    lane = lax.broadcasted_iota(jnp.int32, (tb, D_POOL), 1)
    row = lax.broadcasted_iota(jnp.int32, (tb, D_POOL), 0)
    grp = lane // POOL_CG
    win_sum = jnp.where(grp == 0, wins[0], jnp.where(grp == 1, wins[1],
                        jnp.where(grp == 2, wins[2], wins[3])))
    wlen = jnp.where(grp == 0, 2, jnp.where(grp == 1, 4, jnp.where(grp == 2, 8, 16)))
    cnt = jnp.minimum(t * tb + row + 1, wlen).astype(F32)
    pooled = win_sum / cnt - u
    a_out = _mm(pooled, pwbd_ref[...]) * pscale_ref[...] * _silu(g_pool)
    pool_ref[i] = u[tb - POOL_BUF:tb, :]
    yield

    zq = jnp.dot(hb, win_ref[:, OFF_QKV:OFF_QKV + 768], preferred_element_type=F32)
    yield
    cos = cos_ref[...]
    sin = sin_ref[...]
    q = _qk_norm_rope(zq[:, 0:512], qg_ref[...], cos, sin, ones_bd) * ATTN_SCALE
    yield
    k = _qk_norm_rope(zq[:, 512:640], kg_ref[...], cos, sin, ones_bd)
    v = zq[:, 640:768]
    qb = _bf(q)
    kb = _bf(k)
    vb = _bf(v)
    ko_ref[i] = k[tb - WINDOW:tb, :]
    vo_ref[i] = v[tb - WINDOW:tb, :]

    nrow = Q_PER_KV * BLOCK
    qi = lax.broadcasted_iota(jnp.int32, (nrow, 2 * BLOCK), 0) % BLOCK
    kj = lax.broadcasted_iota(jnp.int32, (nrow, 2 * BLOCK), 1)
    band = jnp.where(kj > qi, jnp.where(kj <= qi + BLOCK, 0.0, NEG), NEG)
    first_lo = jnp.where(t == 0, BLOCK, 0)
    band_first = jnp.where(kj >= first_lo, band, NEG)
    hrow = lax.broadcasted_iota(jnp.int32, (nrow, 1), 0) // BLOCK

    b_blocks = []
    for j in range(nqb):
        rs = slice(j * BLOCK, (j + 1) * BLOCK)
        if j == 0:
            kx = jnp.concatenate([kprev[...], kb[rs]], axis=0)
            vx = jnp.concatenate([vprev[...], vb[rs]], axis=0)
            valid = band_first
        else:
            kx = kb[(j - 1) * BLOCK:(j + 1) * BLOCK]
            vx = vb[(j - 1) * BLOCK:(j + 1) * BLOCK]
            valid = band
        heads = []
        for g in range(N_KV):
            kxg = kx[:, g * HEAD_DIM:(g + 1) * HEAD_DIM]
            vxg = vx[:, g * HEAD_DIM:(g + 1) * HEAD_DIM]
            qs = jnp.concatenate(
                [qb[rs, (g * Q_PER_KV + i) * HEAD_DIM:(g * Q_PER_KV + i + 1) * HEAD_DIM]
                 for i in range(Q_PER_KV)], axis=0)
            s = lax.dot_general(qs, kxg, (((1,), (1,)), ((), ())),
                                preferred_element_type=F32)
            s = s + valid
            sink = jnp.where(hrow == 0, sinks_ref[g * Q_PER_KV],
                             jnp.where(hrow == 1, sinks_ref[g * Q_PER_KV + 1],
                                       jnp.where(hrow == 2, sinks_ref[g * Q_PER_KV + 2],
                                                 sinks_ref[g * Q_PER_KV + 3])))
            m = jnp.maximum(jnp.max(s, axis=-1, keepdims=True), sink)
            p = jnp.exp(s - m)
            denom = jnp.sum(p, axis=-1, keepdims=True) + jnp.exp(sink - m)
            o = jnp.dot(_bf(p), vxg, preferred_element_type=F32) / denom
            for hq in range(Q_PER_KV):
                heads.append(o[hq * BLOCK:(hq + 1) * BLOCK, :])
            yield
        b_blocks.append(jnp.concatenate(heads, axis=-1))
    b_att = b_blocks[0] if nqb == 1 else jnp.concatenate(b_blocks, axis=0)
    kprev[...] = kb[tb - BLOCK:tb, :]
    vprev[...] = vb[tb - BLOCK:tb, :]

    g_attn = jnp.dot(hb, win_ref[:, OFF_GATTN:OFF_GATTN + 512], preferred_element_type=F32)
    yield
    b_out = b_att * _silu(g_attn)

    rin = jnp.dot(hb, win_ref[:, OFF_RIN:OFF_RIN + D_SHIFT_PAD], preferred_element_type=F32)
    yield
    rext[pl.ds(8, tb), :] = rin
    prev = rext[pl.ds(7, tb), :]
    rext[pl.ds(7, 1), :] = rin[tb - 1:tb, :]
    xs = rin + (prev - rin) * mu_ref[...]
    sh_ref[i] = rin[tb - 1:tb, 0:D_SHIFT]

    rwv = rwv_ref[...]
    r, k2, vv, kk, a, logw = _rwkv_token_params(xs, rwv, lora_ref[...], ones_bd)
    yield

    ti = lax.broadcasted_iota(jnp.int32, (tb, tb), 0)
    tj = lax.broadcasted_iota(jnp.int32, (tb, tb), 1)
    tril_blk = jnp.where(ti // CHUNK == tj // CHUNK, jnp.where(tj <= ti, 1.0, 0.0), 0.0).astype(BF16)
    cum = None
    for piece in _split3(logw):
        c = jnp.dot(tril_blk, piece, preferred_element_type=F32)
        cum = c if cum is None else cum + c
    cum3 = cum.reshape(nch, CHUNK, D_RWKV)
    cum_last = jnp.broadcast_to(cum3[:, CHUNK - 1:CHUNK, :], cum3.shape).reshape(tb, D_RWKV)
    w_inc = jnp.exp(cum)
    w_exc = jnp.exp(cum - logw)
    w_inv = jnp.exp(-cum)
    rel = jnp.exp(cum_last - cum)
    w_last = jnp.exp(cum_last)
    bvec = kk * a

    def stack(z):
        parts = [z[:, h * RWKV_HEAD:(h + 1) * RWKV_HEAD].reshape(nch, 1, CHUNK, RWKV_HEAD)
                 for h in range(RWKV_HEADS)]
        return jnp.concatenate(parts, axis=1).reshape(nch * RWKV_HEADS, CHUNK, RWKV_HEAD)

    kap = stack(kk * w_exc)
    bt = stack(bvec * w_inv)
    kt = stack(k2 * w_inv)
    rt = stack(r * w_inc)
    bt_l = stack(bvec * rel)
    kt_l = stack(k2 * rel)
    vs = stack(vv)
    wl = stack(w_last)
    yield

    li = lax.broadcasted_iota(jnp.int32, (1, CHUNK, 2 * CHUNK), 1)
    lj = lax.broadcasted_iota(jnp.int32, (1, CHUNK, 2 * CHUNK), 2)
    lj = jnp.where(lj >= CHUNK, lj - CHUNK, lj)
    btkt = jnp.concatenate([bt, kt], axis=1)
    a_bk = jnp.where(lj < li, _bnt(kap, btkt), 0.0)
    p_bk = jnp.where(lj <= li, _bnt(rt, btkt), 0.0)
    a_b = a_bk[:, :, 0:CHUNK]
    a_k = a_bk[:, :, CHUNK:2 * CHUNK]
    xcat = jnp.concatenate([kap, -_bnn(a_k, vs)], axis=2)
    pw = -a_b
    yield
    span = 1
    while 2 * span < CHUNK:
        both = _bnn(pw, jnp.concatenate([xcat, pw], axis=2))
        xcat = xcat + both[:, :, 0:2 * RWKV_HEAD]
        pw = both[:, :, 2 * RWKV_HEAD:2 * RWKV_HEAD + CHUNK]
        span *= 2
        yield
    xcat = xcat + _bnn(pw, xcat)
    yield
    xswap = pltpu.roll(xcat, RWKV_HEAD, 2)
    rhs = jnp.concatenate([xswap, jnp.concatenate([vs, jnp.zeros_like(vs)], axis=2)], axis=1)
    o0_pk = _bnn(p_bk, rhs)
    o0 = o0_pk[:, :, 0:RWKV_HEAD]
    rp = rt - o0_pk[:, :, RWKV_HEAD:2 * RWKV_HEAD]
    tn = _btn(bt_l, xcat)
    eye = jnp.where(lax.broadcasted_iota(jnp.int32, (1, CHUNK, CHUNK), 1)
                    == lax.broadcasted_iota(jnp.int32, (1, CHUNK, CHUNK), 2), 1.0, 0.0)
    m_t = eye * wl - tn[:, :, 0:RWKV_HEAD]
    c_t = tn[:, :, RWKV_HEAD:2 * RWKV_HEAD] + _btn(kt_l, vs)
    rp_mt = jnp.concatenate([rp, m_t], axis=1)
    yield

    st_cur = s_ref[...]
    o_chunks = []
    for c in range(nch):
        sl = slice(c * RWKV_HEADS, (c + 1) * RWKV_HEADS)
        both = _bnn(rp_mt[sl], st_cur)
        o_chunks.append(both[:, 0:CHUNK, :] + o0[sl])
        st_cur = both[:, CHUNK:2 * CHUNK, :] + c_t[sl]
        yield
    s_ref[...] = st_cur
    for h in range(RWKV_HEADS):
        wkv_ref[i, pl.ds(h * RWKV_HEAD, RWKV_HEAD), :] = st_cur[h].T

    o_heads = []
    for h in range(RWKV_HEADS):
        o_heads.append(jnp.concatenate([oc[h] for oc in o_chunks], axis=0))
    o_wkv = jnp.concatenate(o_heads, axis=-1)
    g_rwkv = jnp.dot(hb, win_ref[:, OFF_GRWKV:OFF_GRWKV + 256], preferred_element_type=F32)
    yield
    c_out = _rwkv_finish(o_wkv, r, k2, vv, rwv, ones_bd) * _silu(g_rwkv)
    yield

    mix = jnp.concatenate([_bf(a_out), _bf(b_out), _bf(c_out)], axis=-1)
    y_ref[i] = x + jnp.dot(mix, wout_ref[...], preferred_element_type=F32)


def _const_spec(shape):
    nd = len(shape)
    return pl.BlockSpec(shape, lambda *_: (0,) * nd)


def _prompt_layer(x, cos, sin, p, tb, rows, lead):
    bsz, seq, _ = x.shape
    nt = seq // tb
    grid = (bsz // rows, nt)
    in_specs = [
        pl.BlockSpec(memory_space=pltpu.SMEM),
        pl.BlockSpec((rows, tb, D_MODEL), lambda b, t: (b, t, 0)),
        pl.BlockSpec((tb, 128), lambda b, t: (t, 0)),
        pl.BlockSpec((tb, 128), lambda b, t: (t, 0)),
        _const_spec((1, D_MODEL)),
        _const_spec((D_MODEL, D_IN_PAD)),
        _const_spec((D_MODEL, D_MODEL)),
        _const_spec((D_POOL, D_POOL)),
        _const_spec((128, 512)),
        _const_spec((1, D_POOL)),
        _const_spec((1, D_ATTN)),
        _const_spec((1, D_KV)),
        _const_spec((1, D_SHIFT_PAD)),
        _const_spec((8, D_RWKV)),
    ]
    out_shape = (
        jax.ShapeDtypeStruct((bsz, seq, D_MODEL), F32),
        jax.ShapeDtypeStruct((bsz, POOL_BUF, D_POOL), F32),
        jax.ShapeDtypeStruct((bsz, WINDOW, D_KV), F32),
        jax.ShapeDtypeStruct((bsz, WINDOW, D_KV), F32),
        jax.ShapeDtypeStruct((bsz, 1, D_SHIFT), F32),
        jax.ShapeDtypeStruct((bsz, RWKV_HEADS * RWKV_HEAD, RWKV_HEAD), F32),
    )
    out_specs = (
        pl.BlockSpec((rows, tb, D_MODEL), lambda b, t: (b, t, 0)),
        pl.BlockSpec((rows, POOL_BUF, D_POOL), lambda b, t: (b, 0, 0)),
        pl.BlockSpec((rows, WINDOW, D_KV), lambda b, t: (b, 0, 0)),
        pl.BlockSpec((rows, WINDOW, D_KV), lambda b, t: (b, 0, 0)),
        pl.BlockSpec((rows, 1, D_SHIFT), lambda b, t: (b, 0, 0)),
        pl.BlockSpec((rows, RWKV_HEADS * RWKV_HEAD, RWKV_HEAD), lambda b, t: (b, 0, 0)),
    )
    scratch = [
        pltpu.VMEM((rows, tb + 24, D_POOL), F32),
        pltpu.VMEM((rows, BLOCK, D_KV), BF16),
        pltpu.VMEM((rows, BLOCK, D_KV), BF16),
        pltpu.VMEM((rows, tb + 8, D_SHIFT_PAD), F32),
        pltpu.VMEM((rows, RWKV_HEADS, RWKV_HEAD, RWKV_HEAD), F32),
    ]
    return pl.pallas_call(
        functools.partial(_prompt_kernel, tb=tb, rows=rows, lead=lead),
        grid=grid,
        in_specs=in_specs,
        out_specs=out_specs,
        out_shape=out_shape,
        scratch_shapes=scratch,
        compiler_params=pltpu.CompilerParams(
            dimension_semantics=("parallel", "arbitrary"),
            vmem_limit_bytes=VMEM_LIMIT_BYTES),
        name="prompt_layer",
    )(p['sinks'], x, cos, sin, p['norm_g'], p['w_in'], p['w_out'], p['pool_wbd'], p['lora_w'],
      p['pool_scale'], p['q_g'], p['k_g'], p['mu'], p['rwv'])


def _heads_rows(z):
    return jnp.stack([z[:, h * RWKV_HEAD:(h + 1) * RWKV_HEAD] for h in range(RWKV_HEADS)], axis=1)


def _sample_kernel(sinks_ref, x_ref, pool_ref, kc_ref, vc_ref, shp_ref, s_ref, cos_ref, sin_ref,
                   ng_ref, win_ref, wout_ref, pwbd_ref, lora_ref, pscale_ref, qg_ref, kg_ref,
                   mu_ref, rwv_ref,
                   y_ref, poolo_ref, ko_ref, vo_ref, sho_ref, so_ref, *, bt):
    ones_bd = _block_ones(256, HEAD_DIM)
    x = x_ref[...]
    hb = _bf(_rmsnorm_rows(x, ng_ref[...]))

    zp = jnp.dot(hb, win_ref[:, OFF_POOL:OFF_POOL + 512], preferred_element_type=F32)
    u = zp[:, 0:256]
    g_pool = zp[:, 256:512]
    buf = pool_ref[...]
    ri = lax.broadcasted_iota(jnp.int32, (1, POOL_BUF, D_POOL), 1)
    grp3 = lax.broadcasted_iota(jnp.int32, (1, POOL_BUF, D_POOL), 2) // POOL_CG
    wlen3 = jnp.where(grp3 == 0, 2, jnp.where(grp3 == 1, 4, jnp.where(grp3 == 2, 8, 16)))
    tail = jnp.sum(jnp.where(ri >= POOL_BUF + 1 - wlen3, buf, 0.0), axis=1)
    grp = lax.broadcasted_iota(jnp.int32, (bt, D_POOL), 1) // POOL_CG
    wlen = jnp.where(grp == 0, 2, jnp.where(grp == 1, 4, jnp.where(grp == 2, 8, 16)))
    cnt = jnp.minimum(PAST_LEN + 1, wlen).astype(F32)
    pooled = (tail + u) / cnt - u
    a_out = _mm(pooled, pwbd_ref[...]) * pscale_ref[...] * _silu(g_pool)
    poolo_ref[:, pl.ds(0, POOL_BUF - 1), :] = pool_ref[:, pl.ds(1, POOL_BUF - 1), :]
    poolo_ref[:, pl.ds(POOL_BUF - 1, 1), :] = u[:, None, :]

    zq = jnp.dot(hb, win_ref[:, OFF_QKV:OFF_QKV + 768], preferred_element_type=F32)
    cos = cos_ref[...]
    sin = sin_ref[...]
    q = _qk_norm_rope(zq[:, 0:512], qg_ref[...], cos, sin, ones_bd) * ATTN_SCALE
    k_new = _qk_norm_rope(zq[:, 512:640], kg_ref[...], cos, sin, ones_bd)
    v_new = zq[:, 640:768]
    si = lax.broadcasted_iota(jnp.int32, (1, WINDOW, D_KV), 1)
    k_win = jnp.where(si == WINDOW - 1, k_new[:, None, :], pltpu.roll(kc_ref[...], WINDOW - 1, 1))
    v_win = jnp.where(si == WINDOW - 1, v_new[:, None, :], pltpu.roll(vc_ref[...], WINDOW - 1, 1))
    ko_ref[...] = k_win
    vo_ref[...] = v_win
    lane_g = lax.broadcasted_iota(jnp.int32, (bt, D_KV), 1) // HEAD_DIM
    q_rows = []
    for h in range(N_HEADS):
        qh = q[:, h * HEAD_DIM:(h + 1) * HEAD_DIM]
        q_rows.append(jnp.where(lane_g == h // Q_PER_KV, jnp.concatenate([qh, qh], axis=-1), 0.0))
    qe = jnp.stack(q_rows, axis=1)
    s = jnp.einsum('bhl,bsl->bhs', _bf(qe), _bf(k_win), preferred_element_type=F32)
    hi = lax.broadcasted_iota(jnp.int32, (1, N_HEADS, 1), 1)
    sink = jnp.zeros((1, N_HEADS, 1), F32)
    for h in range(N_HEADS):
        sink = jnp.where(hi == h, sinks_ref[h], sink)
    m = jnp.maximum(jnp.max(s, axis=-1, keepdims=True), sink)
    p = jnp.exp(s - m)
    denom = jnp.sum(p, axis=-1, keepdims=True) + jnp.exp(sink - m)
    o = jnp.einsum('bhs,bsl->bhl', _bf(p), _bf(v_win), preferred_element_type=F32) / denom
    b_att = jnp.concatenate(
        [o[:, h, (h // Q_PER_KV) * HEAD_DIM:(h // Q_PER_KV + 1) * HEAD_DIM] for h in range(N_HEADS)],
        axis=-1)
    g_attn = jnp.dot(hb, win_ref[:, OFF_GATTN:OFF_GATTN + 512], preferred_element_type=F32)
    b_out = b_att * _silu(g_attn)

    rin = jnp.dot(hb, win_ref[:, OFF_RIN:OFF_RIN + D_SHIFT_PAD], preferred_element_type=F32)
    xs = rin + (shp_ref[...] - rin) * mu_ref[...]
    sho_ref[...] = rin[:, 0:D_SHIFT]
    rwv = rwv_ref[...]
    r, k2, vv, kk, a, logw = _rwkv_token_params(xs, rwv, lora_ref[...], ones_bd)
    st = s_ref[...]
    kk4 = _heads_rows(kk)[:, :, None, :]
    w4 = _heads_rows(jnp.exp(logw))[:, :, None, :]
    b4 = _heads_rows(kk * a)[:, :, None, :]
    k4 = _heads_rows(k2)[:, :, None, :]
    r4 = _heads_rows(r)[:, :, None, :]
    v_col = _heads_rows(vv)[:, :, :, None]
    sa = jnp.sum(st * kk4, axis=-1, keepdims=True)
    st = st * w4 - sa * b4 + v_col * k4
    so_ref[...] = st
    o4 = jnp.sum(st * r4, axis=-1)
    o_wkv = jnp.concatenate([o4[:, h, :] for h in range(RWKV_HEADS)], axis=-1)
    g_rwkv = jnp.dot(hb, win_ref[:, OFF_GRWKV:OFF_GRWKV + 256], preferred_element_type=F32)
    c_out = _rwkv_finish(o_wkv, r, k2, vv, rwv, ones_bd) * _silu(g_rwkv)

    mix = jnp.concatenate([_bf(a_out), _bf(b_out), _bf(c_out)], axis=-1)
    y_ref[...] = x + jnp.dot(mix, wout_ref[...], preferred_element_type=F32)


def _sample_layer(x, pool, kc, vc, shift, wkv, cos, sin, p, bt):
    nb = x.shape[0]
    assert PAST_LEN >= WINDOW and nb % bt == 0
    row2 = lambda i: (i, 0)
    row3 = lambda i: (i, 0, 0)
    row4 = lambda i: (i, 0, 0, 0)
    in_specs = [
        pl.BlockSpec(memory_space=pltpu.SMEM),
        pl.BlockSpec((bt, D_MODEL), row2),
        pl.BlockSpec((bt, POOL_BUF, D_POOL), row3),
        pl.BlockSpec((bt, WINDOW, D_KV), row3),
        pl.BlockSpec((bt, WINDOW, D_KV), row3),
        pl.BlockSpec((bt, D_SHIFT_PAD), row2),
        pl.BlockSpec((bt, RWKV_HEADS, RWKV_HEAD, RWKV_HEAD), row4),
        _const_spec((1, 128)),
        _const_spec((1, 128)),
        _const_spec((1, D_MODEL)),
        _const_spec((D_MODEL, D_IN_PAD)),
        _const_spec((D_MODEL, D_MODEL)),
        _const_spec((D_POOL, D_POOL)),
        _const_spec((128, 512)),
        _const_spec((1, D_POOL)),
        _const_spec((1, D_ATTN)),
        _const_spec((1, D_KV)),
        _const_spec((1, D_SHIFT_PAD)),
        _const_spec((8, D_RWKV)),
    ]
    out_shape = (
        jax.ShapeDtypeStruct((nb, D_MODEL), F32),
        jax.ShapeDtypeStruct((nb, POOL_BUF, D_POOL), F32),
        jax.ShapeDtypeStruct((nb, WINDOW, D_KV), F32),
        jax.ShapeDtypeStruct((nb, WINDOW, D_KV), F32),
        jax.ShapeDtypeStruct((nb, D_SHIFT), F32),
        jax.ShapeDtypeStruct((nb, RWKV_HEADS, RWKV_HEAD, RWKV_HEAD), F32),
    )
    out_specs = (
        pl.BlockSpec((bt, D_MODEL), row2),
        pl.BlockSpec((bt, POOL_BUF, D_POOL), row3),
        pl.BlockSpec((bt, WINDOW, D_KV), row3),
        pl.BlockSpec((bt, WINDOW, D_KV), row3),
        pl.BlockSpec((bt, D_SHIFT), row2),
        pl.BlockSpec((bt, RWKV_HEADS, RWKV_HEAD, RWKV_HEAD), row4),
    )
    return pl.pallas_call(
        functools.partial(_sample_kernel, bt=bt),
        grid=(nb // bt,),
        in_specs=in_specs,
        out_specs=out_specs,
        out_shape=out_shape,
        compiler_params=pltpu.CompilerParams(
            dimension_semantics=("parallel",),
            vmem_limit_bytes=VMEM_LIMIT_BYTES),
        name="sample_layer",
    )(p['sinks'], x, pool, kc, vc, shift, wkv, cos, sin, p['norm_g'], p['w_in'], p['w_out'],
      p['pool_wbd'], p['lora_w'], p['pool_scale'], p['q_g'], p['k_g'], p['mu'], p['rwv'])


def _rope_tables(pos):
    half = HEAD_DIM // 2
    freqs = ROPE_THETA ** (-jnp.arange(half, dtype=F32) / half)
    ang = pos.astype(F32)[:, None] * freqs[None, :]
    c = jnp.cos(ang)
    s = jnp.sin(ang)
    cos = jnp.concatenate([c, c, c, c], axis=-1)
    sin = jnp.concatenate([-s, s, -s, s], axis=-1)
    return cos, sin


def _layer_params(l, norm_g, w_in, w_out, pool_w, pool_scale, q_norm_g, k_norm_g, attn_sinks,
                  rwkv_mu, rwkv_w0, rwkv_w_up, rwkv_a0, rwkv_a_up, rwkv_k_k, rwkv_k_a, rwkv_r_k,
                  rwkv_ln_g, rwkv_ln_b):
    d_rin_end = OFF_RIN + D_SHIFT
    wi = w_in[l]
    w_in_p = jnp.concatenate(
        [wi[:, :d_rin_end], jnp.zeros((D_MODEL, D_SHIFT_PAD - D_SHIFT), wi.dtype), wi[:, d_rin_end:]],
        axis=1).astype(BF16)
    pw = pool_w[l]
    bd = jnp.zeros((D_POOL, D_POOL), F32)
    for g in range(len(POOL_WINDOWS)):
        bd = bd.at[g * POOL_CG:(g + 1) * POOL_CG, g * POOL_CG:(g + 1) * POOL_CG].set(pw[g])
    lora = jnp.zeros((128, 512), F32)
    lora = lora.at[0:LORA, 0:256].set(rwkv_w_up[l])
    lora = lora.at[LORA:2 * LORA, 256:512].set(rwkv_a_up[l])
    rwv = jnp.stack([rwkv_w0[l], rwkv_a0[l], rwkv_k_k[l], rwkv_k_a[l], rwkv_r_k[l],
                     rwkv_ln_g[l], rwkv_ln_b[l], jnp.zeros((D_RWKV,), F32)], axis=0)
    return {
        'sinks': attn_sinks[l].astype(F32),
        'norm_g': norm_g[l][None, :],
        'w_in': w_in_p,
        'w_out': w_out[l].astype(BF16),
        'pool_wbd': bd.astype(BF16),
        'lora_w': lora.astype(BF16),
        'pool_scale': pool_scale[l][None, :],
        'q_g': jnp.tile(q_norm_g[l], N_HEADS)[None, :],
        'k_g': jnp.tile(k_norm_g[l], N_KV)[None, :],
        'mu': jnp.pad(rwkv_mu[l], (0, D_SHIFT_PAD - D_SHIFT))[None, :],
        'rwv': rwv,
    }


def kernel(x_prompt, x_sample, state_pool, cache_swa_k, cache_swa_v, state_rwkv_shift, state_rwkv_wkv, norm_g, w_in, w_out, pool_w, pool_scale, q_norm_g, k_norm_g, attn_sinks, rwkv_mu, rwkv_w0, rwkv_w_up, rwkv_a0, rwkv_a_up, rwkv_k_k, rwkv_k_a, rwkv_r_k, rwkv_ln_g, rwkv_ln_b):
    depth = w_in.shape[0]
    bsz, seq, _ = x_prompt.shape
    tb = 256 if seq % 256 == 0 else BLOCK
    rows = 2 if bsz % 2 == 0 else 1
    nb = x_sample.shape[0]
    bt = 16 if nb % 16 == 0 else nb
    cos_p, sin_p = _rope_tables(jnp.arange(seq))
    cos_s, sin_s = _rope_tables(jnp.full((1,), PAST_LEN))
    yp = x_prompt
    ys = x_sample.reshape(nb, D_MODEL)
    outs_p = [[] for _ in range(5)]
    outs_s = [[] for _ in range(5)]
    for l in range(depth):
        p = _layer_params(l, norm_g, w_in, w_out, pool_w, pool_scale, q_norm_g, k_norm_g,
                          attn_sinks, rwkv_mu, rwkv_w0, rwkv_w_up, rwkv_a0, rwkv_a_up,
                          rwkv_k_k, rwkv_k_a, rwkv_r_k, rwkv_ln_g, rwkv_ln_b)
        yp, pool_n, k_n, v_n, sh_n, wkv_n = _prompt_layer(yp, cos_p, sin_p, p, tb, rows, PROMPT_LEAD)
        outs_p[0].append(pool_n)
        outs_p[1].append(k_n.reshape(bsz, WINDOW, N_KV, HEAD_DIM))
        outs_p[2].append(v_n.reshape(bsz, WINDOW, N_KV, HEAD_DIM))
        outs_p[3].append(sh_n.reshape(bsz, D_SHIFT))
        outs_p[4].append(wkv_n.reshape(bsz, RWKV_HEADS, RWKV_HEAD, RWKV_HEAD))
        ys, pool_n, k_n, v_n, sh_n, wkv_n = _sample_layer(
            ys, state_pool[l],
            cache_swa_k[l].reshape(nb, WINDOW, D_KV), cache_swa_v[l].reshape(nb, WINDOW, D_KV),
            jnp.pad(state_rwkv_shift[l], ((0, 0), (0, D_SHIFT_PAD - D_SHIFT))),
            state_rwkv_wkv[l], cos_s, sin_s, p, bt)
        outs_s[0].append(pool_n)
        outs_s[1].append(k_n.reshape(nb, WINDOW, N_KV, HEAD_DIM))
        outs_s[2].append(v_n.reshape(nb, WINDOW, N_KV, HEAD_DIM))
        outs_s[3].append(sh_n)
        outs_s[4].append(wkv_n)
    res = [yp, ys.reshape(nb, 1, D_MODEL)]
    for a, b in zip(outs_p, outs_s):
        res.append(jnp.stack(a))
        res.append(jnp.stack(b))
    return tuple(res)
```

```python
import functools

import numpy as np
import jax
import jax.numpy as jnp
from jax import lax
from jax.experimental import pallas as pl
from jax.experimental.pallas import tpu as pltpu

D_MODEL = 1024
D_POOL = 256
POOL_WINDOWS = (2, 4, 8, 16)
POOL_CG = 64
POOL_BUF = 15
HEAD_DIM = 64
D_ATTN = 512
N_HEADS = 8
N_KV = 2
Q_PER_KV = 4
D_KV = 128
WINDOW = 128
BLOCK = 128
ROPE_THETA = 10000.0
QK_EPS = 1e-6
D_RWKV = 256
RWKV_HEAD = 64
RWKV_HEADS = 4
LORA = 32
D_SHIFT = 832
D_SHIFT_PAD = 896
GN_EPS = 64e-5
NORM_EPS = 1e-6
PAST_LEN = 16384
ATTN_SCALE = HEAD_DIM ** -0.5
LOG2E = 1.4426950408889634

OFF_POOL = 0
OFF_QKV = 512
OFF_GATTN = 1280
OFF_RIN = 1792
OFF_GRWKV = 2688
D_IN_PAD = 2944

CHUNK = 64
NEG = -1e30
F32 = jnp.float32
BF16 = jnp.bfloat16

VMEM_LIMIT_BYTES = 56 * 1024 * 1024


def _bf(x):
    return x.astype(BF16)


def _mm(x, y):
    return jnp.dot(_bf(x), _bf(y), preferred_element_type=F32)


def _bnt(x, y):
    return jnp.einsum('bik,bjk->bij', _bf(x), _bf(y), preferred_element_type=F32)


def _bnn(x, y):
    return jnp.einsum('bik,bkj->bij', _bf(x), _bf(y), preferred_element_type=F32)


def _btn(x, y):
    return jnp.einsum('bli,blj->bij', _bf(x), _bf(y), preferred_element_type=F32)


def _sigmoid(x):
    return 0.5 + 0.5 * jnp.tanh(0.5 * x)


def _silu(x):
    half = 0.5 * x
    return half + half * jnp.tanh(half)


def _softplus(x):
    return jnp.maximum(x, 0.0) + jnp.log(1.0 + jnp.exp(-jnp.abs(x)))


def _split3(x):
    h1 = _bf(x)
    r1 = x - h1.astype(F32)
    h2 = _bf(r1)
    h3 = _bf(r1 - h2.astype(F32))
    return h1, h2, h3


def _block_ones(n, seg):
    r = lax.broadcasted_iota(jnp.int32, (n, n), 0) // seg
    c = lax.broadcasted_iota(jnp.int32, (n, n), 1) // seg
    return jnp.where(r == c, 1.0, 0.0).astype(BF16)


def _seg_sum(x, ones_bd):
    w = x.shape[-1]
    step = min(w, ones_bd.shape[0])
    outs = [jnp.dot(_bf(x[:, c0:c0 + step]), ones_bd[:step, :step], preferred_element_type=F32)
            for c0 in range(0, w, step)]
    return outs[0] if len(outs) == 1 else jnp.concatenate(outs, axis=-1)


def _rmsnorm_rows(x, g):
    ms = jnp.mean(x * x, axis=-1, keepdims=True)
    return x * lax.rsqrt(ms + NORM_EPS) * g


def _swap_halves(x):
    w = x.shape[-1]
    lane = lax.broadcasted_iota(jnp.int32, x.shape, x.ndim - 1)
    fwd = pltpu.roll(x, w - HEAD_DIM // 2, x.ndim - 1)
    bwd = pltpu.roll(x, HEAD_DIM // 2, x.ndim - 1)
    return jnp.where(lane % HEAD_DIM < HEAD_DIM // 2, fwd, bwd)


def _qk_norm_rope(x, g, cos, sin_signed, ones_bd):
    ms = _seg_sum(x * x, ones_bd) * (1.0 / HEAD_DIM)
    xn = x * lax.rsqrt(ms + QK_EPS) * g
    reps = x.shape[-1] // cos.shape[-1]
    if reps > 1:
        cos = jnp.concatenate([cos] * reps, axis=-1)
        sin_signed = jnp.concatenate([sin_signed] * reps, axis=-1)
    return xn * cos + _swap_halves(xn) * sin_signed


def _rwkv_token_params(xs, rwv, lora_w, ones_bd):
    r = xs[:, 0:256]
    k = xs[:, 256:512]
    v = xs[:, 512:768]
    lo = xs[:, 768:896]
    lane = lax.broadcasted_iota(jnp.int32, lo.shape, 1)
    lo_in = jnp.where(lane < LORA, jnp.tanh(lo), lo)
    pre = _mm(lo_in, lora_w)
    w_pre = rwv[0:1, :] + pre[:, 0:256]
    a_pre = rwv[1:2, :] + pre[:, 256:512]
    w_log = -_softplus(-w_pre) - 0.5
    logw = -jnp.exp(w_log)
    a = _sigmoid(a_pre)
    kk = k * rwv[2:3, :]
    nrm = jnp.sqrt(_seg_sum(kk * kk, ones_bd))
    kk = kk / jnp.maximum(nrm, 1e-12)
    k2 = k * (1.0 + (a - 1.0) * rwv[3:4, :])
    return r, k2, v, kk, a, logw


def _rwkv_finish(o, r, k2, v, rwv, ones_bd):
    inv_n = 1.0 / RWKV_HEAD
    mu_o = _seg_sum(o, ones_bd) * inv_n
    d = o - mu_o
    var = _seg_sum(d * d, ones_bd) * inv_n
    on = d * lax.rsqrt(var + GN_EPS) * rwv[5:6, :] + rwv[6:7, :]
    bonus = _seg_sum(r * k2 * rwv[4:5, :], ones_bd) * v
    return on + bonus


def _run_interleaved(programs):
    state = [[prog, next(prog), 0] for prog in programs]
    last = 'M'
    while state:
        want = 'V' if last == 'M' else 'M'
        order = sorted(state, key=lambda s: s[2])
        pick = next((s for s in order if s[1] == want), None)
        if pick is None:
            pick = next((s for s in order if s[1] == 'X'), order[0])
        if pick[1] != 'X':
            last = pick[1]
        pick[2] += 1
        try:
            pick[1] = next(pick[0])
        except StopIteration:
            state.remove(pick)


def _prompt_kernel(sinks_ref, x_ref, cos_ref, sin_ref, ng_ref, win_ref, wout_ref, pwbd_ref,
                   lora_ref, pscale_ref, qg_ref, kg_ref, mu_ref, rwv_ref,
                   y_ref, pool_ref, ko_ref, vo_ref, sh_ref, wkv_ref,
                   pext_all, kprev_all, vprev_all, rext_all, s_all, *, tb, rows):
    t = pl.program_id(1)

    @pl.when(t == 0)
    def _init():
        pext_all[...] = jnp.zeros(pext_all.shape, F32)
        kprev_all[...] = jnp.zeros(kprev_all.shape, BF16)
        vprev_all[...] = jnp.zeros(vprev_all.shape, BF16)
        rext_all[:, pl.ds(0, 8), :] = jnp.zeros((rows, 8, D_SHIFT_PAD), F32)
        s_all[...] = jnp.zeros(s_all.shape, F32)

    ones_bd = _block_ones(256, HEAD_DIM)
    programs = [
        _prompt_tile(i, t, ones_bd, sinks_ref, x_ref, cos_ref, sin_ref, ng_ref, win_ref, wout_ref,
                     pwbd_ref, lora_ref, pscale_ref, qg_ref, kg_ref, mu_ref, rwv_ref,
                     y_ref, pool_ref, ko_ref, vo_ref, sh_ref, wkv_ref,
                     pext_all.at[i], kprev_all.at[i], vprev_all.at[i], rext_all.at[i], s_all.at[i],
                     tb=tb)
        for i in range(rows)]
    _run_interleaved(programs)


def _prompt_tile(i, t, ones_bd, sinks_ref, x_ref, cos_ref, sin_ref, ng_ref, win_ref, wout_ref,
                 pwbd_ref, lora_ref, pscale_ref, qg_ref, kg_ref, mu_ref, rwv_ref,
                 y_ref, pool_ref, ko_ref, vo_ref, sh_ref, wkv_ref,
                 pext, kprev, vprev, rext, s_ref, *, tb):
    nqb = tb // BLOCK
    nch = tb // CHUNK

    def proj(off, width):
        return jnp.dot(hb, win_ref[:, off:off + width], preferred_element_type=F32)

    yield 'V'
    x = x_ref[i]
    hb = _bf(_rmsnorm_rows(x, ng_ref[...]))

    yield 'M'
    zp = proj(OFF_POOL, 512)
    yield 'M'
    zq = proj(OFF_QKV, 768)

    yield 'V'
    u = zp[:, 0:256]
    g_pool = zp[:, 256:512]
    pext[pl.ds(24, tb), :] = u
    n_ext = tb + 16
    wins = []
    for sh in (1, 2, 4, 8):
        cur = pext[pl.ds(8, n_ext), :] + pext[pl.ds(8 - sh, n_ext), :]
        pext[pl.ds(8, n_ext), :] = cur
        wins.append(cur[16:16 + tb, :])
    pext[pl.ds(8, 16), :] = u[tb - 16:tb, :]
    lane = lax.broadcasted_iota(jnp.int32, (tb, D_POOL), 1)
    row = lax.broadcasted_iota(jnp.int32, (tb, D_POOL), 0)
    grp = lane // POOL_CG
    win_sum = jnp.where(grp == 0, wins[0], jnp.where(grp == 1, wins[1],
                        jnp.where(grp == 2, wins[2], wins[3])))
    wlen = jnp.where(grp == 0, 2, jnp.where(grp == 1, 4, jnp.where(grp == 2, 8, 16)))
    cnt = jnp.minimum(t * tb + row + 1, wlen).astype(F32)
    pooled = win_sum / cnt - u
    a_out = _mm(pooled, pwbd_ref[...]) * pscale_ref[...] * _silu(g_pool)
    pool_ref[i] = u[tb - POOL_BUF:tb, :]

    yield 'M'
    rin = proj(OFF_RIN, D_SHIFT_PAD)

    yield 'V'
    cos = cos_ref[...]
    sin = sin_ref[...]
    q = _qk_norm_rope(zq[:, 0:512], qg_ref[...], cos, sin, ones_bd) * (ATTN_SCALE * LOG2E)
    qb = _bf(q)

    yield 'M'
    g_attn = proj(OFF_GATTN, 512)
    g_rwkv = proj(OFF_GRWKV, 256)

    yield 'V'
    k = _qk_norm_rope(zq[:, 512:640], kg_ref[...], cos, sin, ones_bd)
    v = zq[:, 640:768]
    kb = _bf(k)
    vb = _bf(v)
    ko_ref[i] = k[tb - WINDOW:tb, :]
    vo_ref[i] = v[tb - WINDOW:tb, :]
    kprev_val = kprev[...]
    vprev_val = vprev[...]
    kprev[...] = kb[tb - BLOCK:tb, :]
    vprev[...] = vb[tb - BLOCK:tb, :]

    nrow = Q_PER_KV * BLOCK
    qi = lax.broadcasted_iota(jnp.int32, (nrow, 2 * BLOCK), 0) % BLOCK
    kj = lax.broadcasted_iota(jnp.int32, (nrow, 2 * BLOCK), 1)
    band = jnp.where(kj > qi, jnp.where(kj <= qi + BLOCK, 0.0, NEG), NEG)
    first_lo = jnp.where(t == 0, BLOCK, 0)
    band_first = jnp.where(kj >= first_lo, band, NEG)
    hrow = lax.broadcasted_iota(jnp.int32, (nrow, 1), 0) // BLOCK

    def attn_unit(j, g):
        rs = slice(j * BLOCK, (j + 1) * BLOCK)
        if j == 0:
            kx = jnp.concatenate([kprev_val, kb[rs]], axis=0)
            vx = jnp.concatenate([vprev_val, vb[rs]], axis=0)
            valid = band_first
        else:
            kx = kb[(j - 1) * BLOCK:(j + 1) * BLOCK]
            vx = vb[(j - 1) * BLOCK:(j + 1) * BLOCK]
            valid = band
        kxg = kx[:, g * HEAD_DIM:(g + 1) * HEAD_DIM]
        vxg = vx[:, g * HEAD_DIM:(g + 1) * HEAD_DIM]
        qs = jnp.concatenate(
            [qb[rs, (g * Q_PER_KV + hq) * HEAD_DIM:(g * Q_PER_KV + hq + 1) * HEAD_DIM]
             for hq in range(Q_PER_KV)], axis=0)
        s = lax.dot_general(qs, kxg, (((1,), (1,)), ((), ())),
                            preferred_element_type=F32)
        s = s + valid
        sink = jnp.where(hrow == 0, sinks_ref[g * Q_PER_KV],
                         jnp.where(hrow == 1, sinks_ref[g * Q_PER_KV + 1],
                                   jnp.where(hrow == 2, sinks_ref[g * Q_PER_KV + 2],
                                             sinks_ref[g * Q_PER_KV + 3]))) * LOG2E
        m = jnp.maximum(jnp.max(s, axis=-1, keepdims=True), sink)
        p = jnp.exp2(s - m)
        denom = jnp.sum(p, axis=-1, keepdims=True) + jnp.exp2(sink - m)
        o = jnp.dot(_bf(p), vxg, preferred_element_type=F32) / denom
        return [o[hq * BLOCK:(hq + 1) * BLOCK, :] for hq in range(Q_PER_KV)]

    attn_todo = [(j, g) for j in range(nqb) for g in range(N_KV)]
    attn_done = {}

    yield 'V'
    rext[pl.ds(8, tb), :] = rin
    prev = rext[pl.ds(7, tb), :]
    rext[pl.ds(7, 1), :] = rin[tb - 1:tb, :]
    xs = rin + (prev - rin) * mu_ref[...]
    sh_ref[i] = rin[tb - 1:tb, 0:D_SHIFT]
    rwv = rwv_ref[...]
    r, k2, vv, kk, a, logw = _rwkv_token_params(xs, rwv, lora_ref[...], ones_bd)

    yield 'M'
    ti = lax.broadcasted_iota(jnp.int32, (tb, tb), 0)
    tj = lax.broadcasted_iota(jnp.int32, (tb, tb), 1)
    tril_blk = jnp.where(ti // CHUNK == tj // CHUNK, jnp.where(tj <= ti, 1.0, 0.0), 0.0).astype(BF16)
    cum = None
    for piece in _split3(logw):
        c = jnp.dot(tril_blk, piece, preferred_element_type=F32)
        cum = c if cum is None else cum + c

    yield 'V'
    cum3 = cum.reshape(nch, CHUNK, D_RWKV)
    cum_last = jnp.broadcast_to(cum3[:, CHUNK - 1:CHUNK, :], cum3.shape).reshape(tb, D_RWKV)
    w_inc = jnp.exp(cum)
    w_exc = jnp.exp(cum - logw)
    w_inv = jnp.exp(-cum)
    rel = jnp.exp(cum_last - cum)
    w_last = jnp.exp(cum_last)
    bvec = kk * a

    def stack(z):
        parts = [z[:, h * RWKV_HEAD:(h + 1) * RWKV_HEAD].reshape(nch, 1, CHUNK, RWKV_HEAD)
                 for h in range(RWKV_HEADS)]
        return jnp.concatenate(parts, axis=1).reshape(nch * RWKV_HEADS, CHUNK, RWKV_HEAD)

    kap = stack(kk * w_exc)
    bt = stack(bvec * w_inv)
    kt = stack(k2 * w_inv)
    rt = stack(r * w_inc)
    bt_l = stack(bvec * rel)
    kt_l = stack(k2 * rel)
    vs = stack(vv)
    wl = stack(w_last)

    yield 'M'
    li = lax.broadcasted_iota(jnp.int32, (1, CHUNK, 2 * CHUNK), 1)
    lj = lax.broadcasted_iota(jnp.int32, (1, CHUNK, 2 * CHUNK), 2)
    lj = jnp.where(lj >= CHUNK, lj - CHUNK, lj)
    btkt = jnp.concatenate([bt, kt], axis=1)
    a_bk = jnp.where(lj < li, _bnt(kap, btkt), 0.0)
    p_bk = jnp.where(lj <= li, _bnt(rt, btkt), 0.0)
    a_b = a_bk[:, :, 0:CHUNK]
    a_k = a_bk[:, :, CHUNK:2 * CHUNK]
    xcat = jnp.concatenate([kap, -_bnn(a_k, vs)], axis=2)
    pw = -a_b
    span = 1
    while 2 * span < CHUNK:
        yield 'M'
        pwb = _bf(pw)
        both = _bnn(pwb, jnp.concatenate([_bf(xcat), pwb], axis=2))
        xcat = xcat + both[:, :, 0:2 * RWKV_HEAD]
        pw = both[:, :, 2 * RWKV_HEAD:2 * RWKV_HEAD + CHUNK]
        span *= 2
        if attn_todo:
            yield 'X'
            jg = attn_todo.pop(0)
            attn_done[jg] = attn_unit(*jg)
    yield 'M'
    xcat = xcat + _bnn(pw, xcat)
    while attn_todo:
        yield 'X'
        jg = attn_todo.pop(0)
        attn_done[jg] = attn_unit(*jg)
    yield 'M'
    xswap = pltpu.roll(xcat, RWKV_HEAD, 2)
    rhs = jnp.concatenate([xswap, jnp.concatenate([vs, jnp.zeros_like(vs)], axis=2)], axis=1)
    o0_pk = _bnn(p_bk, rhs)
    o0 = o0_pk[:, :, 0:RWKV_HEAD]
    rp = rt - o0_pk[:, :, RWKV_HEAD:2 * RWKV_HEAD]
    tn = _btn(bt_l, xcat)
    eye = jnp.where(lax.broadcasted_iota(jnp.int32, (1, CHUNK, CHUNK), 1)
                    == lax.broadcasted_iota(jnp.int32, (1, CHUNK, CHUNK), 2), 1.0, 0.0)
    m_t = eye * wl - tn[:, :, 0:RWKV_HEAD]
    c_t = tn[:, :, RWKV_HEAD:2 * RWKV_HEAD] + _btn(kt_l, vs)
    rp_mt = jnp.concatenate([rp, m_t], axis=1)

    yield 'V'
    b_blocks = [jnp.concatenate(attn_done[(j, 0)] + attn_done[(j, 1)], axis=-1) for j in range(nqb)]
    b_att = b_blocks[0] if nqb == 1 else jnp.concatenate(b_blocks, axis=0)
    b_out = b_att * _silu(g_attn)

    st_cur = s_ref[...]
    o_chunks = []
    for c in range(nch):
        yield 'M'
        sl = slice(c * RWKV_HEADS, (c + 1) * RWKV_HEADS)
        both = _bnn(rp_mt[sl], st_cur)
        o_chunks.append(both[:, 0:CHUNK, :] + o0[sl])
        st_cur = both[:, CHUNK:2 * CHUNK, :] + c_t[sl]
    s_ref[...] = st_cur
    for h in range(RWKV_HEADS):
        wkv_ref[i, pl.ds(h * RWKV_HEAD, RWKV_HEAD), :] = st_cur[h].T

    yield 'V'
    o_heads = []
    for h in range(RWKV_HEADS):
        o_heads.append(jnp.concatenate([oc[h] for oc in o_chunks], axis=0))
    o_wkv = jnp.concatenate(o_heads, axis=-1)
    c_out = _rwkv_finish(o_wkv, r, k2, vv, rwv, ones_bd) * _silu(g_rwkv)

    yield 'M'
    mix =jnp.concatenate([_bf(a_out), _bf(b_out), _bf(c_out)], axis=-1)
    y_ref[i] = x + jnp.dot(mix, wout_ref[...], preferred_element_type=F32)


def _const_spec(shape):
    nd = len(shape)
    return pl.BlockSpec(shape, lambda *_: (0,) * nd)


def _prompt_layer(x, cos, sin, p, tb, rows):
    bsz, seq, _ = x.shape
    nt = seq // tb
    grid = (bsz // rows, nt)
    in_specs = [
        pl.BlockSpec(memory_space=pltpu.SMEM),
        pl.BlockSpec((rows, tb, D_MODEL), lambda b, t: (b, t, 0)),
        pl.BlockSpec((tb, 128), lambda b, t: (t, 0)),
        pl.BlockSpec((tb, 128), lambda b, t: (t, 0)),
        _const_spec((1, D_MODEL)),
        _const_spec((D_MODEL, D_IN_PAD)),
        _const_spec((D_MODEL, D_MODEL)),
        _const_spec((D_POOL, D_POOL)),
        _const_spec((128, 512)),
        _const_spec((1, D_POOL)),
        _const_spec((1, D_ATTN)),
        _const_spec((1, D_KV)),
        _const_spec((1, D_SHIFT_PAD)),
        _const_spec((8, D_RWKV)),
    ]
    out_shape = (
        jax.ShapeDtypeStruct((bsz, seq, D_MODEL), F32),
        jax.ShapeDtypeStruct((bsz, POOL_BUF, D_POOL), F32),
        jax.ShapeDtypeStruct((bsz, WINDOW, D_KV), F32),
        jax.ShapeDtypeStruct((bsz, WINDOW, D_KV), F32),
        jax.ShapeDtypeStruct((bsz, 1, D_SHIFT), F32),
        jax.ShapeDtypeStruct((bsz, RWKV_HEADS * RWKV_HEAD, RWKV_HEAD), F32),
    )
    out_specs = (
        pl.BlockSpec((rows, tb, D_MODEL), lambda b, t: (b, t, 0)),
        pl.BlockSpec((rows, POOL_BUF, D_POOL), lambda b, t: (b, 0, 0)),
        pl.BlockSpec((rows, WINDOW, D_KV), lambda b, t: (b, 0, 0)),
        pl.BlockSpec((rows, WINDOW, D_KV), lambda b, t: (b, 0, 0)),
        pl.BlockSpec((rows, 1, D_SHIFT), lambda b, t: (b, 0, 0)),
        pl.BlockSpec((rows, RWKV_HEADS * RWKV_HEAD, RWKV_HEAD), lambda b, t: (b, 0, 0)),
    )
    scratch = [
        pltpu.VMEM((rows, tb + 24, D_POOL), F32),
        pltpu.VMEM((rows, BLOCK, D_KV), BF16),
        pltpu.VMEM((rows, BLOCK, D_KV), BF16),
        pltpu.VMEM((rows, tb + 8, D_SHIFT_PAD), F32),
        pltpu.VMEM((rows, RWKV_HEADS, RWKV_HEAD, RWKV_HEAD), F32),
    ]
    return pl.pallas_call(
        functools.partial(_prompt_kernel, tb=tb, rows=rows),
        grid=grid,
        in_specs=in_specs,
        out_specs=out_specs,
        out_shape=out_shape,
        scratch_shapes=scratch,
        compiler_params=pltpu.CompilerParams(
            dimension_semantics=("parallel", "arbitrary"),
            vmem_limit_bytes=VMEM_LIMIT_BYTES),
        name="prompt_layer",
    )(p['sinks'], x, cos, sin, p['norm_g'], p['w_in'], p['w_out'], p['pool_wbd'], p['lora_w'],
      p['pool_scale'], p['q_g'], p['k_g'], p['mu'], p['rwv'])


def _heads_rows(z):
    return jnp.stack([z[:, h * RWKV_HEAD:(h + 1) * RWKV_HEAD] for h in range(RWKV_HEADS)], axis=1)


def _sample_kernel(sinks_ref, x_ref, pool_ref, kc_ref, vc_ref, shp_ref, s_ref, cos_ref, sin_ref,
                   ng_ref, win_ref, wout_ref, pwbd_ref, lora_ref, pscale_ref, qg_ref, kg_ref,
                   mu_ref, rwv_ref,
                   y_ref, poolo_ref, ko_ref, vo_ref, sho_ref, so_ref, *, bt):
    ones_bd = _block_ones(256, HEAD_DIM)
    x = x_ref[...]
    hb = _bf(_rmsnorm_rows(x, ng_ref[...]))

    zp = jnp.dot(hb, win_ref[:, OFF_POOL:OFF_POOL + 512], preferred_element_type=F32)
    u = zp[:, 0:256]
    g_pool = zp[:, 256:512]
    buf = pool_ref[...]
    ri = lax.broadcasted_iota(jnp.int32, (1, POOL_BUF, D_POOL), 1)
    grp3 = lax.broadcasted_iota(jnp.int32, (1, POOL_BUF, D_POOL), 2) // POOL_CG
    wlen3 = jnp.where(grp3 == 0, 2, jnp.where(grp3 == 1, 4, jnp.where(grp3 == 2, 8, 16)))
    tail = jnp.sum(jnp.where(ri >= POOL_BUF + 1 - wlen3, buf, 0.0), axis=1)
    grp = lax.broadcasted_iota(jnp.int32, (bt, D_POOL), 1) // POOL_CG
    wlen = jnp.where(grp == 0, 2, jnp.where(grp == 1, 4, jnp.where(grp == 2, 8, 16)))
    cnt = jnp.minimum(PAST_LEN + 1, wlen).astype(F32)
    pooled = (tail + u) / cnt - u
    a_out = _mm(pooled, pwbd_ref[...]) * pscale_ref[...] * _silu(g_pool)
    poolo_ref[:, pl.ds(0, POOL_BUF - 1), :] = pool_ref[:, pl.ds(1, POOL_BUF - 1), :]
    poolo_ref[:, pl.ds(POOL_BUF - 1, 1), :] = u[:, None, :]

    zq = jnp.dot(hb, win_ref[:, OFF_QKV:OFF_QKV + 768], preferred_element_type=F32)
    cos = cos_ref[...]
    sin = sin_ref[...]
    q = _qk_norm_rope(zq[:, 0:512], qg_ref[...], cos, sin, ones_bd) * ATTN_SCALE
    k_new = _qk_norm_rope(zq[:, 512:640], kg_ref[...], cos, sin, ones_bd)
    v_new = zq[:, 640:768]
    si = lax.broadcasted_iota(jnp.int32, (1, WINDOW, D_KV), 1)
    k_win = jnp.where(si == WINDOW - 1, k_new[:, None, :], pltpu.roll(kc_ref[...], WINDOW - 1, 1))
    v_win = jnp.where(si == WINDOW - 1, v_new[:, None, :], pltpu.roll(vc_ref[...], WINDOW - 1, 1))
    ko_ref[...] = k_win
    vo_ref[...] = v_win
    lane_g = lax.broadcasted_iota(jnp.int32, (bt, D_KV), 1) // HEAD_DIM
    q_rows = []
    for h in range(N_HEADS):
        qh = q[:, h * HEAD_DIM:(h + 1) * HEAD_DIM]
        q_rows.append(jnp.where(lane_g == h // Q_PER_KV, jnp.concatenate([qh, qh], axis=-1), 0.0))
    qe = jnp.stack(q_rows, axis=1)
    s = jnp.einsum('bhl,bsl->bhs', _bf(qe), _bf(k_win), preferred_element_type=F32)
    hi = lax.broadcasted_iota(jnp.int32, (1, N_HEADS, 1), 1)
    sink = jnp.zeros((1, N_HEADS, 1), F32)
    for h in range(N_HEADS):
        sink = jnp.where(hi == h, sinks_ref[h], sink)
    m = jnp.maximum(jnp.max(s, axis=-1, keepdims=True), sink)
    p = jnp.exp(s - m)
    denom = jnp.sum(p, axis=-1, keepdims=True) + jnp.exp(sink - m)
    o = jnp.einsum('bhs,bsl->bhl', _bf(p), _bf(v_win), preferred_element_type=F32) / denom
    b_att = jnp.concatenate(
        [o[:, h, (h // Q_PER_KV) * HEAD_DIM:(h // Q_PER_KV + 1) * HEAD_DIM] for h in range(N_HEADS)],
        axis=-1)
    g_attn = jnp.dot(hb, win_ref[:, OFF_GATTN:OFF_GATTN + 512], preferred_element_type=F32)
    b_out = b_att * _silu(g_attn)

    rin = jnp.dot(hb, win_ref[:, OFF_RIN:OFF_RIN + D_SHIFT_PAD], preferred_element_type=F32)
    xs = rin + (shp_ref[...] - rin) * mu_ref[...]
    sho_ref[...] = rin[:, 0:D_SHIFT]
    rwv = rwv_ref[...]
    r, k2, vv, kk, a, logw = _rwkv_token_params(xs, rwv, lora_ref[...], ones_bd)
    st = s_ref[...]
    kk4 = _heads_rows(kk)[:, :, None, :]
    w4 = _heads_rows(jnp.exp(logw))[:, :, None, :]
    b4 = _heads_rows(kk * a)[:, :, None, :]
    k4 = _heads_rows(k2)[:, :, None, :]
    r4 = _heads_rows(r)[:, :, None, :]
    v_col = _heads_rows(vv)[:, :, :, None]
    sa = jnp.sum(st * kk4, axis=-1, keepdims=True)
    st = st * w4 - sa * b4 + v_col * k4
    so_ref[...] = st
    o4 = jnp.sum(st * r4, axis=-1)
    o_wkv = jnp.concatenate([o4[:, h, :] for h in range(RWKV_HEADS)], axis=-1)
    g_rwkv = jnp.dot(hb, win_ref[:, OFF_GRWKV:OFF_GRWKV + 256], preferred_element_type=F32)
    c_out = _rwkv_finish(o_wkv, r, k2, vv, rwv, ones_bd) * _silu(g_rwkv)

    mix = jnp.concatenate([_bf(a_out), _bf(b_out), _bf(c_out)], axis=-1)
    y_ref[...] = x + jnp.dot(mix, wout_ref[...], preferred_element_type=F32)


def _sample_layer(x, pool, kc, vc, shift, wkv, cos, sin, p, bt):
    nb = x.shape[0]
    assert PAST_LEN >= WINDOW and nb % bt == 0
    row2 = lambda i: (i, 0)
    row3 = lambda i: (i, 0, 0)
    row4 = lambda i: (i, 0, 0, 0)
    in_specs = [
        pl.BlockSpec(memory_space=pltpu.SMEM),
        pl.BlockSpec((bt, D_MODEL), row2),
        pl.BlockSpec((bt, POOL_BUF, D_POOL), row3),
        pl.BlockSpec((bt, WINDOW, D_KV), row3),
        pl.BlockSpec((bt, WINDOW, D_KV), row3),
        pl.BlockSpec((bt, D_SHIFT_PAD), row2),
        pl.BlockSpec((bt, RWKV_HEADS, RWKV_HEAD, RWKV_HEAD), row4),
        _const_spec((1, 128)),
        _const_spec((1, 128)),
        _const_spec((1, D_MODEL)),
        _const_spec((D_MODEL, D_IN_PAD)),
        _const_spec((D_MODEL, D_MODEL)),
        _const_spec((D_POOL, D_POOL)),
        _const_spec((128, 512)),
        _const_spec((1, D_POOL)),
        _const_spec((1, D_ATTN)),
        _const_spec((1, D_KV)),
        _const_spec((1, D_SHIFT_PAD)),
        _const_spec((8, D_RWKV)),
    ]
    out_shape = (
        jax.ShapeDtypeStruct((nb, D_MODEL), F32),
        jax.ShapeDtypeStruct((nb, POOL_BUF, D_POOL), F32),
        jax.ShapeDtypeStruct((nb, WINDOW, D_KV), F32),
        jax.ShapeDtypeStruct((nb, WINDOW, D_KV), F32),
        jax.ShapeDtypeStruct((nb, D_SHIFT), F32),
        jax.ShapeDtypeStruct((nb, RWKV_HEADS, RWKV_HEAD, RWKV_HEAD), F32),
    )
    out_specs = (
        pl.BlockSpec((bt, D_MODEL), row2),
        pl.BlockSpec((bt, POOL_BUF, D_POOL), row3),
        pl.BlockSpec((bt, WINDOW, D_KV), row3),
        pl.BlockSpec((bt, WINDOW, D_KV), row3),
        pl.BlockSpec((bt, D_SHIFT), row2),
        pl.BlockSpec((bt, RWKV_HEADS, RWKV_HEAD, RWKV_HEAD), row4),
    )
    return pl.pallas_call(
        functools.partial(_sample_kernel, bt=bt),
        grid=(nb // bt,),
        in_specs=in_specs,
        out_specs=out_specs,
        out_shape=out_shape,
        compiler_params=pltpu.CompilerParams(
            dimension_semantics=("parallel",),
            vmem_limit_bytes=VMEM_LIMIT_BYTES),
        name="sample_layer",
    )(p['sinks'], x, pool, kc, vc, shift, wkv, cos, sin, p['norm_g'], p['w_in'], p['w_out'],
      p['pool_wbd'], p['lora_w'], p['pool_scale'], p['q_g'], p['k_g'], p['mu'], p['rwv'])


def _rope_tables(pos):
    half = HEAD_DIM // 2
    freqs = ROPE_THETA ** (-jnp.arange(half, dtype=F32) / half)
    ang = pos.astype(F32)[:, None] * freqs[None, :]
    c = jnp.cos(ang)
    s = jnp.sin(ang)
    cos = jnp.concatenate([c, c, c, c], axis=-1)
    sin = jnp.concatenate([-s, s, -s, s], axis=-1)
    return cos, sin


def _layer_params(l, norm_g, w_in, w_out, pool_w, pool_scale, q_norm_g, k_norm_g, attn_sinks,
                  rwkv_mu, rwkv_w0, rwkv_w_up, rwkv_a0, rwkv_a_up, rwkv_k_k, rwkv_k_a, rwkv_r_k,
                  rwkv_ln_g, rwkv_ln_b):
    d_rin_end = OFF_RIN + D_SHIFT
    wi = w_in[l]
    w_in_p = jnp.concatenate(
        [wi[:, :d_rin_end], jnp.zeros((D_MODEL, D_SHIFT_PAD - D_SHIFT), wi.dtype), wi[:, d_rin_end:]],
        axis=1).astype(BF16)
    pw = pool_w[l]
    bd = jnp.zeros((D_POOL, D_POOL), F32)
    for g in range(len(POOL_WINDOWS)):
        bd = bd.at[g * POOL_CG:(g + 1) * POOL_CG, g * POOL_CG:(g + 1) * POOL_CG].set(pw[g])
    lora = jnp.zeros((128, 512), F32)
    lora = lora.at[0:LORA, 0:256].set(rwkv_w_up[l])
    lora = lora.at[LORA:2 * LORA, 256:512].set(rwkv_a_up[l])
    rwv = jnp.stack([rwkv_w0[l], rwkv_a0[l], rwkv_k_k[l], rwkv_k_a[l], rwkv_r_k[l],
                     rwkv_ln_g[l], rwkv_ln_b[l], jnp.zeros((D_RWKV,), F32)], axis=0)
    return {
        'sinks': attn_sinks[l].astype(F32),
        'norm_g': norm_g[l][None, :],
        'w_in': w_in_p,
        'w_out': w_out[l].astype(BF16),
        'pool_wbd': bd.astype(BF16),
        'lora_w': lora.astype(BF16),
        'pool_scale': pool_scale[l][None, :],
        'q_g': jnp.tile(q_norm_g[l], N_HEADS)[None, :],
        'k_g': jnp.tile(k_norm_g[l], N_KV)[None, :],
        'mu': jnp.pad(rwkv_mu[l], (0, D_SHIFT_PAD - D_SHIFT))[None, :],
        'rwv': rwv,
    }


def kernel(x_prompt, x_sample, state_pool, cache_swa_k, cache_swa_v, state_rwkv_shift, state_rwkv_wkv, norm_g, w_in, w_out, pool_w, pool_scale, q_norm_g, k_norm_g, attn_sinks, rwkv_mu, rwkv_w0, rwkv_w_up, rwkv_a0, rwkv_a_up, rwkv_k_k, rwkv_k_a, rwkv_r_k, rwkv_ln_g, rwkv_ln_b):
    depth = w_in.shape[0]
    bsz, seq, _ = x_prompt.shape
    tb = 256 if seq % 256 == 0 else BLOCK
    rows = 2 if bsz % 2 == 0 else 1
    nb = x_sample.shape[0]
    bt = 16 if nb % 16 == 0 else nb
    cos_p, sin_p = _rope_tables(jnp.arange(seq))
    cos_s, sin_s = _rope_tables(jnp.full((1,), PAST_LEN))
    yp = x_prompt
    ys = x_sample.reshape(nb, D_MODEL)
    outs_p = [[] for _ in range(5)]
    outs_s = [[] for _ in range(5)]
    for l in range(depth):
        p = _layer_params(l, norm_g, w_in, w_out, pool_w, pool_scale, q_norm_g, k_norm_g,
                          attn_sinks, rwkv_mu, rwkv_w0, rwkv_w_up, rwkv_a0, rwkv_a_up,
                          rwkv_k_k, rwkv_k_a, rwkv_r_k, rwkv_ln_g, rwkv_ln_b)
        yp, pool_n, k_n, v_n, sh_n, wkv_n = _prompt_layer(yp, cos_p, sin_p, p, tb, rows)
        outs_p[0].append(pool_n)
        outs_p[1].append(k_n.reshape(bsz, WINDOW, N_KV, HEAD_DIM))
        outs_p[2].append(v_n.reshape(bsz, WINDOW, N_KV, HEAD_DIM))
        outs_p[3].append(sh_n.reshape(bsz, D_SHIFT))
        outs_p[4].append(wkv_n.reshape(bsz, RWKV_HEADS, RWKV_HEAD, RWKV_HEAD))
        ys, pool_n, k_n, v_n, sh_n, wkv_n = _sample_layer(
            ys, state_pool[l],
            cache_swa_k[l].reshape(nb, WINDOW, D_KV), cache_swa_v[l].reshape(nb, WINDOW, D_KV),
            jnp.pad(state_rwkv_shift[l], ((0, 0), (0, D_SHIFT_PAD - D_SHIFT))),
            state_rwkv_wkv[l], cos_s, sin_s, p, bt)
        outs_s[0].append(pool_n)
        outs_s[1].append(k_n.reshape(nb, WINDOW, N_KV, HEAD_DIM))
        outs_s[2].append(v_n.reshape(nb, WINDOW, N_KV, HEAD_DIM))
        outs_s[3].append(sh_n)
        outs_s[4].append(wkv_n)
    res = [yp, ys.reshape(nb, 1, D_MODEL)]
    for a, b in zip(outs_p, outs_s):
        res.append(jnp.stack(a))
        res.append(jnp.stack(b))
    return tuple(res)
```

```python
import functools

import numpy as np
import jax
import jax.numpy as jnp
from jax import lax
from jax.experimental import pallas as pl
from jax.experimental.pallas import tpu as pltpu

D_MODEL = 1024
D_POOL = 256
POOL_WINDOWS = (2, 4, 8, 16)
POOL_CG = 64
POOL_BUF = 15
HEAD_DIM = 64
D_ATTN = 512
N_HEADS = 8
N_KV = 2
Q_PER_KV = 4
D_KV = 128
WINDOW = 128
BLOCK = 128
ROPE_THETA = 10000.0
QK_EPS = 1e-6
D_RWKV = 256
RWKV_HEAD = 64
RWKV_HEADS = 4
LORA = 32
D_SHIFT = 832
D_SHIFT_PAD = 896
GN_EPS = 64e-5
NORM_EPS = 1e-6
PAST_LEN = 16384
ATTN_SCALE = HEAD_DIM ** -0.5
LOG2E = 1.4426950408889634

OFF_POOL = 0
OFF_QKV = 512
OFF_GATTN = 1280
OFF_RIN = 1792
OFF_GRWKV = 2688
D_IN_PAD = 2944

CHUNK = 64
NEG = -1e30
F32 = jnp.float32
BF16 = jnp.bfloat16

VMEM_LIMIT_BYTES = 56 * 1024 * 1024


def _bf(x):
    return x.astype(BF16)


def _mm(x, y):
    return jnp.dot(_bf(x), _bf(y), preferred_element_type=F32)


def _bnt(x, y):
    return jnp.einsum('bik,bjk->bij', _bf(x), _bf(y), preferred_element_type=F32)


def _bnn(x, y):
    return jnp.einsum('bik,bkj->bij', _bf(x), _bf(y), preferred_element_type=F32)


def _btn(x, y):
    return jnp.einsum('bli,blj->bij', _bf(x), _bf(y), preferred_element_type=F32)


def _sigmoid(x):
    return 0.5 + 0.5 * jnp.tanh(0.5 * x)


def _silu(x):
    half = 0.5 * x
    return half + half * jnp.tanh(half)


def _softplus(x):
    return jnp.maximum(x, 0.0) + jnp.log(1.0 + jnp.exp(-jnp.abs(x)))


def _split3(x):
    h1 = _bf(x)
    r1 = x - h1.astype(F32)
    h2 = _bf(r1)
    h3 = _bf(r1 - h2.astype(F32))
    return h1, h2, h3


def _block_ones(n, seg):
    r = lax.broadcasted_iota(jnp.int32, (n, n), 0) // seg
    c = lax.broadcasted_iota(jnp.int32, (n, n), 1) // seg
    return jnp.where(r == c, 1.0, 0.0).astype(BF16)


def _seg_sum(x, ones_bd):
    w = x.shape[-1]
    step = min(w, ones_bd.shape[0])
    outs = [jnp.dot(_bf(x[:, c0:c0 + step]), ones_bd[:step, :step], preferred_element_type=F32)
            for c0 in range(0, w, step)]
    return outs[0] if len(outs) == 1 else jnp.concatenate(outs, axis=-1)


def _rmsnorm_rows(x, g):
    ms = jnp.mean(x * x, axis=-1, keepdims=True)
    return x * lax.rsqrt(ms + NORM_EPS) * g


def _swap_halves(x):
    w = x.shape[-1]
    lane = lax.broadcasted_iota(jnp.int32, x.shape, x.ndim - 1)
    fwd = pltpu.roll(x, w - HEAD_DIM // 2, x.ndim - 1)
    bwd = pltpu.roll(x, HEAD_DIM // 2, x.ndim - 1)
    return jnp.where(lane % HEAD_DIM < HEAD_DIM // 2, fwd, bwd)


def _qk_norm_rope(x, g, cos, sin_signed, ones_bd):
    ms = _seg_sum(x * x, ones_bd) * (1.0 / HEAD_DIM)
    xn = x * lax.rsqrt(ms + QK_EPS) * g
    reps = x.shape[-1] // cos.shape[-1]
    if reps > 1:
        cos = jnp.concatenate([cos] * reps, axis=-1)
        sin_signed = jnp.concatenate([sin_signed] * reps, axis=-1)
    return xn * cos + _swap_halves(xn) * sin_signed


def _rwkv_token_params(xs, rwv, lora_w, ones_bd):
    r = xs[:, 0:256]
    k = xs[:, 256:512]
    v = xs[:, 512:768]
    lo = xs[:, 768:896]
    lane = lax.broadcasted_iota(jnp.int32, lo.shape, 1)
    lo_in = jnp.where(lane < LORA, jnp.tanh(lo), lo)
    pre = _mm(lo_in, lora_w)
    w_pre = rwv[0:1, :] + pre[:, 0:256]
    a_pre = rwv[1:2, :] + pre[:, 256:512]
    w_log = -_softplus(-w_pre) - 0.5
    logw = -jnp.exp(w_log)
    a = _sigmoid(a_pre)
    kk = k * rwv[2:3, :]
    nrm = jnp.sqrt(_seg_sum(kk * kk, ones_bd))
    kk = kk / jnp.maximum(nrm, 1e-12)
    k2 = k * (1.0 + (a - 1.0) * rwv[3:4, :])
    return r, k2, v, kk, a, logw


def _rwkv_finish(o, r, k2, v, rwv, ones_bd):
    inv_n = 1.0 / RWKV_HEAD
    mu_o = _seg_sum(o, ones_bd) * inv_n
    d = o - mu_o
    var = _seg_sum(d * d, ones_bd) * inv_n
    on = d * lax.rsqrt(var + GN_EPS) * rwv[5:6, :] + rwv[6:7, :]
    bonus = _seg_sum(r * k2 * rwv[4:5, :], ones_bd) * v
    return on + bonus


def _run_interleaved(programs):
    state = [[prog, next(prog), 0] for prog in programs]
    last = 'M'
    while state:
        want = 'V' if last == 'M' else 'M'
        order = sorted(state, key=lambda s: s[2])
        pick = next((s for s in order if s[1] == want), None)
        if pick is None:
            pick = next((s for s in order if s[1] == 'X'), order[0])
        if pick[1] != 'X':
            last = pick[1]
        pick[2] += 1
        try:
            pick[1] = next(pick[0])
        except StopIteration:
            state.remove(pick)


def _prompt_kernel(sinks_ref, x_ref, cos_ref, sin_ref, ng_ref, win_ref, wout_ref, pwbd_ref,
                   lora_ref, pscale_ref, qg_ref, kg_ref, mu_ref, rwv_ref,
                   y_ref, pool_ref, ko_ref, vo_ref, sh_ref, wkv_ref,
                   pext_all, kprev_all, vprev_all, rext_all, s_all, *, tb, rows, layer):
    t = pl.program_id(1)

    @pl.when(t == 0)
    def _init():
        pext_all[...] = jnp.zeros(pext_all.shape, F32)
        kprev_all[...] = jnp.zeros(kprev_all.shape, BF16)
        vprev_all[...] = jnp.zeros(vprev_all.shape, BF16)
        rext_all[:, pl.ds(0, 8), :] = jnp.zeros((rows, 8, D_SHIFT_PAD), F32)
        s_all[...] = jnp.zeros(s_all.shape, F32)

    ones_bd = _block_ones(256, HEAD_DIM)
    programs = [
        _prompt_tile(i, t, layer, ones_bd, sinks_ref, x_ref, cos_ref, sin_ref, ng_ref, win_ref, wout_ref,
                     pwbd_ref, lora_ref, pscale_ref, qg_ref, kg_ref, mu_ref, rwv_ref,
                     y_ref, pool_ref, ko_ref, vo_ref, sh_ref, wkv_ref,
                     pext_all.at[i], kprev_all.at[i], vprev_all.at[i], rext_all.at[i], s_all.at[i],
                     tb=tb)
        for i in range(rows)]
    _run_interleaved(programs)


def _prompt_tile(i, t, layer, ones_bd, sinks_ref, x_ref, cos_ref, sin_ref, ng_ref, win_ref, wout_ref,
                 pwbd_ref, lora_ref, pscale_ref, qg_ref, kg_ref, mu_ref, rwv_ref,
                 y_ref, pool_ref, ko_ref, vo_ref, sh_ref, wkv_ref,
                 pext, kprev, vprev, rext, s_ref, *, tb):
    nqb = tb // BLOCK
    nch = tb // CHUNK

    def proj(off, width):
        return jnp.dot(hb, win_ref[:, off:off + width], preferred_element_type=F32)

    yield 'V'
    x = x_ref[i]
    hb = _bf(_rmsnorm_rows(x, ng_ref[...]))

    yield 'M'
    zp = proj(OFF_POOL, 512)
    yield 'M'
    zq = proj(OFF_QKV, 768)

    yield 'V'
    u = zp[:, 0:256]
    g_pool = zp[:, 256:512]
    pext[pl.ds(24, tb), :] = u
    n_ext = tb + 16
    wins = []
    for sh in (1, 2, 4, 8):
        cur = pext[pl.ds(8, n_ext), :] + pext[pl.ds(8 - sh, n_ext), :]
        pext[pl.ds(8, n_ext), :] = cur
        wins.append(cur[16:16 + tb, :])
    pext[pl.ds(8, 16), :] = u[tb - 16:tb, :]
    lane = lax.broadcasted_iota(jnp.int32, (tb, D_POOL), 1)
    row = lax.broadcasted_iota(jnp.int32, (tb, D_POOL), 0)
    grp = lane // POOL_CG
    win_sum = jnp.where(grp == 0, wins[0], jnp.where(grp == 1, wins[1],
                        jnp.where(grp == 2, wins[2], wins[3])))
    wlen = jnp.where(grp == 0, 2, jnp.where(grp == 1, 4, jnp.where(grp == 2, 8, 16)))
    cnt = jnp.minimum(t * tb + row + 1, wlen).astype(F32)
    pooled = win_sum / cnt - u
    a_out = _mm(pooled, pwbd_ref[...]) * pscale_ref[...] * _silu(g_pool)
    pool_ref[i] = u[tb - POOL_BUF:tb, :]

    yield 'M'
    rin = proj(OFF_RIN, D_SHIFT_PAD)

    yield 'V'
    cos = cos_ref[...]
    sin = sin_ref[...]
    q = _qk_norm_rope(zq[:, 0:512], qg_ref[...], cos, sin, ones_bd) * (ATTN_SCALE * LOG2E)
    qb = _bf(q)

    yield 'M'
    g_attn = proj(OFF_GATTN, 512)
    g_rwkv = proj(OFF_GRWKV, 256)

    yield 'V'
    k = _qk_norm_rope(zq[:, 512:640], kg_ref[...], cos, sin, ones_bd)
    v = zq[:, 640:768]
    kb = _bf(k)
    vb = _bf(v)
    ko_ref[i] = k[tb - WINDOW:tb, :]
    vo_ref[i] = v[tb - WINDOW:tb, :]
    kprev_val = kprev[...]
    vprev_val = vprev[...]
    kprev[...] = kb[tb - BLOCK:tb, :]
    vprev[...] = vb[tb - BLOCK:tb, :]

    nrow = Q_PER_KV * BLOCK
    qi = lax.broadcasted_iota(jnp.int32, (nrow, 2 * BLOCK), 0) % BLOCK
    kj = lax.broadcasted_iota(jnp.int32, (nrow, 2 * BLOCK), 1)
    band = jnp.where(kj > qi, jnp.where(kj <= qi + BLOCK, 0.0, NEG), NEG)
    first_lo = jnp.where(t == 0, BLOCK, 0)
    band_first = jnp.where(kj >= first_lo, band, NEG)
    hrow = lax.broadcasted_iota(jnp.int32, (nrow, 1), 0) // BLOCK

    def attn_unit(j, g):
        rs = slice(j * BLOCK, (j + 1) * BLOCK)
        if j == 0:
            kx = jnp.concatenate([kprev_val, kb[rs]], axis=0)
            vx = jnp.concatenate([vprev_val, vb[rs]], axis=0)
            valid = band_first
        else:
            kx = kb[(j - 1) * BLOCK:(j + 1) * BLOCK]
            vx = vb[(j - 1) * BLOCK:(j + 1) * BLOCK]
            valid = band
        kxg = kx[:, g * HEAD_DIM:(g + 1) * HEAD_DIM]
        vxg = vx[:, g * HEAD_DIM:(g + 1) * HEAD_DIM]
        qs = jnp.concatenate(
            [qb[rs, (g * Q_PER_KV + hq) * HEAD_DIM:(g * Q_PER_KV + hq + 1) * HEAD_DIM]
             for hq in range(Q_PER_KV)], axis=0)
        s = lax.dot_general(qs, kxg, (((1,), (1,)), ((), ())),
                            preferred_element_type=F32)
        s = s + valid
        sink = jnp.where(hrow == 0, sinks_ref[layer, g * Q_PER_KV],
                         jnp.where(hrow == 1, sinks_ref[layer, g * Q_PER_KV + 1],
                                   jnp.where(hrow == 2, sinks_ref[layer, g * Q_PER_KV + 2],
                                             sinks_ref[layer, g * Q_PER_KV + 3]))) * LOG2E
        m = jnp.maximum(jnp.max(s, axis=-1, keepdims=True), sink)
        p = jnp.exp2(s - m)
        denom = jnp.sum(p, axis=-1, keepdims=True) + jnp.exp2(sink - m)
        o = jnp.dot(_bf(p), vxg, preferred_element_type=F32) / denom
        return [o[hq * BLOCK:(hq + 1) * BLOCK, :] for hq in range(Q_PER_KV)]

    attn_todo = [(j, g) for j in range(nqb) for g in range(N_KV)]
    attn_done = {}

    yield 'V'
    rext[pl.ds(8, tb), :] = rin
    prev = rext[pl.ds(7, tb), :]
    rext[pl.ds(7, 1), :] = rin[tb - 1:tb, :]
    xs = rin + (prev - rin) * mu_ref[...]
    sh_ref[i] = rin[tb - 1:tb, 0:D_SHIFT]
    rwv = rwv_ref[...]
    r, k2, vv, kk, a, logw = _rwkv_token_params(xs, rwv, lora_ref[...], ones_bd)

    yield 'M'
    ti = lax.broadcasted_iota(jnp.int32, (tb, tb), 0)
    tj = lax.broadcasted_iota(jnp.int32, (tb, tb), 1)
    tril_blk = jnp.where(ti // CHUNK == tj // CHUNK, jnp.where(tj <= ti, 1.0, 0.0), 0.0).astype(BF16)
    cum = None
    for piece in _split3(logw):
        c = jnp.dot(tril_blk, piece, preferred_element_type=F32)
        cum = c if cum is None else cum + c

    yield 'V'
    cum3 = cum.reshape(nch, CHUNK, D_RWKV)
    cum_last = jnp.broadcast_to(cum3[:, CHUNK - 1:CHUNK, :], cum3.shape).reshape(tb, D_RWKV)
    w_inc = jnp.exp(cum)
    w_exc = jnp.exp(cum - logw)
    w_inv = jnp.exp(-cum)
    rel = jnp.exp(cum_last - cum)
    w_last = jnp.exp(cum_last)
    bvec = kk * a

    def stack(z):
        parts = [z[:, h * RWKV_HEAD:(h + 1) * RWKV_HEAD].reshape(nch, 1, CHUNK, RWKV_HEAD)
                 for h in range(RWKV_HEADS)]
        return jnp.concatenate(parts, axis=1).reshape(nch * RWKV_HEADS, CHUNK, RWKV_HEAD)

    kap = stack(kk * w_exc)
    bt = stack(bvec * w_inv)
    kt = stack(k2 * w_inv)
    rt = stack(r * w_inc)
    bt_l = stack(bvec * rel)
    kt_l = stack(k2 * rel)
    vs = stack(vv)
    wl = stack(w_last)

    yield 'M'
    li = lax.broadcasted_iota(jnp.int32, (1, CHUNK, 2 * CHUNK), 1)
    lj = lax.broadcasted_iota(jnp.int32, (1, CHUNK, 2 * CHUNK), 2)
    lj = jnp.where(lj >= CHUNK, lj - CHUNK, lj)
    btkt = jnp.concatenate([bt, kt], axis=1)
    a_bk = jnp.where(lj < li, _bnt(kap, btkt), 0.0)
    p_bk = jnp.where(lj <= li, _bnt(rt, btkt), 0.0)
    a_b = a_bk[:, :, 0:CHUNK]
    a_k = a_bk[:, :, CHUNK:2 * CHUNK]
    xcat = jnp.concatenate([kap, -_bnn(a_k, vs)], axis=2)
    pw = -a_b
    span = 1
    while 2 * span < CHUNK:
        yield 'M'
        pwb = _bf(pw)
        both = _bnn(pwb, jnp.concatenate([_bf(xcat), pwb], axis=2))
        xcat = xcat + both[:, :, 0:2 * RWKV_HEAD]
        pw = both[:, :, 2 * RWKV_HEAD:2 * RWKV_HEAD + CHUNK]
        span *= 2
        if attn_todo:
            yield 'X'
            jg = attn_todo.pop(0)
            attn_done[jg] = attn_unit(*jg)
    yield 'M'
    xcat = xcat + _bnn(pw, xcat)
    while attn_todo:
        yield 'X'
        jg = attn_todo.pop(0)
        attn_done[jg] = attn_unit(*jg)
    yield 'M'
    xswap = pltpu.roll(xcat, RWKV_HEAD, 2)
    rhs = jnp.concatenate([xswap, jnp.concatenate([vs, jnp.zeros_like(vs)], axis=2)], axis=1)
    o0_pk = _bnn(p_bk, rhs)
    o0 = o0_pk[:, :, 0:RWKV_HEAD]
    rp = rt - o0_pk[:, :, RWKV_HEAD:2 * RWKV_HEAD]
    tn = _btn(bt_l, xcat)
    eye = jnp.where(lax.broadcasted_iota(jnp.int32, (1, CHUNK, CHUNK), 1)
                    == lax.broadcasted_iota(jnp.int32, (1, CHUNK, CHUNK), 2), 1.0, 0.0)
    m_t = eye * wl - tn[:, :, 0:RWKV_HEAD]
    c_t = tn[:, :, RWKV_HEAD:2 * RWKV_HEAD] + _btn(kt_l, vs)
    rp_mt = jnp.concatenate([rp, m_t], axis=1)

    yield 'V'
    b_blocks = [jnp.concatenate(attn_done[(j, 0)] + attn_done[(j, 1)], axis=-1) for j in range(nqb)]
    b_att = b_blocks[0] if nqb == 1 else jnp.concatenate(b_blocks, axis=0)
    b_out = b_att * _silu(g_attn)

    st_cur = s_ref[...]
    o_chunks = []
    for c in range(nch):
        yield 'M'
        sl = slice(c * RWKV_HEADS, (c + 1) * RWKV_HEADS)
        both = _bnn(rp_mt[sl], st_cur)
        o_chunks.append(both[:, 0:CHUNK, :] + o0[sl])
        st_cur = both[:, CHUNK:2 * CHUNK, :] + c_t[sl]
    s_ref[...] = st_cur
    for h in range(RWKV_HEADS):
        wkv_ref[i, pl.ds(h * RWKV_HEAD, RWKV_HEAD), :] = st_cur[h].T

    yield 'V'
    o_heads = []
    for h in range(RWKV_HEADS):
        o_heads.append(jnp.concatenate([oc[h] for oc in o_chunks], axis=0))
    o_wkv = jnp.concatenate(o_heads, axis=-1)
    c_out = _rwkv_finish(o_wkv, r, k2, vv, rwv, ones_bd) * _silu(g_rwkv)

    yield 'M'
    mix = jnp.concatenate([_bf(a_out), _bf(b_out), _bf(c_out)], axis=-1)
    y_ref[i] = x + jnp.dot(mix, wout_ref[...], preferred_element_type=F32)


def _layer_spec(shape, layer):
    nd = len(shape)
    return pl.BlockSpec((None,) + tuple(shape), lambda *_: (layer,) + (0,) * nd)


def _prompt_layer(x, cos, sin, p, layer, tb, rows):
    bsz, seq, _ = x.shape
    nt = seq // tb
    grid = (bsz // rows, nt)
    in_specs = [
        pl.BlockSpec(memory_space=pltpu.SMEM),
        pl.BlockSpec((rows, tb, D_MODEL), lambda b, t: (b, t, 0)),
        pl.BlockSpec((tb, 128), lambda b, t: (t, 0)),
        pl.BlockSpec((tb, 128), lambda b, t: (t, 0)),
        _layer_spec((1, D_MODEL), layer),
        _layer_spec((D_MODEL, D_IN_PAD), layer),
        _layer_spec((D_MODEL, D_MODEL), layer),
        _layer_spec((D_POOL, D_POOL), layer),
        _layer_spec((128, 512), layer),
        _layer_spec((1, D_POOL), layer),
        _layer_spec((1, D_ATTN), layer),
        _layer_spec((1, D_KV), layer),
        _layer_spec((1, D_SHIFT_PAD), layer),
        _layer_spec((8, D_RWKV), layer),
    ]
    out_shape = (
        jax.ShapeDtypeStruct((bsz, seq, D_MODEL), F32),
        jax.ShapeDtypeStruct((bsz, POOL_BUF, D_POOL), F32),
        jax.ShapeDtypeStruct((bsz, WINDOW, D_KV), F32),
        jax.ShapeDtypeStruct((bsz, WINDOW, D_KV), F32),
        jax.ShapeDtypeStruct((bsz, 1, D_SHIFT), F32),
        jax.ShapeDtypeStruct((bsz, RWKV_HEADS * RWKV_HEAD, RWKV_HEAD), F32),
    )
    out_specs = (
        pl.BlockSpec((rows, tb, D_MODEL), lambda b, t: (b, t, 0)),
        pl.BlockSpec((rows, POOL_BUF, D_POOL), lambda b, t: (b, 0, 0)),
        pl.BlockSpec((rows, WINDOW, D_KV), lambda b, t: (b, 0, 0)),
        pl.BlockSpec((rows, WINDOW, D_KV), lambda b, t: (b, 0, 0)),
        pl.BlockSpec((rows, 1, D_SHIFT), lambda b, t: (b, 0, 0)),
        pl.BlockSpec((rows, RWKV_HEADS * RWKV_HEAD, RWKV_HEAD), lambda b, t: (b, 0, 0)),
    )
    scratch = [
        pltpu.VMEM((rows, tb + 24, D_POOL), F32),
        pltpu.VMEM((rows, BLOCK, D_KV), BF16),
        pltpu.VMEM((rows, BLOCK, D_KV), BF16),
        pltpu.VMEM((rows, tb + 8, D_SHIFT_PAD), F32),
        pltpu.VMEM((rows, RWKV_HEADS, RWKV_HEAD, RWKV_HEAD), F32),
    ]
    return pl.pallas_call(
        functools.partial(_prompt_kernel, tb=tb, rows=rows, layer=layer),
        grid=grid,
        in_specs=in_specs,
        out_specs=out_specs,
        out_shape=out_shape,
        scratch_shapes=scratch,
        compiler_params=pltpu.CompilerParams(
            dimension_semantics=("parallel", "arbitrary"),
            vmem_limit_bytes=VMEM_LIMIT_BYTES),
        name="prompt_layer",
    )(p['sinks'], x, cos, sin, p['norm_g'], p['w_in'], p['w_out'], p['pool_wbd'], p['lora_w'],
      p['pool_scale'], p['q_g'], p['k_g'], p['mu'], p['rwv'])


def _heads_rows(z):
    return jnp.stack([z[:, h * RWKV_HEAD:(h + 1) * RWKV_HEAD] for h in range(RWKV_HEADS)], axis=1)


def _sample_kernel(sinks_ref, x_ref, pool_ref, kc_ref, vc_ref, shp_ref, s_ref, cos_ref, sin_ref,
                   ng_ref, win_ref, wout_ref, pwbd_ref, lora_ref, pscale_ref, qg_ref, kg_ref,
                   mu_ref, rwv_ref,
                   y_ref, poolo_ref, ko_ref, vo_ref, sho_ref, so_ref, ys_scr, *, bt):
    layer = pl.program_id(0)
    rows = pl.ds(pl.multiple_of(pl.program_id(1) * bt, bt), bt)

    @pl.when(layer == 0)
    def _first_layer_input():
        ys_scr[rows, :] = x_ref[...]

    ones_bd = _block_ones(256, HEAD_DIM)
    x = ys_scr[rows, :]
    hb = _bf(_rmsnorm_rows(x, ng_ref[...]))

    zp = jnp.dot(hb, win_ref[:, OFF_POOL:OFF_POOL + 512], preferred_element_type=F32)
    u = zp[:, 0:256]
    g_pool = zp[:, 256:512]
    buf = pool_ref[...]
    ri = lax.broadcasted_iota(jnp.int32, (POOL_BUF, 1, D_POOL), 0)
    grp3 = lax.broadcasted_iota(jnp.int32, (POOL_BUF, 1, D_POOL), 2) // POOL_CG
    wlen3 = jnp.where(grp3 == 0, 2, jnp.where(grp3 == 1, 4, jnp.where(grp3 == 2, 8, 16)))
    tail = jnp.sum(jnp.where(ri >= POOL_BUF + 1 - wlen3, buf, 0.0), axis=0)
    grp = lax.broadcasted_iota(jnp.int32, (bt, D_POOL), 1) // POOL_CG
    wlen = jnp.where(grp == 0, 2, jnp.where(grp == 1, 4, jnp.where(grp == 2, 8, 16)))
    cnt = jnp.minimum(PAST_LEN + 1, wlen).astype(F32)
    pooled = (tail + u) / cnt - u
    a_out = _mm(pooled, pwbd_ref[...]) * pscale_ref[...] * _silu(g_pool)
    poolo_ref[pl.ds(0, POOL_BUF - 1)] = pool_ref[pl.ds(1, POOL_BUF - 1)]
    poolo_ref[POOL_BUF - 1] = u

    zq = jnp.dot(hb, win_ref[:, OFF_QKV:OFF_QKV + 768], preferred_element_type=F32)
    cos = cos_ref[...]
    sin = sin_ref[...]
    q = _qk_norm_rope(zq[:, 0:512], qg_ref[...], cos, sin, ones_bd) * ATTN_SCALE
    k_new = _qk_norm_rope(zq[:, 512:640], kg_ref[...], cos, sin, ones_bd)
    v_new = zq[:, 640:768]
    si = lax.broadcasted_iota(jnp.int32, (1, D_KV, WINDOW), 2)
    k_win = jnp.where(si == WINDOW - 1, k_new[:, :, None], pltpu.roll(kc_ref[...], WINDOW - 1, 2))
    v_win = jnp.where(si == WINDOW - 1, v_new[:, :, None], pltpu.roll(vc_ref[...], WINDOW - 1, 2))
    ko_ref[...] = k_win
    vo_ref[...] = v_win
    lane_g = lax.broadcasted_iota(jnp.int32, (bt, D_KV), 1) // HEAD_DIM
    q_rows = []
    for h in range(N_HEADS):
        qh = q[:, h * HEAD_DIM:(h + 1) * HEAD_DIM]
        q_rows.append(jnp.where(lane_g == h // Q_PER_KV, jnp.concatenate([qh, qh], axis=-1), 0.0))
    qe = jnp.stack(q_rows, axis=1)
    s = jnp.einsum('bhl,bls->bhs', _bf(qe), _bf(k_win), preferred_element_type=F32)
    hi = lax.broadcasted_iota(jnp.int32, (1, N_HEADS, 1), 1)
    sink = jnp.zeros((1, N_HEADS, 1), F32)
    for h in range(N_HEADS):
        sink = jnp.where(hi == h, sinks_ref[layer, h], sink)
    m = jnp.maximum(jnp.max(s, axis=-1, keepdims=True), sink)
    p = jnp.exp(s - m)
    denom = jnp.sum(p, axis=-1, keepdims=True) + jnp.exp(sink - m)
    o = jnp.einsum('bhs,bls->bhl', _bf(p), _bf(v_win), preferred_element_type=F32) / denom
    b_att = jnp.concatenate(
        [o[:, h, (h // Q_PER_KV) * HEAD_DIM:(h // Q_PER_KV + 1) * HEAD_DIM] for h in range(N_HEADS)],
        axis=-1)
    g_attn = jnp.dot(hb, win_ref[:, OFF_GATTN:OFF_GATTN + 512], preferred_element_type=F32)
    b_out = b_att * _silu(g_attn)

    rin = jnp.dot(hb, win_ref[:, OFF_RIN:OFF_RIN + D_SHIFT_PAD], preferred_element_type=F32)
    xs = rin + (shp_ref[...] - rin) * mu_ref[...]
    sho_ref[...] = rin[:, 0:D_SHIFT]
    rwv = rwv_ref[...]
    r, k2, vv, kk, a, logw = _rwkv_token_params(xs, rwv, lora_ref[...], ones_bd)
    st = s_ref[...]
    kk4 = _heads_rows(kk)[:, :, None, :]
    w4 = _heads_rows(jnp.exp(logw))[:, :, None, :]
    b4 = _heads_rows(kk * a)[:, :, None, :]
    k4 = _heads_rows(k2)[:, :, None, :]
    r4 = _heads_rows(r)[:, :, None, :]
    v_col = _heads_rows(vv)[:, :, :, None]
    sa = jnp.sum(st * kk4, axis=-1, keepdims=True)
    st = st * w4 - sa * b4 + v_col * k4
    so_ref[...] = st
    o4 = jnp.sum(st * r4, axis=-1)
    o_wkv = jnp.concatenate([o4[:, h, :] for h in range(RWKV_HEADS)], axis=-1)
    g_rwkv = jnp.dot(hb, win_ref[:, OFF_GRWKV:OFF_GRWKV + 256], preferred_element_type=F32)
    c_out = _rwkv_finish(o_wkv, r, k2, vv, rwv, ones_bd) * _silu(g_rwkv)

    mix = jnp.concatenate([_bf(a_out), _bf(b_out), _bf(c_out)], axis=-1)
    y = x + jnp.dot(mix, wout_ref[...], preferred_element_type=F32)
    ys_scr[rows, :] = y
    y_ref[...] = y


def _sample_layers(x, pool, kc, vc, shift, wkv, cos, sin, p, bt):
    depth = pool.shape[0]
    nb = x.shape[0]
    assert PAST_LEN >= WINDOW and nb % bt == 0

    def per_layer(shape):
        nd = len(shape)
        return pl.BlockSpec((None,) + tuple(shape), lambda l, i: (l,) + (0,) * nd)

    in_specs = [
        pl.BlockSpec(memory_space=pltpu.SMEM),
        pl.BlockSpec((bt, D_MODEL), lambda l, i: (i, 0)),
        pl.BlockSpec((None, POOL_BUF, bt, D_POOL), lambda l, i: (l, 0, i, 0)),
        pl.BlockSpec((None, bt, D_KV, WINDOW), lambda l, i: (l, i, 0, 0)),
        pl.BlockSpec((None, bt, D_KV, WINDOW), lambda l, i: (l, i, 0, 0)),
        pl.BlockSpec((None, bt, D_SHIFT_PAD), lambda l, i: (l, i, 0)),
        pl.BlockSpec((None, bt, RWKV_HEADS, RWKV_HEAD, RWKV_HEAD), lambda l, i: (l, i, 0, 0, 0)),
        pl.BlockSpec((1, 128), lambda l, i: (0, 0)),
        pl.BlockSpec((1, 128), lambda l, i: (0, 0)),
        per_layer((1, D_MODEL)),
        per_layer((D_MODEL, D_IN_PAD)),
        per_layer((D_MODEL, D_MODEL)),
        per_layer((D_POOL, D_POOL)),
        per_layer((128, 512)),
        per_layer((1, D_POOL)),
        per_layer((1, D_ATTN)),
        per_layer((1, D_KV)),
        per_layer((1, D_SHIFT_PAD)),
        per_layer((8, D_RWKV)),
    ]
    out_shape = (
        jax.ShapeDtypeStruct((depth, nb, D_MODEL), F32),
        jax.ShapeDtypeStruct((depth, POOL_BUF, nb, D_POOL), F32),
        jax.ShapeDtypeStruct((depth, nb, D_KV, WINDOW), F32),
        jax.ShapeDtypeStruct((depth, nb, D_KV, WINDOW), F32),
        jax.ShapeDtypeStruct((depth, nb, D_SHIFT), F32),
        jax.ShapeDtypeStruct((depth, nb, RWKV_HEADS, RWKV_HEAD, RWKV_HEAD), F32),
    )
    out_specs = (
        pl.BlockSpec((None, bt, D_MODEL), lambda l, i: (l, i, 0)),
        pl.BlockSpec((None, POOL_BUF, bt, D_POOL), lambda l, i: (l, 0, i, 0)),
        pl.BlockSpec((None, bt, D_KV, WINDOW), lambda l, i: (l, i, 0, 0)),
        pl.BlockSpec((None, bt, D_KV, WINDOW), lambda l, i: (l, i, 0, 0)),
        pl.BlockSpec((None, bt, D_SHIFT), lambda l, i: (l, i, 0)),
        pl.BlockSpec((None, bt, RWKV_HEADS, RWKV_HEAD, RWKV_HEAD), lambda l, i: (l, i, 0, 0, 0)),
    )
    return pl.pallas_call(
        functools.partial(_sample_kernel, bt=bt),
        grid=(depth, nb // bt),
        in_specs=in_specs,
        out_specs=out_specs,
        out_shape=out_shape,
        scratch_shapes=[pltpu.VMEM((nb, D_MODEL), F32)],
        compiler_params=pltpu.CompilerParams(
            dimension_semantics=("arbitrary", "arbitrary"),
            vmem_limit_bytes=VMEM_LIMIT_BYTES),
        name="sample_layers",
    )(p['sinks'], x, pool, kc, vc, shift, wkv, cos, sin, p['norm_g'], p['w_in'], p['w_out'],
      p['pool_wbd'], p['lora_w'], p['pool_scale'], p['q_g'], p['k_g'], p['mu'], p['rwv'])


def _rope_tables(pos):
    half = HEAD_DIM // 2
    freqs = ROPE_THETA ** (-jnp.arange(half, dtype=F32) / half)
    ang = pos.astype(F32)[:, None] * freqs[None, :]
    c = jnp.cos(ang)
    s = jnp.sin(ang)
    cos = jnp.concatenate([c, c, c, c], axis=-1)
    sin = jnp.concatenate([-s, s, -s, s], axis=-1)
    return cos, sin


def _prep_params(norm_g, w_in, w_out, pool_w, pool_scale, q_norm_g, k_norm_g, attn_sinks,
                 rwkv_mu, rwkv_w0, rwkv_w_up, rwkv_a0, rwkv_a_up, rwkv_k_k, rwkv_k_a, rwkv_r_k,
                 rwkv_ln_g, rwkv_ln_b):
    depth = w_in.shape[0]
    d_rin_end = OFF_RIN + D_SHIFT
    w_in_p = jnp.concatenate(
        [w_in[:, :, :d_rin_end], jnp.zeros((depth, D_MODEL, D_SHIFT_PAD - D_SHIFT), w_in.dtype),
         w_in[:, :, d_rin_end:]], axis=2).astype(BF16)
    groups = len(POOL_WINDOWS)
    bd = jnp.einsum('lgcd,gh->lgchd', pool_w.astype(F32), jnp.eye(groups, dtype=F32))
    bd = bd.reshape(depth, D_POOL, D_POOL)
    zl = jnp.zeros((depth, LORA, D_RWKV), F32)
    lora = jnp.concatenate(
        [jnp.concatenate([rwkv_w_up, zl], axis=2), jnp.concatenate([zl, rwkv_a_up], axis=2),
         jnp.zeros((depth, 128 - 2 * LORA, 2 * D_RWKV), F32)], axis=1)
    rwv = jnp.stack([rwkv_w0, rwkv_a0, rwkv_k_k, rwkv_k_a, rwkv_r_k, rwkv_ln_g, rwkv_ln_b,
                     jnp.zeros((depth, D_RWKV), F32)], axis=1)
    return {
        'sinks': attn_sinks.astype(F32),
        'norm_g': norm_g[:, None, :],
        'w_in': w_in_p,
        'w_out': w_out.astype(BF16),
        'pool_wbd': bd.astype(BF16),
        'lora_w': lora.astype(BF16),
        'pool_scale': pool_scale[:, None, :],
        'q_g': jnp.tile(q_norm_g, (1, N_HEADS))[:, None, :],
        'k_g': jnp.tile(k_norm_g, (1, N_KV))[:, None, :],
        'mu': jnp.pad(rwkv_mu, ((0, 0), (0, D_SHIFT_PAD - D_SHIFT)))[:, None, :],
        'rwv': rwv,
    }


def kernel(x_prompt, x_sample, state_pool, cache_swa_k, cache_swa_v, state_rwkv_shift, state_rwkv_wkv, norm_g, w_in, w_out, pool_w, pool_scale, q_norm_g, k_norm_g, attn_sinks, rwkv_mu, rwkv_w0, rwkv_w_up, rwkv_a0, rwkv_a_up, rwkv_k_k, rwkv_k_a, rwkv_r_k, rwkv_ln_g, rwkv_ln_b):
    depth = w_in.shape[0]
    bsz, seq, _ = x_prompt.shape
    tb = 256 if seq % 256 == 0 else BLOCK
    rows = 2 if bsz % 2 == 0 else 1
    nb = x_sample.shape[0]
    bt = 16 if nb % 16 == 0 else nb
    p = _prep_params(norm_g, w_in, w_out, pool_w, pool_scale, q_norm_g, k_norm_g, attn_sinks,
                     rwkv_mu, rwkv_w0, rwkv_w_up, rwkv_a0, rwkv_a_up, rwkv_k_k, rwkv_k_a,
                     rwkv_r_k, rwkv_ln_g, rwkv_ln_b)

    cos_p, sin_p = _rope_tables(jnp.arange(seq))
    yp = x_prompt
    outs_p = [[] for _ in range(5)]
    for layer in range(depth):
        yp, pool_n, k_n, v_n, sh_n, wkv_n = _prompt_layer(yp, cos_p, sin_p, p, layer, tb, rows)
        outs_p[0].append(pool_n)
        outs_p[1].append(k_n.reshape(bsz, WINDOW, N_KV, HEAD_DIM))
        outs_p[2].append(v_n.reshape(bsz, WINDOW, N_KV, HEAD_DIM))
        outs_p[3].append(sh_n.reshape(bsz, D_SHIFT))
        outs_p[4].append(wkv_n.reshape(bsz, RWKV_HEADS, RWKV_HEAD, RWKV_HEAD))
    pool_p, k_p, v_p, sh_p, wkv_p = [jnp.stack(o) for o in outs_p]

    cos_s, sin_s = _rope_tables(jnp.full((1,), PAST_LEN))
    to_kernel_cache = lambda c: jnp.transpose(c, (0, 1, 3, 4, 2)).reshape(depth, nb, D_KV, WINDOW)
    from_kernel_cache = lambda c: jnp.transpose(
        c.reshape(depth, nb, N_KV, HEAD_DIM, WINDOW), (0, 1, 4, 2, 3))
    ys, pool_s, k_s, v_s, sh_s, wkv_s = _sample_layers(
        x_sample.reshape(nb, D_MODEL),
        jnp.transpose(state_pool, (0, 2, 1, 3)),
        to_kernel_cache(cache_swa_k), to_kernel_cache(cache_swa_v),
        jnp.pad(state_rwkv_shift, ((0, 0), (0, 0), (0, D_SHIFT_PAD - D_SHIFT))),
        state_rwkv_wkv, cos_s, sin_s, p, bt)
    return (yp, ys[depth - 1].reshape(nb, 1, D_MODEL),
            pool_p, jnp.transpose(pool_s, (0, 2, 1, 3)),
            k_p, from_kernel_cache(k_s),
            v_p, from_kernel_cache(v_s),
            sh_p, sh_s,
            wkv_p, wkv_s)
```

```python
import functools

import numpy as np
import jax
import jax.numpy as jnp
from jax import lax
from jax.experimental import pallas as pl
from jax.experimental.pallas import tpu as pltpu

D_MODEL = 1024
D_POOL = 256
POOL_WINDOWS = (2, 4, 8, 16)
POOL_CG = 64
POOL_BUF = 15
HEAD_DIM = 64
D_ATTN = 512
N_HEADS = 8
N_KV = 2
Q_PER_KV = 4
D_KV = 128
WINDOW = 128
BLOCK = 128
ROPE_THETA = 10000.0
QK_EPS = 1e-6
D_RWKV = 256
RWKV_HEAD = 64
RWKV_HEADS = 4
LORA = 32
D_SHIFT = 832
D_SHIFT_PAD = 896
GN_EPS = 64e-5
NORM_EPS = 1e-6
PAST_LEN = 16384
ATTN_SCALE = HEAD_DIM ** -0.5
LOG2E = 1.4426950408889634

OFF_POOL = 0
OFF_QKV = 512
OFF_GATTN = 1280
OFF_RIN = 1792
OFF_GRWKV = 2688
D_IN_PAD = 2944

CHUNK = 64
NEG = -1e30
F32 = jnp.float32
BF16 = jnp.bfloat16

VMEM_LIMIT_BYTES = 56 * 1024 * 1024


def _bf(x):
    return x.astype(BF16)


def _mm(x, y):
    return jnp.dot(_bf(x), _bf(y), preferred_element_type=F32)


def _bnt(x, y):
    return jnp.einsum('bik,bjk->bij', _bf(x), _bf(y), preferred_element_type=F32)


def _bnn(x, y):
    return jnp.einsum('bik,bkj->bij', _bf(x), _bf(y), preferred_element_type=F32)


def _btn(x, y):
    return jnp.einsum('bli,blj->bij', _bf(x), _bf(y), preferred_element_type=F32)


def _sigmoid(x):
    return 0.5 + 0.5 * jnp.tanh(0.5 * x)


def _silu(x):
    half = 0.5 * x
    return half + half * jnp.tanh(half)


def _softplus(x):
    return jnp.maximum(x, 0.0) + jnp.log(1.0 + jnp.exp(-jnp.abs(x)))


def _split3(x):
    h1 = _bf(x)
    r1 = x - h1.astype(F32)
    h2 = _bf(r1)
    h3 = _bf(r1 - h2.astype(F32))
    return h1, h2, h3


def _block_ones(n, seg):
    r = lax.broadcasted_iota(jnp.int32, (n, n), 0) // seg
    c = lax.broadcasted_iota(jnp.int32, (n, n), 1) // seg
    return jnp.where(r == c, 1.0, 0.0).astype(BF16)


def _seg_sum(x, ones_bd):
    w = x.shape[-1]
    step = min(w, ones_bd.shape[0])
    outs = [jnp.dot(_bf(x[:, c0:c0 + step]), ones_bd[:step, :step], preferred_element_type=F32)
            for c0 in range(0, w, step)]
    return outs[0] if len(outs) == 1 else jnp.concatenate(outs, axis=-1)


def _rmsnorm_rows(x, g):
    ms = jnp.mean(x * x, axis=-1, keepdims=True)
    return x * lax.rsqrt(ms + NORM_EPS) * g


def _swap_halves(x):
    w = x.shape[-1]
    lane = lax.broadcasted_iota(jnp.int32, x.shape, x.ndim - 1)
    fwd = pltpu.roll(x, w - HEAD_DIM // 2, x.ndim - 1)
    bwd = pltpu.roll(x, HEAD_DIM // 2, x.ndim - 1)
    return jnp.where(lane % HEAD_DIM < HEAD_DIM // 2, fwd, bwd)


def _qk_norm_rope(x, g, cos, sin_signed, ones_bd):
    ms = _seg_sum(x * x, ones_bd) * (1.0 / HEAD_DIM)
    xn = x * lax.rsqrt(ms + QK_EPS) * g
    reps = x.shape[-1] // cos.shape[-1]
    if reps > 1:
        cos = jnp.concatenate([cos] * reps, axis=-1)
        sin_signed = jnp.concatenate([sin_signed] * reps, axis=-1)
    return xn * cos + _swap_halves(xn) * sin_signed


def _rwkv_token_params(xs, rwv, lora_w, ones_bd):
    r = xs[:, 0:256]
    k = xs[:, 256:512]
    v = xs[:, 512:768]
    lo = xs[:, 768:896]
    lane = lax.broadcasted_iota(jnp.int32, lo.shape, 1)
    lo_in = jnp.where(lane < LORA, jnp.tanh(lo), lo)
    pre = _mm(lo_in, lora_w)
    w_pre = rwv[0:1, :] + pre[:, 0:256]
    a_pre = rwv[1:2, :] + pre[:, 256:512]
    w_log = -_softplus(-w_pre) - 0.5
    logw = -jnp.exp(w_log)
    a = _sigmoid(a_pre)
    kk = k * rwv[2:3, :]
    nrm = jnp.sqrt(_seg_sum(kk * kk, ones_bd))
    kk = kk / jnp.maximum(nrm, 1e-12)
    k2 = k * (1.0 + (a - 1.0) * rwv[3:4, :])
    return r, k2, v, kk, a, logw


def _rwkv_finish(o, r, k2, v, rwv, ones_bd):
    inv_n = 1.0 / RWKV_HEAD
    mu_o = _seg_sum(o, ones_bd) * inv_n
    d = o - mu_o
    var = _seg_sum(d * d, ones_bd) * inv_n
    on = d * lax.rsqrt(var + GN_EPS) * rwv[5:6, :] + rwv[6:7, :]
    bonus = _seg_sum(r * k2 * rwv[4:5, :], ones_bd) * v
    return on + bonus


def _run_interleaved(programs):
    state = [[prog, next(prog), 0] for prog in programs]
    last = 'M'
    while state:
        want = 'V' if last == 'M' else 'M'
        order = sorted(state, key=lambda s: s[2])
        pick = next((s for s in order if s[1] == want), None)
        if pick is None:
            pick = next((s for s in order if s[1] == 'X'), order[0])
        if pick[1] != 'X':
            last = pick[1]
        pick[2] += 1
        try:
            pick[1] = next(pick[0])
        except StopIteration:
            state.remove(pick)


def _prompt_kernel(sinks_ref, x_ref, cos_ref, sin_ref, ng_ref, win_ref, wout_ref, pwbd_ref,
                   lora_ref, pscale_ref, qg_ref, kg_ref, mu_ref, rwv_ref,
                   y_ref, pool_ref, ko_ref, vo_ref, sh_ref, wkv_ref,
                   pext_all, kprev_all, vprev_all, rext_all, s_all, *, tb, rows, layer):
    t = pl.program_id(1)

    @pl.when(t == 0)
    def _init():
        pext_all[...] = jnp.zeros(pext_all.shape, F32)
        kprev_all[...] = jnp.zeros(kprev_all.shape, BF16)
        vprev_all[...] = jnp.zeros(vprev_all.shape, BF16)
        rext_all[:, pl.ds(0, 8), :] = jnp.zeros((rows, 8, D_SHIFT_PAD), F32)
        s_all[...] = jnp.zeros(s_all.shape, F32)

    ones_bd = _block_ones(256, HEAD_DIM)
    programs = [
        _prompt_tile(i, t, layer, ones_bd, sinks_ref, x_ref, cos_ref, sin_ref, ng_ref, win_ref, wout_ref,
                     pwbd_ref, lora_ref, pscale_ref, qg_ref, kg_ref, mu_ref, rwv_ref,
                     y_ref, pool_ref, ko_ref, vo_ref, sh_ref, wkv_ref,
                     pext_all.at[i], kprev_all.at[i], vprev_all.at[i], rext_all.at[i], s_all.at[i],
                     tb=tb)
        for i in range(rows)]
    _run_interleaved(programs)


def _prompt_tile(i, t, layer, ones_bd, sinks_ref, x_ref, cos_ref, sin_ref, ng_ref, win_ref, wout_ref,
                 pwbd_ref, lora_ref, pscale_ref, qg_ref, kg_ref, mu_ref, rwv_ref,
                 y_ref, pool_ref, ko_ref, vo_ref, sh_ref, wkv_ref,
                 pext, kprev, vprev, rext, s_ref, *, tb):
    nqb = tb // BLOCK
    nch = tb // CHUNK

    def proj(off, width):
        return jnp.dot(hb, win_ref[:, off:off + width], preferred_element_type=F32)

    yield 'V'
    x = x_ref[i]
    hb = _bf(_rmsnorm_rows(x, ng_ref[...]))

    yield 'M'
    zp = proj(OFF_POOL, 512)
    yield 'M'
    zq = proj(OFF_QKV, 768)

    yield 'V'
    u = zp[:, 0:256]
    g_pool = zp[:, 256:512]
    pext[pl.ds(24, tb), :] = u
    n_ext = tb + 16
    wins = []
    for sh in (1, 2, 4, 8):
        cur = pext[pl.ds(8, n_ext), :] + pext[pl.ds(8 - sh, n_ext), :]
        pext[pl.ds(8, n_ext), :] = cur
        wins.append(cur[16:16 + tb, :])
    pext[pl.ds(8, 16), :] = u[tb - 16:tb, :]
    lane = lax.broadcasted_iota(jnp.int32, (tb, D_POOL), 1)
    row = lax.broadcasted_iota(jnp.int32, (tb, D_POOL), 0)
    grp = lane // POOL_CG
    win_sum = jnp.where(grp == 0, wins[0], jnp.where(grp == 1, wins[1],
                        jnp.where(grp == 2, wins[2], wins[3])))
    wlen = jnp.where(grp == 0, 2, jnp.where(grp == 1, 4, jnp.where(grp == 2, 8, 16)))
    cnt = jnp.minimum(t * tb + row + 1, wlen).astype(F32)
    pooled = win_sum / cnt - u
    a_out = _mm(pooled, pwbd_ref[...]) * pscale_ref[...] * _silu(g_pool)
    pool_ref[i] = u[tb - POOL_BUF:tb, :]

    yield 'M'
    rin = proj(OFF_RIN, D_SHIFT_PAD)

    yield 'V'
    cos = cos_ref[...]
    sin = sin_ref[...]
    q = _qk_norm_rope(zq[:, 0:512], qg_ref[...], cos, sin, ones_bd) * (ATTN_SCALE * LOG2E)
    qb = _bf(q)

    yield 'M'
    g_attn = proj(OFF_GATTN, 512)
    g_rwkv = proj(OFF_GRWKV, 256)

    yield 'V'
    k = _qk_norm_rope(zq[:, 512:640], kg_ref[...], cos, sin, ones_bd)
    v = zq[:, 640:768]
    kb = _bf(k)
    vb = _bf(v)
    ko_ref[i] = k[tb - WINDOW:tb, :]
    vo_ref[i] = v[tb - WINDOW:tb, :]
    kprev_val = kprev[...]
    vprev_val = vprev[...]
    kprev[...] = kb[tb - BLOCK:tb, :]
    vprev[...] = vb[tb - BLOCK:tb, :]

    nrow = Q_PER_KV * BLOCK
    qi = lax.broadcasted_iota(jnp.int32, (nrow, 2 * BLOCK), 0) % BLOCK
    kj = lax.broadcasted_iota(jnp.int32, (nrow, 2 * BLOCK), 1)
    band = jnp.where(kj > qi, jnp.where(kj <= qi + BLOCK, 0.0, NEG), NEG)
    first_lo = jnp.where(t == 0, BLOCK, 0)
    band_first = jnp.where(kj >= first_lo, band, NEG)
    hrow = lax.broadcasted_iota(jnp.int32, (nrow, 1), 0) // BLOCK

    def attn_unit(j, g):
        rs = slice(j * BLOCK, (j + 1) * BLOCK)
        if j == 0:
            kx = jnp.concatenate([kprev_val, kb[rs]], axis=0)
            vx = jnp.concatenate([vprev_val, vb[rs]], axis=0)
            valid = band_first
        else:
            kx = kb[(j - 1) * BLOCK:(j + 1) * BLOCK]
            vx = vb[(j - 1) * BLOCK:(j + 1) * BLOCK]
            valid = band
        kxg = kx[:, g * HEAD_DIM:(g + 1) * HEAD_DIM]
        vxg = vx[:, g * HEAD_DIM:(g + 1) * HEAD_DIM]
        qs = jnp.concatenate(
            [qb[rs, (g * Q_PER_KV + hq) * HEAD_DIM:(g * Q_PER_KV + hq + 1) * HEAD_DIM]
             for hq in range(Q_PER_KV)], axis=0)
        s = lax.dot_general(qs, kxg, (((1,), (1,)), ((), ())),
                            preferred_element_type=F32)
        s = s + valid
        sink = jnp.where(hrow == 0, sinks_ref[layer, g * Q_PER_KV],
                         jnp.where(hrow == 1, sinks_ref[layer, g * Q_PER_KV + 1],
                                   jnp.where(hrow == 2, sinks_ref[layer, g * Q_PER_KV + 2],
                                             sinks_ref[layer, g * Q_PER_KV + 3]))) * LOG2E
        m = jnp.maximum(jnp.max(s, axis=-1, keepdims=True), sink)
        p = jnp.exp2(s - m)
        denom = jnp.sum(p, axis=-1, keepdims=True) + jnp.exp2(sink - m)
        o = jnp.dot(_bf(p), vxg, preferred_element_type=F32) / denom
        return [o[hq * BLOCK:(hq + 1) * BLOCK, :] for hq in range(Q_PER_KV)]

    attn_todo = [(j, g) for j in range(nqb) for g in range(N_KV)]
    attn_done = {}

    yield 'V'
    rext[pl.ds(8, tb), :] = rin
    prev = rext[pl.ds(7, tb), :]
    rext[pl.ds(7, 1), :] = rin[tb - 1:tb, :]
    xs = rin + (prev - rin) * mu_ref[...]
    sh_ref[i] = rin[tb - 1:tb, 0:D_SHIFT]
    rwv = rwv_ref[...]
    r, k2, vv, kk, a, logw = _rwkv_token_params(xs, rwv, lora_ref[...], ones_bd)

    yield 'M'
    ti = lax.broadcasted_iota(jnp.int32, (tb, tb), 0)
    tj = lax.broadcasted_iota(jnp.int32, (tb, tb), 1)
    tril_blk = jnp.where(ti // CHUNK == tj // CHUNK, jnp.where(tj <= ti, 1.0, 0.0), 0.0).astype(BF16)
    cum = None
    for piece in _split3(logw):
        c = jnp.dot(tril_blk, piece, preferred_element_type=F32)
        cum = c if cum is None else cum + c

    yield 'V'
    cum3 = cum.reshape(nch, CHUNK, D_RWKV)
    cum_last = jnp.broadcast_to(cum3[:, CHUNK - 1:CHUNK, :], cum3.shape).reshape(tb, D_RWKV)
    w_inc = jnp.exp(cum)
    w_exc = jnp.exp(cum - logw)
    w_inv = jnp.exp(-cum)
    rel = jnp.exp(cum_last - cum)
    w_last = jnp.exp(cum_last)
    bvec = kk * a

    def stack(z):
        parts = [z[:, h * RWKV_HEAD:(h + 1) * RWKV_HEAD].reshape(nch, 1, CHUNK, RWKV_HEAD)
                 for h in range(RWKV_HEADS)]
        return jnp.concatenate(parts, axis=1).reshape(nch * RWKV_HEADS, CHUNK, RWKV_HEAD)

    kap = stack(kk * w_exc)
    bt = stack(bvec * w_inv)
    kt = stack(k2 * w_inv)
    rt = stack(r * w_inc)
    bt_l = stack(bvec * rel)
    kt_l = stack(k2 * rel)
    vs = stack(vv)
    wl = stack(w_last)

    yield 'M'
    li = lax.broadcasted_iota(jnp.int32, (1, CHUNK, 2 * CHUNK), 1)
    lj = lax.broadcasted_iota(jnp.int32, (1, CHUNK, 2 * CHUNK), 2)
    lj = jnp.where(lj >= CHUNK, lj - CHUNK, lj)
    btkt = jnp.concatenate([bt, kt], axis=1)
    a_bk = jnp.where(lj < li, _bnt(kap, btkt), 0.0)
    p_bk = jnp.where(lj <= li, _bnt(rt, btkt), 0.0)
    a_b = a_bk[:, :, 0:CHUNK]
    a_k = a_bk[:, :, CHUNK:2 * CHUNK]
    xcat = jnp.concatenate([kap, -_bnn(a_k, vs)], axis=2)
    pw = -a_b
    span = 1
    while 2 * span < CHUNK:
        yield 'M'
        pwb = _bf(pw)
        both = _bnn(pwb, jnp.concatenate([_bf(xcat), pwb], axis=2))
        xcat = xcat + both[:, :, 0:2 * RWKV_HEAD]
        pw = both[:, :, 2 * RWKV_HEAD:2 * RWKV_HEAD + CHUNK]
        span *= 2
        if attn_todo:
            yield 'X'
            jg = attn_todo.pop(0)
            attn_done[jg] = attn_unit(*jg)
    yield 'M'
    xcat = xcat + _bnn(pw, xcat)
    while attn_todo:
        yield 'X'
        jg = attn_todo.pop(0)
        attn_done[jg] = attn_unit(*jg)
    yield 'M'
    xswap = pltpu.roll(xcat, RWKV_HEAD, 2)
    rhs = jnp.concatenate([xswap, jnp.concatenate([vs, jnp.zeros_like(vs)], axis=2)], axis=1)
    o0_pk = _bnn(p_bk, rhs)
    o0 = o0_pk[:, :, 0:RWKV_HEAD]
    rp = rt - o0_pk[:, :, RWKV_HEAD:2 * RWKV_HEAD]
    tn = _btn(bt_l, xcat)
    eye = jnp.where(lax.broadcasted_iota(jnp.int32, (1, CHUNK, CHUNK), 1)
                    == lax.broadcasted_iota(jnp.int32, (1, CHUNK, CHUNK), 2), 1.0, 0.0)
    m_t = eye * wl - tn[:, :, 0:RWKV_HEAD]
    c_t = tn[:, :, RWKV_HEAD:2 * RWKV_HEAD] + _btn(kt_l, vs)
    rp_mt = jnp.concatenate([rp, m_t], axis=1)

    yield 'V'
    b_blocks = [jnp.concatenate(attn_done[(j, 0)] + attn_done[(j, 1)], axis=-1) for j in range(nqb)]
    b_att = b_blocks[0] if nqb == 1 else jnp.concatenate(b_blocks, axis=0)
    b_out = b_att * _silu(g_attn)

    st_cur = s_ref[...]
    o_chunks = []
    for c in range(nch):
        yield 'M'
        sl = slice(c * RWKV_HEADS, (c + 1) * RWKV_HEADS)
        both = _bnn(rp_mt[sl], st_cur)
        o_chunks.append(both[:, 0:CHUNK, :] + o0[sl])
        st_cur = both[:, CHUNK:2 * CHUNK, :] + c_t[sl]
    s_ref[...] = st_cur
    for h in range(RWKV_HEADS):
        wkv_ref[i, pl.ds(h * RWKV_HEAD, RWKV_HEAD), :] = st_cur[h].T

    yield 'V'
    o_heads = []
    for h in range(RWKV_HEADS):
        o_heads.append(jnp.concatenate([oc[h] for oc in o_chunks], axis=0))
    o_wkv = jnp.concatenate(o_heads, axis=-1)
    c_out = _rwkv_finish(o_wkv, r, k2, vv, rwv, ones_bd) * _silu(g_rwkv)

    yield 'M'
    mix = jnp.concatenate([_bf(a_out), _bf(b_out), _bf(c_out)], axis=-1)
    y_ref[i] = x + jnp.dot(mix, wout_ref[...], preferred_element_type=F32)


def _layer_spec(shape, layer):
    nd = len(shape)
    return pl.BlockSpec((None,) + tuple(shape), lambda *_: (layer,) + (0,) * nd)


def _prompt_layer(x, cos, sin, p, layer, tb, rows):
    bsz, seq, _ = x.shape
    nt = seq // tb
    grid = (bsz // rows, nt)
    in_specs = [
        pl.BlockSpec(memory_space=pltpu.SMEM),
        pl.BlockSpec((rows, tb, D_MODEL), lambda b, t: (b, t, 0)),
        pl.BlockSpec((tb, 128), lambda b, t: (t, 0)),
        pl.BlockSpec((tb, 128), lambda b, t: (t, 0)),
        _layer_spec((1, D_MODEL), layer),
        _layer_spec((D_MODEL, D_IN_PAD), layer),
        _layer_spec((D_MODEL, D_MODEL), layer),
        _layer_spec((D_POOL, D_POOL), layer),
        _layer_spec((128, 512), layer),
        _layer_spec((1, D_POOL), layer),
        _layer_spec((1, D_ATTN), layer),
        _layer_spec((1, D_KV), layer),
        _layer_spec((1, D_SHIFT_PAD), layer),
        _layer_spec((8, D_RWKV), layer),
    ]
    out_shape = (
        jax.ShapeDtypeStruct((bsz, seq, D_MODEL), F32),
        jax.ShapeDtypeStruct((bsz, POOL_BUF, D_POOL), F32),
        jax.ShapeDtypeStruct((bsz, WINDOW, D_KV), F32),
        jax.ShapeDtypeStruct((bsz, WINDOW, D_KV), F32),
        jax.ShapeDtypeStruct((bsz, 1, D_SHIFT), F32),
        jax.ShapeDtypeStruct((bsz, RWKV_HEADS * RWKV_HEAD, RWKV_HEAD), F32),
    )
    out_specs = (
        pl.BlockSpec((rows, tb, D_MODEL), lambda b, t: (b, t, 0)),
        pl.BlockSpec((rows, POOL_BUF, D_POOL), lambda b, t: (b, 0, 0)),
        pl.BlockSpec((rows, WINDOW, D_KV), lambda b, t: (b, 0, 0)),
        pl.BlockSpec((rows, WINDOW, D_KV), lambda b, t: (b, 0, 0)),
        pl.BlockSpec((rows, 1, D_SHIFT), lambda b, t: (b, 0, 0)),
        pl.BlockSpec((rows, RWKV_HEADS * RWKV_HEAD, RWKV_HEAD), lambda b, t: (b, 0, 0)),
    )
    scratch = [
        pltpu.VMEM((rows, tb + 24, D_POOL), F32),
        pltpu.VMEM((rows, BLOCK, D_KV), BF16),
        pltpu.VMEM((rows, BLOCK, D_KV), BF16),
        pltpu.VMEM((rows, tb + 8, D_SHIFT_PAD), F32),
        pltpu.VMEM((rows, RWKV_HEADS, RWKV_HEAD, RWKV_HEAD), F32),
    ]
    return pl.pallas_call(
        functools.partial(_prompt_kernel, tb=tb, rows=rows, layer=layer),
        grid=grid,
        in_specs=in_specs,
        out_specs=out_specs,
        out_shape=out_shape,
        scratch_shapes=scratch,
        compiler_params=pltpu.CompilerParams(
            dimension_semantics=("parallel", "arbitrary"),
            vmem_limit_bytes=VMEM_LIMIT_BYTES),
        name="prompt_layer",
    )(p['sinks'], x, cos, sin, p['norm_g'], p['w_in'], p['w_out'], p['pool_wbd'], p['lora_w'],
      p['pool_scale'], p['q_g'], p['k_g'], p['mu'], p['rwv'])


def _sample_kernel(sinks_ref, x_ref, pool_ref, kc_ref, vc_ref, shp_ref, s_ref, cos_ref, sin_ref,
                   ng_ref, win_ref, wout_ref, pwbd_ref, lora_ref, pscale_ref, qg_ref, kg_ref,
                   mu_ref, rwv_ref,
                   y_ref, poolo_ref, ko_ref, vo_ref, sho_ref, so_ref,
                   ys_scr, z_scr, mix_scr, rkv_scr, vec_scr, ot_scr, *, bt, vs):
    layer = pl.program_id(0)
    step = pl.program_id(1)
    rows = pl.ds(pl.multiple_of(step * bt, bt), bt)
    ones_bd = _block_ones(256, HEAD_DIM)

    @pl.when(step == 0)
    def _project_all_sequences():
        @pl.when(layer == 0)
        def _first_layer_input():
            ys_scr[...] = x_ref[...]
        hb_all = _bf(_rmsnorm_rows(ys_scr[...], ng_ref[...]))
        z_scr[...] = jnp.dot(hb_all, win_ref[...], preferred_element_type=F32)
        rin = z_scr[:, OFF_RIN:OFF_RIN + D_SHIFT_PAD]
        xs = rin + (shp_ref[...] - rin) * mu_ref[...]
        sho_ref[...] = rin[:, 0:D_SHIFT]
        r, k2, vv, kk, a, logw = _rwkv_token_params(xs, rwv_ref[...], lora_ref[...], ones_bd)
        rkv_scr[0] = r
        rkv_scr[1] = k2
        rkv_scr[2] = vv
        for n, vec in enumerate((kk, jnp.exp(logw), kk * a, k2, r, vv)):
            vec_scr[n] = vec.T

    zp = z_scr[rows, OFF_POOL:OFF_POOL + 512]
    u = zp[:, 0:256]
    g_pool = zp[:, 256:512]
    buf = pool_ref[...]
    ri = lax.broadcasted_iota(jnp.int32, (POOL_BUF, 1, D_POOL), 0)
    grp3 = lax.broadcasted_iota(jnp.int32, (POOL_BUF, 1, D_POOL), 2) // POOL_CG
    wlen3 = jnp.where(grp3 == 0, 2, jnp.where(grp3 == 1, 4, jnp.where(grp3 == 2, 8, 16)))
    tail = jnp.sum(jnp.where(ri >= POOL_BUF + 1 - wlen3, buf, 0.0), axis=0)
    grp = lax.broadcasted_iota(jnp.int32, (bt, D_POOL), 1) // POOL_CG
    wlen = jnp.where(grp == 0, 2, jnp.where(grp == 1, 4, jnp.where(grp == 2, 8, 16)))
    cnt = jnp.minimum(PAST_LEN + 1, wlen).astype(F32)
    pooled = (tail + u) / cnt - u
    a_out = _mm(pooled, pwbd_ref[...]) * pscale_ref[...] * _silu(g_pool)
    poolo_ref[pl.ds(0, POOL_BUF - 1)] = pool_ref[pl.ds(1, POOL_BUF - 1)]
    poolo_ref[POOL_BUF - 1] = u

    zq = z_scr[rows, OFF_QKV:OFF_QKV + 768]
    cos = cos_ref[...]
    sin = sin_ref[...]
    q = _qk_norm_rope(zq[:, 0:512], qg_ref[...], cos, sin, ones_bd) * ATTN_SCALE
    k_new = _qk_norm_rope(zq[:, 512:640], kg_ref[...], cos, sin, ones_bd)
    v_new = zq[:, 640:768]
    si = lax.broadcasted_iota(jnp.int32, (1, D_KV, WINDOW), 2)
    k_win = jnp.where(si == WINDOW - 1, k_new[:, :, None], pltpu.roll(kc_ref[...], WINDOW - 1, 2))
    v_win = jnp.where(si == WINDOW - 1, v_new[:, :, None], pltpu.roll(vc_ref[...], WINDOW - 1, 2))
    ko_ref[...] = k_win
    vo_ref[...] = v_win
    lane_g = lax.broadcasted_iota(jnp.int32, (bt, D_KV), 1) // HEAD_DIM
    q_rows = []
    for h in range(N_HEADS):
        qh = q[:, h * HEAD_DIM:(h + 1) * HEAD_DIM]
        q_rows.append(jnp.where(lane_g == h // Q_PER_KV, jnp.concatenate([qh, qh], axis=-1), 0.0))
    qe = jnp.stack(q_rows, axis=1)
    s = jnp.einsum('bhl,bls->bhs', _bf(qe), _bf(k_win), preferred_element_type=F32)
    hi = lax.broadcasted_iota(jnp.int32, (1, N_HEADS, 1), 1)
    sink = jnp.zeros((1, N_HEADS, 1), F32)
    for h in range(N_HEADS):
        sink = jnp.where(hi == h, sinks_ref[layer, h], sink)
    m = jnp.maximum(jnp.max(s, axis=-1, keepdims=True), sink)
    p = jnp.exp(s - m)
    denom = jnp.sum(p, axis=-1, keepdims=True) + jnp.exp(sink - m)
    o = jnp.einsum('bhs,bls->bhl', _bf(p), _bf(v_win), preferred_element_type=F32) / denom
    b_att = jnp.concatenate(
        [o[:, h, (h // Q_PER_KV) * HEAD_DIM:(h // Q_PER_KV + 1) * HEAD_DIM] for h in range(N_HEADS)],
        axis=-1)
    g_attn = z_scr[rows, OFF_GATTN:OFF_GATTN + 512]
    b_out = b_att * _silu(g_attn)

    mix_scr[rows, 0:D_POOL + D_ATTN] = jnp.concatenate([a_out, b_out], axis=-1)

    head_rows = pl.ds(pl.multiple_of((step * vs) // RWKV_HEAD * RWKV_HEAD, RWKV_HEAD), RWKV_HEAD)
    val_rows = pl.ds(pl.multiple_of(step * vs, vs), vs)
    kk_h = vec_scr[0, head_rows, :][None]
    w_h = vec_scr[1, head_rows, :][None]
    b_h = vec_scr[2, head_rows, :][None]
    k_h = vec_scr[3, head_rows, :][None]
    r_h = vec_scr[4, head_rows, :][None]
    v_s = vec_scr[5, val_rows, :][:, None, :]
    st = s_ref[...]
    sa = jnp.sum(st * kk_h, axis=1, keepdims=True)
    st = st * w_h - sa * b_h + v_s * k_h
    so_ref[...] = st
    ot_scr[val_rows, :] = jnp.sum(st * r_h, axis=1)

    @pl.when(step == pl.num_programs(1) - 1)
    def _output_projection_all_sequences():
        g_rwkv = z_scr[:, OFF_GRWKV:OFF_GRWKV + 256]
        c_out = _rwkv_finish(ot_scr[...].T, rkv_scr[0], rkv_scr[1], rkv_scr[2], rwv_ref[...], ones_bd)
        mix_scr[:, D_POOL + D_ATTN:D_MODEL] = c_out * _silu(g_rwkv)
        y = ys_scr[...] + jnp.dot(_bf(mix_scr[...]), wout_ref[...], preferred_element_type=F32)
        ys_scr[...] = y
        y_ref[...] = y


def _sample_layers(x, pool, kc, vc, shift, wkv, cos, sin, p, bt):
    depth = pool.shape[0]
    nb = x.shape[0]
    steps = nb // bt
    vs = RWKV_HEADS * RWKV_HEAD // steps
    assert PAST_LEN >= WINDOW and nb % bt == 0 and vs * steps == RWKV_HEADS * RWKV_HEAD
    assert vs <= RWKV_HEAD and RWKV_HEAD % vs == 0 and vs % 8 == 0

    def per_layer(shape):
        nd = len(shape)
        return pl.BlockSpec((None,) + tuple(shape), lambda l, i: (l,) + (0,) * nd)

    in_specs = [
        pl.BlockSpec(memory_space=pltpu.SMEM),
        pl.BlockSpec((nb, D_MODEL), lambda l, i: (0, 0)),
        pl.BlockSpec((None, POOL_BUF, bt, D_POOL), lambda l, i: (l, 0, i, 0)),
        pl.BlockSpec((None, bt, D_KV, WINDOW), lambda l, i: (l, i, 0, 0)),
        pl.BlockSpec((None, bt, D_KV, WINDOW), lambda l, i: (l, i, 0, 0)),
        per_layer((nb, D_SHIFT_PAD)),
        pl.BlockSpec((None, vs, RWKV_HEAD, nb), lambda l, i: (l, i, 0, 0)),
        pl.BlockSpec((1, 128), lambda l, i: (0, 0)),
        pl.BlockSpec((1, 128), lambda l, i: (0, 0)),
        per_layer((1, D_MODEL)),
        per_layer((D_MODEL, D_IN_PAD)),
        per_layer((D_MODEL, D_MODEL)),
        per_layer((D_POOL, D_POOL)),
        per_layer((128, 512)),
        per_layer((1, D_POOL)),
        per_layer((1, D_ATTN)),
        per_layer((1, D_KV)),
        per_layer((1, D_SHIFT_PAD)),
        per_layer((8, D_RWKV)),
    ]
    out_shape = (
        jax.ShapeDtypeStruct((depth, nb, D_MODEL), F32),
        jax.ShapeDtypeStruct((depth, POOL_BUF, nb, D_POOL), F32),
        jax.ShapeDtypeStruct((depth, nb, D_KV, WINDOW), F32),
        jax.ShapeDtypeStruct((depth, nb, D_KV, WINDOW), F32),
        jax.ShapeDtypeStruct((depth, nb, D_SHIFT), F32),
        jax.ShapeDtypeStruct((depth, RWKV_HEADS * RWKV_HEAD, RWKV_HEAD, nb), F32),
    )
    out_specs = (
        per_layer((nb, D_MODEL)),
        pl.BlockSpec((None, POOL_BUF, bt, D_POOL), lambda l, i: (l, 0, i, 0)),
        pl.BlockSpec((None, bt, D_KV, WINDOW), lambda l, i: (l, i, 0, 0)),
        pl.BlockSpec((None, bt, D_KV, WINDOW), lambda l, i: (l, i, 0, 0)),
        per_layer((nb, D_SHIFT)),
        pl.BlockSpec((None, vs, RWKV_HEAD, nb), lambda l, i: (l, i, 0, 0)),
    )
    scratch = [
        pltpu.VMEM((nb, D_MODEL), F32),
        pltpu.VMEM((nb, D_IN_PAD), F32),
        pltpu.VMEM((nb, D_MODEL), F32),
        pltpu.VMEM((3, nb, D_RWKV), F32),
        pltpu.VMEM((6, D_RWKV, nb), F32),
        pltpu.VMEM((D_RWKV, nb), F32),
    ]
    return pl.pallas_call(
        functools.partial(_sample_kernel, bt=bt, vs=vs),
        grid=(depth, steps),
        in_specs=in_specs,
        out_specs=out_specs,
        out_shape=out_shape,
        scratch_shapes=scratch,
        compiler_params=pltpu.CompilerParams(
            dimension_semantics=("arbitrary", "arbitrary"),
            vmem_limit_bytes=VMEM_LIMIT_BYTES),
        name="sample_layers",
    )(p['sinks'], x, pool, kc, vc, shift, wkv, cos, sin, p['norm_g'], p['w_in'], p['w_out'],
      p['pool_wbd'], p['lora_w'], p['pool_scale'], p['q_g'], p['k_g'], p['mu'], p['rwv'])


def _rope_tables(pos):
    half = HEAD_DIM // 2
    freqs = ROPE_THETA ** (-jnp.arange(half, dtype=F32) / half)
    ang = pos.astype(F32)[:, None] * freqs[None, :]
    c = jnp.cos(ang)
    s = jnp.sin(ang)
    cos = jnp.concatenate([c, c, c, c], axis=-1)
    sin = jnp.concatenate([-s, s, -s, s], axis=-1)
    return cos, sin


def _prep_params(norm_g, w_in, w_out, pool_w, pool_scale, q_norm_g, k_norm_g, attn_sinks,
                 rwkv_mu, rwkv_w0, rwkv_w_up, rwkv_a0, rwkv_a_up, rwkv_k_k, rwkv_k_a, rwkv_r_k,
                 rwkv_ln_g, rwkv_ln_b):
    depth = w_in.shape[0]
    d_rin_end = OFF_RIN + D_SHIFT
    w_in_p = jnp.concatenate(
        [w_in[:, :, :d_rin_end], jnp.zeros((depth, D_MODEL, D_SHIFT_PAD - D_SHIFT), w_in.dtype),
         w_in[:, :, d_rin_end:]], axis=2).astype(BF16)
    groups = len(POOL_WINDOWS)
    bd = jnp.einsum('lgcd,gh->lgchd', pool_w.astype(F32), jnp.eye(groups, dtype=F32))
    bd = bd.reshape(depth, D_POOL, D_POOL)
    zl = jnp.zeros((depth, LORA, D_RWKV), F32)
    lora = jnp.concatenate(
        [jnp.concatenate([rwkv_w_up, zl], axis=2), jnp.concatenate([zl, rwkv_a_up], axis=2),
         jnp.zeros((depth, 128 - 2 * LORA, 2 * D_RWKV), F32)], axis=1)
    rwv = jnp.stack([rwkv_w0, rwkv_a0, rwkv_k_k, rwkv_k_a, rwkv_r_k, rwkv_ln_g, rwkv_ln_b,
                     jnp.zeros((depth, D_RWKV), F32)], axis=1)
    return {
        'sinks': attn_sinks.astype(F32),
        'norm_g': norm_g[:, None, :],
        'w_in': w_in_p,
        'w_out': w_out.astype(BF16),
        'pool_wbd': bd.astype(BF16),
        'lora_w': lora.astype(BF16),
        'pool_scale': pool_scale[:, None, :],
        'q_g': jnp.tile(q_norm_g, (1, N_HEADS))[:, None, :],
        'k_g': jnp.tile(k_norm_g, (1, N_KV))[:, None, :],
        'mu': jnp.pad(rwkv_mu, ((0, 0), (0, D_SHIFT_PAD - D_SHIFT)))[:, None, :],
        'rwv': rwv,
    }


def kernel(x_prompt, x_sample, state_pool, cache_swa_k, cache_swa_v, state_rwkv_shift, state_rwkv_wkv, norm_g, w_in, w_out, pool_w, pool_scale, q_norm_g, k_norm_g, attn_sinks, rwkv_mu, rwkv_w0, rwkv_w_up, rwkv_a0, rwkv_a_up, rwkv_k_k, rwkv_k_a, rwkv_r_k, rwkv_ln_g, rwkv_ln_b):
    depth = w_in.shape[0]
    bsz, seq, _ = x_prompt.shape
    tb = 256 if seq % 256 == 0 else BLOCK
    rows = 2 if bsz % 2 == 0 else 1
    nb = x_sample.shape[0]
    bt = 16 if nb % 16 == 0 else nb
    p = _prep_params(norm_g, w_in, w_out, pool_w, pool_scale, q_norm_g, k_norm_g, attn_sinks,
                     rwkv_mu, rwkv_w0, rwkv_w_up, rwkv_a0, rwkv_a_up, rwkv_k_k, rwkv_k_a,
                     rwkv_r_k, rwkv_ln_g, rwkv_ln_b)

    cos_p, sin_p = _rope_tables(jnp.arange(seq))
    yp = x_prompt
    outs_p = [[] for _ in range(5)]
    for layer in range(depth):
        yp, pool_n, k_n, v_n, sh_n, wkv_n = _prompt_layer(yp, cos_p, sin_p, p, layer, tb, rows)
        outs_p[0].append(pool_n)
        outs_p[1].append(k_n.reshape(bsz, WINDOW, N_KV, HEAD_DIM))
        outs_p[2].append(v_n.reshape(bsz, WINDOW, N_KV, HEAD_DIM))
        outs_p[3].append(sh_n.reshape(bsz, D_SHIFT))
        outs_p[4].append(wkv_n.reshape(bsz, RWKV_HEADS, RWKV_HEAD, RWKV_HEAD))
    pool_p, k_p, v_p, sh_p, wkv_p = [jnp.stack(o) for o in outs_p]

    cos_s, sin_s = _rope_tables(jnp.full((1,), PAST_LEN))
    to_kernel_cache = lambda c: jnp.transpose(c, (0, 1, 3, 4, 2)).reshape(depth, nb, D_KV, WINDOW)
    from_kernel_cache = lambda c: jnp.transpose(
        c.reshape(depth, nb, N_KV, HEAD_DIM, WINDOW), (0, 1, 4, 2, 3))
    ys, pool_s, k_s, v_s, sh_s, wkv_s = _sample_layers(
        x_sample.reshape(nb, D_MODEL),
        jnp.transpose(state_pool, (0, 2, 1, 3)),
        to_kernel_cache(cache_swa_k), to_kernel_cache(cache_swa_v),
        jnp.pad(state_rwkv_shift, ((0, 0), (0, 0), (0, D_SHIFT_PAD - D_SHIFT))),
        jnp.transpose(state_rwkv_wkv, (0, 2, 3, 4, 1)).reshape(
            depth, RWKV_HEADS * RWKV_HEAD, RWKV_HEAD, nb),
        cos_s, sin_s, p, bt)
    return (yp, ys[depth - 1].reshape(nb, 1, D_MODEL),
            pool_p, jnp.transpose(pool_s, (0, 2, 1, 3)),
            k_p, from_kernel_cache(k_s),
            v_p, from_kernel_cache(v_s),
            sh_p, sh_s,
            wkv_p, jnp.transpose(
                wkv_s.reshape(depth, RWKV_HEADS, RWKV_HEAD, RWKV_HEAD, nb), (0, 4, 1, 2, 3)))
```

```python
import functools

import numpy as np
import jax
import jax.numpy as jnp
from jax import lax
from jax.experimental import pallas as pl
from jax.experimental.pallas import tpu as pltpu

D_MODEL = 1024
D_POOL = 256
POOL_WINDOWS = (2, 4, 8, 16)
POOL_CG = 64
POOL_BUF = 15
HEAD_DIM = 64
D_ATTN = 512
N_HEADS = 8
N_KV = 2
Q_PER_KV = 4
D_KV = 128
WINDOW = 128
BLOCK = 128
ROPE_THETA = 10000.0
QK_EPS = 1e-6
D_RWKV = 256
RWKV_HEAD = 64
RWKV_HEADS = 4
LORA = 32
D_SHIFT = 832
D_SHIFT_PAD = 896
GN_EPS = 64e-5
NORM_EPS = 1e-6
PAST_LEN = 16384
ATTN_SCALE = HEAD_DIM ** -0.5
LOG2E = 1.4426950408889634

OFF_POOL = 0
OFF_QKV = 512
OFF_GATTN = 1280
OFF_RIN = 1792
OFF_GRWKV = 2688
D_IN_PAD = 2944

CHUNK = 64
NEG = -1e30
F32 = jnp.float32
BF16 = jnp.bfloat16

VMEM_LIMIT_BYTES = 56 * 1024 * 1024


def _bf(x):
    return x.astype(BF16)


def _mm(x, y):
    return jnp.dot(_bf(x), _bf(y), preferred_element_type=F32)


def _bnt(x, y):
    return jnp.einsum('bik,bjk->bij', _bf(x), _bf(y), preferred_element_type=F32)


def _bnn(x, y):
    return jnp.einsum('bik,bkj->bij', _bf(x), _bf(y), preferred_element_type=F32)


def _btn(x, y):
    return jnp.einsum('bli,blj->bij', _bf(x), _bf(y), preferred_element_type=F32)


def _sigmoid(x):
    return 0.5 + 0.5 * jnp.tanh(0.5 * x)


def _silu(x):
    half = 0.5 * x
    return half + half * jnp.tanh(half)


def _softplus(x):
    return jnp.maximum(x, 0.0) + jnp.log(1.0 + jnp.exp(-jnp.abs(x)))


def _split3(x):
    h1 = _bf(x)
    r1 = x - h1.astype(F32)
    h2 = _bf(r1)
    h3 = _bf(r1 - h2.astype(F32))
    return h1, h2, h3


def _block_ones(n, seg):
    r = lax.broadcasted_iota(jnp.int32, (n, n), 0) // seg
    c = lax.broadcasted_iota(jnp.int32, (n, n), 1) // seg
    return jnp.where(r == c, 1.0, 0.0).astype(BF16)


def _seg_sum(x, ones_bd):
    w = x.shape[-1]
    step = min(w, ones_bd.shape[0])
    outs = [jnp.dot(_bf(x[:, c0:c0 + step]), ones_bd[:step, :step], preferred_element_type=F32)
            for c0 in range(0, w, step)]
    return outs[0] if len(outs) == 1 else jnp.concatenate(outs, axis=-1)


def _rmsnorm_rows(x, g):
    ms = jnp.mean(x * x, axis=-1, keepdims=True)
    return x * lax.rsqrt(ms + NORM_EPS) * g


def _swap_halves(x):
    w = x.shape[-1]
    lane = lax.broadcasted_iota(jnp.int32, x.shape, x.ndim - 1)
    fwd = pltpu.roll(x, w - HEAD_DIM // 2, x.ndim - 1)
    bwd = pltpu.roll(x, HEAD_DIM // 2, x.ndim - 1)
    return jnp.where(lane % HEAD_DIM < HEAD_DIM // 2, fwd, bwd)


def _qk_norm_rope(x, g, cos, sin_signed, ones_bd):
    ms = _seg_sum(x * x, ones_bd) * (1.0 / HEAD_DIM)
    xn = x * lax.rsqrt(ms + QK_EPS) * g
    reps = x.shape[-1] // cos.shape[-1]
    if reps > 1:
        cos = jnp.concatenate([cos] * reps, axis=-1)
        sin_signed = jnp.concatenate([sin_signed] * reps, axis=-1)
    return xn * cos + _swap_halves(xn) * sin_signed


def _rwkv_token_params(xs, rwv, lora_w, ones_bd):
    r = xs[:, 0:256]
    k = xs[:, 256:512]
    v = xs[:, 512:768]
    lo = xs[:, 768:896]
    lane = lax.broadcasted_iota(jnp.int32, lo.shape, 1)
    lo_in = jnp.where(lane < LORA, jnp.tanh(lo), lo)
    pre = _mm(lo_in, lora_w)
    w_pre = rwv[0:1, :] + pre[:, 0:256]
    a_pre = rwv[1:2, :] + pre[:, 256:512]
    w_log = -_softplus(-w_pre) - 0.5
    logw = -jnp.exp(w_log)
    a = _sigmoid(a_pre)
    kk = k * rwv[2:3, :]
    nrm = jnp.sqrt(_seg_sum(kk * kk, ones_bd))
    kk = kk / jnp.maximum(nrm, 1e-12)
    k2 = k * (1.0 + (a - 1.0) * rwv[3:4, :])
    return r, k2, v, kk, a, logw


def _rwkv_finish(o, r, k2, v, rwv, ones_bd):
    inv_n = 1.0 / RWKV_HEAD
    mu_o = _seg_sum(o, ones_bd) * inv_n
    d = o - mu_o
    var = _seg_sum(d * d, ones_bd) * inv_n
    on = d * lax.rsqrt(var + GN_EPS) * rwv[5:6, :] + rwv[6:7, :]
    bonus = _seg_sum(r * k2 * rwv[4:5, :], ones_bd) * v
    return on + bonus


def _run_interleaved(programs):
    state = [[prog, next(prog), 0] for prog in programs]
    last = 'M'
    while state:
        want = 'V' if last == 'M' else 'M'
        order = sorted(state, key=lambda s: s[2])
        pick = next((s for s in order if s[1] == want), None)
        if pick is None:
            pick = next((s for s in order if s[1] == 'X'), order[0])
        if pick[1] != 'X':
            last = pick[1]
        pick[2] += 1
        try:
            pick[1] = next(pick[0])
        except StopIteration:
            state.remove(pick)


def _prompt_kernel(sinks_ref, x_ref, cos_ref, sin_ref, ng_ref, win_ref, wout_ref, pwbd_ref,
                   lora_ref, pscale_ref, qg_ref, kg_ref, mu_ref, rwv_ref, wg_ref,
                   y_ref, pool_ref, ko_ref, vo_ref, sh_ref, wkv_ref,
                   pext_all, kprev_all, vprev_all, rext_all, s_all, *, tb, rows, layer):
    t = pl.program_id(1)

    @pl.when(t == 0)
    def _init():
        pext_all[...] = jnp.zeros(pext_all.shape, F32)
        kprev_all[...] = jnp.zeros(kprev_all.shape, BF16)
        vprev_all[...] = jnp.zeros(vprev_all.shape, BF16)
        rext_all[:, pl.ds(0, 8), :] = jnp.zeros((rows, 8, D_SHIFT_PAD), F32)
        s_all[...] = jnp.zeros(s_all.shape, F32)

    ones_bd = _block_ones(256, HEAD_DIM)
    programs = [
        _prompt_tile(i, t, layer, ones_bd, sinks_ref, x_ref, cos_ref, sin_ref, ng_ref, win_ref, wout_ref,
                     pwbd_ref, lora_ref, pscale_ref, qg_ref, kg_ref, mu_ref, rwv_ref, wg_ref,
                     y_ref, pool_ref, ko_ref, vo_ref, sh_ref, wkv_ref,
                     pext_all.at[i], kprev_all.at[i], vprev_all.at[i], rext_all.at[i], s_all.at[i],
                     tb=tb)
        for i in range(rows)]
    _run_interleaved(programs)


def _prompt_tile(i, t, layer, ones_bd, sinks_ref, x_ref, cos_ref, sin_ref, ng_ref, win_ref, wout_ref,
                 pwbd_ref, lora_ref, pscale_ref, qg_ref, kg_ref, mu_ref, rwv_ref, wg_ref,
                 y_ref, pool_ref, ko_ref, vo_ref, sh_ref, wkv_ref,
                 pext, kprev, vprev, rext, s_ref, *, tb):
    nqb = tb // BLOCK
    nch = tb // CHUNK

    def proj(off, width):
        return jnp.dot(hb, win_ref[:, off:off + width], preferred_element_type=F32)

    yield 'V'
    x = x_ref[i]
    hb = _bf(_rmsnorm_rows(x, ng_ref[...]))

    yield 'M'
    zp = proj(OFF_POOL, 512)
    yield 'M'
    zq = proj(OFF_QKV, 768)

    yield 'V'
    u = zp[:, 0:256]
    g_pool = zp[:, 256:512]
    pext[pl.ds(24, tb), :] = u
    n_ext = tb + 16
    wins = []
    for sh in (1, 2, 4, 8):
        cur = pext[pl.ds(8, n_ext), :] + pext[pl.ds(8 - sh, n_ext), :]
        pext[pl.ds(8, n_ext), :] = cur
        wins.append(cur[16:16 + tb, :])
    pext[pl.ds(8, 16), :] = u[tb - 16:tb, :]
    lane = lax.broadcasted_iota(jnp.int32, (tb, D_POOL), 1)
    row = lax.broadcasted_iota(jnp.int32, (tb, D_POOL), 0)
    grp = lane // POOL_CG
    win_sum = jnp.where(grp == 0, wins[0], jnp.where(grp == 1, wins[1],
                        jnp.where(grp == 2, wins[2], wins[3])))
    wlen = jnp.where(grp == 0, 2, jnp.where(grp == 1, 4, jnp.where(grp == 2, 8, 16)))
    cnt = jnp.minimum(t * tb + row + 1, wlen).astype(F32)
    pooled = win_sum / cnt - u
    a_out = _mm(pooled, pwbd_ref[...]) * pscale_ref[...] * _silu(g_pool)
    pool_ref[i] = u[tb - POOL_BUF:tb, :]

    yield 'M'
    rin = proj(OFF_RIN, D_SHIFT_PAD)

    yield 'V'
    cos = cos_ref[...]
    sin = sin_ref[...]
    q = _qk_norm_rope(zq[:, 0:512], qg_ref[...], cos, sin, ones_bd) * (ATTN_SCALE * LOG2E)
    qb = _bf(q)

    yield 'M'
    g_attn = proj(OFF_GATTN, 512)
    g_rwkv = jnp.dot(hb, wg_ref[...], preferred_element_type=F32)

    yield 'V'
    k = _qk_norm_rope(zq[:, 512:640], kg_ref[...], cos, sin, ones_bd)
    v = zq[:, 640:768]
    kb = _bf(k)
    vb = _bf(v)
    ko_ref[i] = k[tb - WINDOW:tb, :]
    vo_ref[i] = v[tb - WINDOW:tb, :]
    kprev_val = kprev[...]
    vprev_val = vprev[...]
    kprev[...] = kb[tb - BLOCK:tb, :]
    vprev[...] = vb[tb - BLOCK:tb, :]

    nrow = Q_PER_KV * BLOCK
    qi = lax.broadcasted_iota(jnp.int32, (nrow, 2 * BLOCK), 0) % BLOCK
    kj = lax.broadcasted_iota(jnp.int32, (nrow, 2 * BLOCK), 1)
    band = jnp.where(kj > qi, jnp.where(kj <= qi + BLOCK, 0.0, NEG), NEG)
    first_lo = jnp.where(t == 0, BLOCK, 0)
    band_first = jnp.where(kj >= first_lo, band, NEG)
    hrow = lax.broadcasted_iota(jnp.int32, (nrow, 1), 0) // BLOCK

    def attn_unit(j, g):
        rs = slice(j * BLOCK, (j + 1) * BLOCK)
        if j == 0:
            kx = jnp.concatenate([kprev_val, kb[rs]], axis=0)
            vx = jnp.concatenate([vprev_val, vb[rs]], axis=0)
            valid = band_first
        else:
            kx = kb[(j - 1) * BLOCK:(j + 1) * BLOCK]
            vx = vb[(j - 1) * BLOCK:(j + 1) * BLOCK]
            valid = band
        kxg = kx[:, g * HEAD_DIM:(g + 1) * HEAD_DIM]
        vxg = vx[:, g * HEAD_DIM:(g + 1) * HEAD_DIM]
        qs = jnp.concatenate(
            [qb[rs, (g * Q_PER_KV + hq) * HEAD_DIM:(g * Q_PER_KV + hq + 1) * HEAD_DIM]
             for hq in range(Q_PER_KV)], axis=0)
        s = lax.dot_general(qs, kxg, (((1,), (1,)), ((), ())),
                            preferred_element_type=F32)
        s = s + valid
        sink = jnp.where(hrow == 0, sinks_ref[layer, g * Q_PER_KV],
                         jnp.where(hrow == 1, sinks_ref[layer, g * Q_PER_KV + 1],
                                   jnp.where(hrow == 2, sinks_ref[layer, g * Q_PER_KV + 2],
                                             sinks_ref[layer, g * Q_PER_KV + 3]))) * LOG2E
        m = jnp.maximum(jnp.max(s, axis=-1, keepdims=True), sink)
        p = jnp.exp2(s - m)
        denom = jnp.sum(p, axis=-1, keepdims=True) + jnp.exp2(sink - m)
        o = jnp.dot(_bf(p), vxg, preferred_element_type=F32) / denom
        return [o[hq * BLOCK:(hq + 1) * BLOCK, :] for hq in range(Q_PER_KV)]

    attn_done = {}
    for jg in [(j, g) for j in range(nqb) for g in range(N_KV)]:
        yield 'X'
        attn_done[jg] = attn_unit(*jg)

    yield 'V'
    rext[pl.ds(8, tb), :] = rin
    prev = rext[pl.ds(7, tb), :]
    rext[pl.ds(7, 1), :] = rin[tb - 1:tb, :]
    xs = rin + (prev - rin) * mu_ref[...]
    sh_ref[i] = rin[tb - 1:tb, 0:D_SHIFT]
    rwv = rwv_ref[...]
    r, k2, vv, kk, a, logw = _rwkv_token_params(xs, rwv, lora_ref[...], ones_bd)

    yield 'M'
    ti = lax.broadcasted_iota(jnp.int32, (tb, tb), 0)
    tj = lax.broadcasted_iota(jnp.int32, (tb, tb), 1)
    tril_blk = jnp.where(ti // CHUNK == tj // CHUNK, jnp.where(tj <= ti, 1.0, 0.0), 0.0).astype(BF16)
    cum = None
    for piece in _split3(logw):
        c = jnp.dot(tril_blk, piece, preferred_element_type=F32)
        cum = c if cum is None else cum + c

    yield 'V'
    cum3 = cum.reshape(nch, CHUNK, D_RWKV)
    cum_last = jnp.broadcast_to(cum3[:, CHUNK - 1:CHUNK, :], cum3.shape).reshape(tb, D_RWKV)
    w_inc = jnp.exp(cum)
    w_exc = jnp.exp(cum - logw)
    w_inv = jnp.exp(-cum)
    rel = jnp.exp(cum_last - cum)
    w_last = jnp.exp(cum_last)
    bvec = kk * a

    def stack(z):
        parts = [z[:, h * RWKV_HEAD:(h + 1) * RWKV_HEAD].reshape(nch, 1, CHUNK, RWKV_HEAD)
                 for h in range(RWKV_HEADS)]
        return jnp.concatenate(parts, axis=1).reshape(nch * RWKV_HEADS, CHUNK, RWKV_HEAD)

    kap = stack(kk * w_exc)
    bt = stack(bvec * w_inv)
    kt = stack(k2 * w_inv)
    rt = stack(r * w_inc)
    bt_l = stack(bvec * rel)
    kt_l = stack(k2 * rel)
    vs = stack(vv)
    wl = stack(w_last)

    yield 'M'
    li = lax.broadcasted_iota(jnp.int32, (1, CHUNK, 2 * CHUNK), 1)
    lj = lax.broadcasted_iota(jnp.int32, (1, CHUNK, 2 * CHUNK), 2)
    lj = jnp.where(lj >= CHUNK, lj - CHUNK, lj)
    btkt = jnp.concatenate([bt, kt], axis=1)
    a_bk = jnp.where(lj < li, _bnt(kap, btkt), 0.0)
    p_bk = jnp.where(lj <= li, _bnt(rt, btkt), 0.0)
    a_b = a_bk[:, :, 0:CHUNK]
    a_k = a_bk[:, :, CHUNK:2 * CHUNK]
    xcat = jnp.concatenate([kap, -_bnn(a_k, vs)], axis=2)
    pw = -a_b
    span = 1
    while 2 * span < CHUNK:
        yield 'M'
        pwb = _bf(pw)
        both = _bnn(pwb, jnp.concatenate([_bf(xcat), pwb], axis=2))
        xcat = xcat + both[:, :, 0:2 * RWKV_HEAD]
        pw = both[:, :, 2 * RWKV_HEAD:2 * RWKV_HEAD + CHUNK]
        span *= 2
    yield 'M'
    xcat = xcat + _bnn(pw, xcat)
    yield 'M'
    xswap = pltpu.roll(xcat, RWKV_HEAD, 2)
    rhs = jnp.concatenate([xswap, jnp.concatenate([vs, jnp.zeros_like(vs)], axis=2)], axis=1)
    o0_pk = _bnn(p_bk, rhs)
    o0 = o0_pk[:, :, 0:RWKV_HEAD]
    rp = rt - o0_pk[:, :, RWKV_HEAD:2 * RWKV_HEAD]
    tn = _btn(bt_l, xcat)
    eye = jnp.where(lax.broadcasted_iota(jnp.int32, (1, CHUNK, CHUNK), 1)
                    == lax.broadcasted_iota(jnp.int32, (1, CHUNK, CHUNK), 2), 1.0, 0.0)
    m_t = eye * wl - tn[:, :, 0:RWKV_HEAD]
    c_t = tn[:, :, RWKV_HEAD:2 * RWKV_HEAD] + _btn(kt_l, vs)
    rp_mt = jnp.concatenate([rp, m_t], axis=1)

    yield 'V'
    b_blocks = [jnp.concatenate(attn_done[(j, 0)] + attn_done[(j, 1)], axis=-1) for j in range(nqb)]
    b_att = b_blocks[0] if nqb == 1 else jnp.concatenate(b_blocks, axis=0)
    b_out = b_att * _silu(g_attn)

    st_cur = s_ref[...]
    o_chunks = []
    for c in range(nch):
        yield 'M'
        sl = slice(c * RWKV_HEADS, (c + 1) * RWKV_HEADS)
        both = _bnn(rp_mt[sl], st_cur)
        o_chunks.append(both[:, 0:CHUNK, :] + o0[sl])
        st_cur = both[:, CHUNK:2 * CHUNK, :] + c_t[sl]
    s_ref[...] = st_cur
    for h in range(RWKV_HEADS):
        wkv_ref[i, pl.ds(h * RWKV_HEAD, RWKV_HEAD), :] = st_cur[h].T

    yield 'V'
    o_heads = []
    for h in range(RWKV_HEADS):
        o_heads.append(jnp.concatenate([oc[h] for oc in o_chunks], axis=0))
    o_wkv = jnp.concatenate(o_heads, axis=-1)
    c_out = _rwkv_finish(o_wkv, r, k2, vv, rwv, ones_bd) * _silu(g_rwkv)

    yield 'M'
    mix = jnp.concatenate([_bf(a_out), _bf(b_out), _bf(c_out)], axis=-1)
    y_ref[i] = x + jnp.dot(mix, wout_ref[...], preferred_element_type=F32)


def _layer_spec(shape, layer):
    nd = len(shape)
    return pl.BlockSpec((None,) + tuple(shape), lambda *_: (layer,) + (0,) * nd)


def _prompt_layer(x, cos, sin, p, layer, tb, rows):
    bsz, seq, _ = x.shape
    nt = seq // tb
    grid = (bsz // rows, nt)
    in_specs = [
        pl.BlockSpec(memory_space=pltpu.SMEM),
        pl.BlockSpec((rows, tb, D_MODEL), lambda b, t: (b, t, 0)),
        pl.BlockSpec((tb, 128), lambda b, t: (t, 0)),
        pl.BlockSpec((tb, 128), lambda b, t: (t, 0)),
        _layer_spec((1, D_MODEL), layer),
        _layer_spec((D_MODEL, OFF_GRWKV), layer),
        _layer_spec((D_MODEL, D_MODEL), layer),
        _layer_spec((D_POOL, D_POOL), layer),
        _layer_spec((128, 512), layer),
        _layer_spec((1, D_POOL), layer),
        _layer_spec((1, D_ATTN), layer),
        _layer_spec((1, D_KV), layer),
        _layer_spec((1, D_SHIFT_PAD), layer),
        _layer_spec((8, D_RWKV), layer),
        _layer_spec((D_MODEL, D_RWKV), layer),
    ]
    out_shape = (
        jax.ShapeDtypeStruct((bsz, seq, D_MODEL), F32),
        jax.ShapeDtypeStruct((bsz, POOL_BUF, D_POOL), F32),
        jax.ShapeDtypeStruct((bsz, WINDOW, D_KV), F32),
        jax.ShapeDtypeStruct((bsz, WINDOW, D_KV), F32),
        jax.ShapeDtypeStruct((bsz, 1, D_SHIFT), F32),
        jax.ShapeDtypeStruct((bsz, RWKV_HEADS * RWKV_HEAD, RWKV_HEAD), F32),
    )
    out_specs = (
        pl.BlockSpec((rows, tb, D_MODEL), lambda b, t: (b, t, 0)),
        pl.BlockSpec((rows, POOL_BUF, D_POOL), lambda b, t: (b, 0, 0)),
        pl.BlockSpec((rows, WINDOW, D_KV), lambda b, t: (b, 0, 0)),
        pl.BlockSpec((rows, WINDOW, D_KV), lambda b, t: (b, 0, 0)),
        pl.BlockSpec((rows, 1, D_SHIFT), lambda b, t: (b, 0, 0)),
        pl.BlockSpec((rows, RWKV_HEADS * RWKV_HEAD, RWKV_HEAD), lambda b, t: (b, 0, 0)),
    )
    scratch = [
        pltpu.VMEM((rows, tb + 24, D_POOL), F32),
        pltpu.VMEM((rows, BLOCK, D_KV), BF16),
        pltpu.VMEM((rows, BLOCK, D_KV), BF16),
        pltpu.VMEM((rows, tb + 8, D_SHIFT_PAD), F32),
        pltpu.VMEM((rows, RWKV_HEADS, RWKV_HEAD, RWKV_HEAD), F32),
    ]
    return pl.pallas_call(
        functools.partial(_prompt_kernel, tb=tb, rows=rows, layer=layer),
        grid=grid,
        in_specs=in_specs,
        out_specs=out_specs,
        out_shape=out_shape,
        scratch_shapes=scratch,
        compiler_params=pltpu.CompilerParams(
            dimension_semantics=("parallel", "arbitrary"),
            vmem_limit_bytes=VMEM_LIMIT_BYTES),
        name="prompt_layer",
    )(p['sinks'], x, cos, sin, p['norm_g'], p['w_in'], p['w_out'], p['pool_wbd'], p['lora_w'],
      p['pool_scale'], p['q_g'], p['k_g'], p['mu'], p['rwv'], p['w_g'])


def _sample_kernel(sinks_ref, x_ref, pool_ref, kc_ref, vc_ref, shp_ref, s_ref, cos_ref, sin_ref,
                   ng_ref, win_ref, wout_ref, pwbd_ref, lora_ref, pscale_ref, qg_ref, kg_ref,
                   mu_ref, rwv_ref, wg_ref,
                   y_ref, poolo_ref, ko_ref, vo_ref, sho_ref, so_ref,
                   ys_scr, z_scr, mix_scr, rkv_scr, vec_scr, ot_scr, *, bt, vs):
    layer = pl.program_id(0)
    step = pl.program_id(1)
    rows = pl.ds(pl.multiple_of(step * bt, bt), bt)
    ones_bd = _block_ones(256, HEAD_DIM)

    @pl.when(step == 0)
    def _project_all_sequences():
        @pl.when(layer == 0)
        def _first_layer_input():
            ys_scr[...] = x_ref[...]
        hb_all = _bf(_rmsnorm_rows(ys_scr[...], ng_ref[...]))
        z_scr[:, 0:OFF_GRWKV] = jnp.dot(hb_all, win_ref[...], preferred_element_type=F32)
        z_scr[:, OFF_GRWKV:D_IN_PAD] = jnp.dot(hb_all, wg_ref[...], preferred_element_type=F32)
        rin = z_scr[:, OFF_RIN:OFF_RIN + D_SHIFT_PAD]
        xs = rin + (shp_ref[...] - rin) * mu_ref[...]
        sho_ref[...] = rin[:, 0:D_SHIFT]
        r, k2, vv, kk, a, logw = _rwkv_token_params(xs, rwv_ref[...], lora_ref[...], ones_bd)
        rkv_scr[0] = r
        rkv_scr[1] = k2
        rkv_scr[2] = vv
        for n, vec in enumerate((kk, jnp.exp(logw), kk * a, k2, r, vv)):
            vec_scr[n] = vec.T

    zp = z_scr[rows, OFF_POOL:OFF_POOL + 512]
    u = zp[:, 0:256]
    g_pool = zp[:, 256:512]
    buf = pool_ref[...]
    ri = lax.broadcasted_iota(jnp.int32, (POOL_BUF, 1, D_POOL), 0)
    grp3 = lax.broadcasted_iota(jnp.int32, (POOL_BUF, 1, D_POOL), 2) // POOL_CG
    wlen3 = jnp.where(grp3 == 0, 2, jnp.where(grp3 == 1, 4, jnp.where(grp3 == 2, 8, 16)))
    tail = jnp.sum(jnp.where(ri >= POOL_BUF + 1 - wlen3, buf, 0.0), axis=0)
    grp = lax.broadcasted_iota(jnp.int32, (bt, D_POOL), 1) // POOL_CG
    wlen = jnp.where(grp == 0, 2, jnp.where(grp == 1, 4, jnp.where(grp == 2, 8, 16)))
    cnt = jnp.minimum(PAST_LEN + 1, wlen).astype(F32)
    pooled = (tail + u) / cnt - u
    a_out = _mm(pooled, pwbd_ref[...]) * pscale_ref[...] * _silu(g_pool)
    poolo_ref[pl.ds(0, POOL_BUF - 1)] = pool_ref[pl.ds(1, POOL_BUF - 1)]
    poolo_ref[POOL_BUF - 1] = u

    zq = z_scr[rows, OFF_QKV:OFF_QKV + 768]
    cos = cos_ref[...]
    sin = sin_ref[...]
    q = _qk_norm_rope(zq[:, 0:512], qg_ref[...], cos, sin, ones_bd) * ATTN_SCALE
    k_new = _qk_norm_rope(zq[:, 512:640], kg_ref[...], cos, sin, ones_bd)
    v_new = zq[:, 640:768]
    si = lax.broadcasted_iota(jnp.int32, (1, D_KV, WINDOW), 2)
    k_win = jnp.where(si == WINDOW - 1, k_new[:, :, None], pltpu.roll(kc_ref[...], WINDOW - 1, 2))
    v_win = jnp.where(si == WINDOW - 1, v_new[:, :, None], pltpu.roll(vc_ref[...], WINDOW - 1, 2))
    ko_ref[...] = k_win
    vo_ref[...] = v_win
    lane_g = lax.broadcasted_iota(jnp.int32, (bt, D_KV), 1) // HEAD_DIM
    q_rows = []
    for h in range(N_HEADS):
        qh = q[:, h * HEAD_DIM:(h + 1) * HEAD_DIM]
        q_rows.append(jnp.where(lane_g == h // Q_PER_KV, jnp.concatenate([qh, qh], axis=-1), 0.0))
    qe = jnp.stack(q_rows, axis=1)
    s = jnp.einsum('bhl,bls->bhs', _bf(qe), _bf(k_win), preferred_element_type=F32)
    hi = lax.broadcasted_iota(jnp.int32, (1, N_HEADS, 1), 1)
    sink = jnp.zeros((1, N_HEADS, 1), F32)
    for h in range(N_HEADS):
        sink = jnp.where(hi == h, sinks_ref[layer, h], sink)
    m = jnp.maximum(jnp.max(s, axis=-1, keepdims=True), sink)
    p = jnp.exp(s - m)
    denom = jnp.sum(p, axis=-1, keepdims=True) + jnp.exp(sink - m)
    o = jnp.einsum('bhs,bls->bhl', _bf(p), _bf(v_win), preferred_element_type=F32) / denom
    b_att = jnp.concatenate(
        [o[:, h, (h // Q_PER_KV) * HEAD_DIM:(h // Q_PER_KV + 1) * HEAD_DIM] for h in range(N_HEADS)],
        axis=-1)
    g_attn = z_scr[rows, OFF_GATTN:OFF_GATTN + 512]
    b_out = b_att * _silu(g_attn)

    mix_scr[rows, 0:D_POOL + D_ATTN] = jnp.concatenate([a_out, b_out], axis=-1)

    head_rows = pl.ds(pl.multiple_of((step * vs) // RWKV_HEAD * RWKV_HEAD, RWKV_HEAD), RWKV_HEAD)
    val_rows = pl.ds(pl.multiple_of(step * vs, vs), vs)
    kk_h = vec_scr[0, head_rows, :][None]
    w_h = vec_scr[1, head_rows, :][None]
    b_h = vec_scr[2, head_rows, :][None]
    k_h = vec_scr[3, head_rows, :][None]
    r_h = vec_scr[4, head_rows, :][None]
    v_s = vec_scr[5, val_rows, :][:, None, :]
    st = s_ref[...]
    sa = jnp.sum(st * kk_h, axis=1, keepdims=True)
    st = st * w_h - sa * b_h + v_s * k_h
    so_ref[...] = st
    ot_scr[val_rows, :] = jnp.sum(st * r_h, axis=1)

    @pl.when(step == pl.num_programs(1) - 1)
    def _output_projection_all_sequences():
        g_rwkv = z_scr[:, OFF_GRWKV:OFF_GRWKV + 256]
        c_out = _rwkv_finish(ot_scr[...].T, rkv_scr[0], rkv_scr[1], rkv_scr[2], rwv_ref[...], ones_bd)
        mix_scr[:, D_POOL + D_ATTN:D_MODEL] = c_out * _silu(g_rwkv)
        y = ys_scr[...] + jnp.dot(_bf(mix_scr[...]), wout_ref[...], preferred_element_type=F32)
        ys_scr[...] = y
        y_ref[...] = y


def _sample_layers(x, pool, kc, vc, shift, wkv, cos, sin, p, bt):
    depth = pool.shape[0]
    nb = x.shape[0]
    steps = nb // bt
    vs = RWKV_HEADS * RWKV_HEAD // steps
    assert PAST_LEN >= WINDOW and nb % bt == 0 and vs * steps == RWKV_HEADS * RWKV_HEAD
    assert vs <= RWKV_HEAD and RWKV_HEAD % vs == 0 and vs % 8 == 0

    def per_layer(shape):
        nd = len(shape)
        return pl.BlockSpec((None,) + tuple(shape), lambda l, i: (l,) + (0,) * nd)

    in_specs = [
        pl.BlockSpec(memory_space=pltpu.SMEM),
        pl.BlockSpec((nb, D_MODEL), lambda l, i: (0, 0)),
        pl.BlockSpec((None, POOL_BUF, bt, D_POOL), lambda l, i: (l, 0, i, 0)),
        pl.BlockSpec((None, bt, D_KV, WINDOW), lambda l, i: (l, i, 0, 0)),
        pl.BlockSpec((None, bt, D_KV, WINDOW), lambda l, i: (l, i, 0, 0)),
        per_layer((nb, D_SHIFT_PAD)),
        pl.BlockSpec((None, vs, RWKV_HEAD, nb), lambda l, i: (l, i, 0, 0)),
        pl.BlockSpec((1, 128), lambda l, i: (0, 0)),
        pl.BlockSpec((1, 128), lambda l, i: (0, 0)),
        per_layer((1, D_MODEL)),
        per_layer((D_MODEL, OFF_GRWKV)),
        per_layer((D_MODEL, D_MODEL)),
        per_layer((D_POOL, D_POOL)),
        per_layer((128, 512)),
        per_layer((1, D_POOL)),
        per_layer((1, D_ATTN)),
        per_layer((1, D_KV)),
        per_layer((1, D_SHIFT_PAD)),
        per_layer((8, D_RWKV)),
        per_layer((D_MODEL, D_RWKV)),
    ]
    out_shape = (
        jax.ShapeDtypeStruct((depth, nb, D_MODEL), F32),
        jax.ShapeDtypeStruct((depth, POOL_BUF, nb, D_POOL), F32),
        jax.ShapeDtypeStruct((depth, nb, D_KV, WINDOW), F32),
        jax.ShapeDtypeStruct((depth, nb, D_KV, WINDOW), F32),
        jax.ShapeDtypeStruct((depth, nb, D_SHIFT), F32),
        jax.ShapeDtypeStruct((depth, RWKV_HEADS * RWKV_HEAD, RWKV_HEAD, nb), F32),
    )
    out_specs = (
        per_layer((nb, D_MODEL)),
        pl.BlockSpec((None, POOL_BUF, bt, D_POOL), lambda l, i: (l, 0, i, 0)),
        pl.BlockSpec((None, bt, D_KV, WINDOW), lambda l, i: (l, i, 0, 0)),
        pl.BlockSpec((None, bt, D_KV, WINDOW), lambda l, i: (l, i, 0, 0)),
        per_layer((nb, D_SHIFT)),
        pl.BlockSpec((None, vs, RWKV_HEAD, nb), lambda l, i: (l, i, 0, 0)),
    )
    scratch = [
        pltpu.VMEM((nb, D_MODEL), F32),
        pltpu.VMEM((nb, D_IN_PAD), F32),
        pltpu.VMEM((nb, D_MODEL), F32),
        pltpu.VMEM((3, nb, D_RWKV), F32),
        pltpu.VMEM((6, D_RWKV, nb), F32),
        pltpu.VMEM((D_RWKV, nb), F32),
    ]
    return pl.pallas_call(
        functools.partial(_sample_kernel, bt=bt, vs=vs),
        grid=(depth, steps),
        in_specs=in_specs,
        out_specs=out_specs,
        out_shape=out_shape,
        scratch_shapes=scratch,
        compiler_params=pltpu.CompilerParams(
            dimension_semantics=("arbitrary", "arbitrary"),
            vmem_limit_bytes=VMEM_LIMIT_BYTES),
        name="sample_layers",
    )(p['sinks'], x, pool, kc, vc, shift, wkv, cos, sin, p['norm_g'], p['w_in'], p['w_out'],
      p['pool_wbd'], p['lora_w'], p['pool_scale'], p['q_g'], p['k_g'], p['mu'], p['rwv'], p['w_g'])


def _rope_tables(pos):
    half = HEAD_DIM // 2
    freqs = ROPE_THETA ** (-jnp.arange(half, dtype=F32) / half)
    ang = pos.astype(F32)[:, None] * freqs[None, :]
    c = jnp.cos(ang)
    s = jnp.sin(ang)
    cos = jnp.concatenate([c, c, c, c], axis=-1)
    sin = jnp.concatenate([-s, s, -s, s], axis=-1)
    return cos, sin


def _prep_params(norm_g, w_in, w_out, pool_w, pool_scale, q_norm_g, k_norm_g, attn_sinks,
                 rwkv_mu, rwkv_w0, rwkv_w_up, rwkv_a0, rwkv_a_up, rwkv_k_k, rwkv_k_a, rwkv_r_k,
                 rwkv_ln_g, rwkv_ln_b):
    depth = w_in.shape[0]
    d_rin_end = OFF_RIN + D_SHIFT
    w_in_p = jnp.pad(w_in[:, :, :d_rin_end].astype(BF16),
                     ((0, 0), (0, 0), (0, D_SHIFT_PAD - D_SHIFT)))
    w_gate = w_in[:, :, d_rin_end:].astype(BF16)
    groups = len(POOL_WINDOWS)
    bd = jnp.einsum('lgcd,gh->lgchd', pool_w.astype(F32), jnp.eye(groups, dtype=F32))
    bd = bd.reshape(depth, D_POOL, D_POOL)
    zl = jnp.zeros((depth, LORA, D_RWKV), F32)
    lora = jnp.concatenate(
        [jnp.concatenate([rwkv_w_up, zl], axis=2), jnp.concatenate([zl, rwkv_a_up], axis=2),
         jnp.zeros((depth, 128 - 2 * LORA, 2 * D_RWKV), F32)], axis=1)
    rwv = jnp.stack([rwkv_w0, rwkv_a0, rwkv_k_k, rwkv_k_a, rwkv_r_k, rwkv_ln_g, rwkv_ln_b,
                     jnp.zeros((depth, D_RWKV), F32)], axis=1)
    return {
        'sinks': attn_sinks.astype(F32),
        'norm_g': norm_g[:, None, :],
        'w_in': w_in_p,
        'w_g': w_gate,
        'w_out': w_out.astype(BF16),
        'pool_wbd': bd.astype(BF16),
        'lora_w': lora.astype(BF16),
        'pool_scale': pool_scale[:, None, :],
        'q_g': jnp.tile(q_norm_g, (1, N_HEADS))[:, None, :],
        'k_g': jnp.tile(k_norm_g, (1, N_KV))[:, None, :],
        'mu': jnp.pad(rwkv_mu, ((0, 0), (0, D_SHIFT_PAD - D_SHIFT)))[:, None, :],
        'rwv': rwv,
    }


def kernel(x_prompt, x_sample, state_pool, cache_swa_k, cache_swa_v, state_rwkv_shift, state_rwkv_wkv, norm_g, w_in, w_out, pool_w, pool_scale, q_norm_g, k_norm_g, attn_sinks, rwkv_mu, rwkv_w0, rwkv_w_up, rwkv_a0, rwkv_a_up, rwkv_k_k, rwkv_k_a, rwkv_r_k, rwkv_ln_g, rwkv_ln_b):
    depth = w_in.shape[0]
    bsz, seq, _ = x_prompt.shape
    tb = 256 if seq % 256 == 0 else BLOCK
    rows = 2 if bsz % 2 == 0 else 1
    nb = x_sample.shape[0]
    bt = 32 if nb % 32 == 0 else nb
    p = _prep_params(norm_g, w_in, w_out, pool_w, pool_scale, q_norm_g, k_norm_g, attn_sinks,
                     rwkv_mu, rwkv_w0, rwkv_w_up, rwkv_a0, rwkv_a_up, rwkv_k_k, rwkv_k_a,
                     rwkv_r_k, rwkv_ln_g, rwkv_ln_b)

    cos_p, sin_p = _rope_tables(jnp.arange(seq))
    yp = x_prompt
    outs_p = [[] for _ in range(5)]
    for layer in range(depth):
        yp, pool_n, k_n, v_n, sh_n, wkv_n = _prompt_layer(yp, cos_p, sin_p, p, layer, tb, rows)
        outs_p[0].append(pool_n)
        outs_p[1].append(k_n.reshape(bsz, WINDOW, N_KV, HEAD_DIM))
        outs_p[2].append(v_n.reshape(bsz, WINDOW, N_KV, HEAD_DIM))
        outs_p[3].append(sh_n.reshape(bsz, D_SHIFT))
        outs_p[4].append(wkv_n.reshape(bsz, RWKV_HEADS, RWKV_HEAD, RWKV_HEAD))
    pool_p, k_p, v_p, sh_p, wkv_p = [jnp.stack(o) for o in outs_p]

    cos_s, sin_s = _rope_tables(jnp.full((1,), PAST_LEN))
    to_kernel_cache = lambda c: jnp.transpose(c, (0, 1, 3, 4, 2)).reshape(depth, nb, D_KV, WINDOW)
    from_kernel_cache = lambda c: jnp.transpose(
        c.reshape(depth, nb, N_KV, HEAD_DIM, WINDOW), (0, 1, 4, 2, 3))
    ys, pool_s, k_s, v_s, sh_s, wkv_s = _sample_layers(
        x_sample.reshape(nb, D_MODEL),
        jnp.transpose(state_pool, (0, 2, 1, 3)),
        to_kernel_cache(cache_swa_k), to_kernel_cache(cache_swa_v),
        jnp.pad(state_rwkv_shift, ((0, 0), (0, 0), (0, D_SHIFT_PAD - D_SHIFT))),
        jnp.transpose(state_rwkv_wkv, (0, 2, 3, 4, 1)).reshape(
            depth, RWKV_HEADS * RWKV_HEAD, RWKV_HEAD, nb),
        cos_s, sin_s, p, bt)
    return (yp, ys[depth - 1].reshape(nb, 1, D_MODEL),
            pool_p, jnp.transpose(pool_s, (0, 2, 1, 3)),
            k_p, from_kernel_cache(k_s),
            v_p, from_kernel_cache(v_s),
            sh_p, sh_s,
            wkv_p, jnp.transpose(
                wkv_s.reshape(depth, RWKV_HEADS, RWKV_HEAD, RWKV_HEAD, nb), (0, 4, 1, 2, 3)))
```

```python
import functools

import numpy as np
import jax
import jax.numpy as jnp
from jax import lax
from jax.experimental import pallas as pl
from jax.experimental.pallas import tpu as pltpu

D_MODEL = 1024
D_POOL = 256
POOL_WINDOWS = (2, 4, 8, 16)
POOL_CG = 64
POOL_BUF = 15
HEAD_DIM = 64
D_ATTN = 512
N_HEADS = 8
N_KV = 2
Q_PER_KV = 4
D_KV = 128
WINDOW = 128
BLOCK = 128
ROPE_THETA = 10000.0
QK_EPS = 1e-6
D_RWKV = 256
RWKV_HEAD = 64
RWKV_HEADS = 4
LORA = 32
D_SHIFT = 832
D_SHIFT_PAD = 896
GN_EPS = 64e-5
NORM_EPS = 1e-6
PAST_LEN = 16384
ATTN_SCALE = HEAD_DIM ** -0.5
LOG2E = 1.4426950408889634

OFF_POOL = 0
OFF_QKV = 512
OFF_GATTN = 1280
OFF_RIN = 1792
OFF_GRWKV = 2688
D_IN_PAD = 2944

CHUNK = 64
NEG = -1e30
F32 = jnp.float32
BF16 = jnp.bfloat16

VMEM_LIMIT_BYTES = 56 * 1024 * 1024


def _bf(x):
    return x.astype(BF16)


def _mm(x, y):
    return jnp.dot(_bf(x), _bf(y), preferred_element_type=F32)


def _bnt(x, y):
    return jnp.einsum('bik,bjk->bij', _bf(x), _bf(y), preferred_element_type=F32)


def _bnn(x, y):
    return jnp.einsum('bik,bkj->bij', _bf(x), _bf(y), preferred_element_type=F32)


def _btn(x, y):
    return jnp.einsum('bli,blj->bij', _bf(x), _bf(y), preferred_element_type=F32)


def _sigmoid(x):
    return 0.5 + 0.5 * jnp.tanh(0.5 * x)


def _silu(x):
    half = 0.5 * x
    return half + half * jnp.tanh(half)


def _softplus(x):
    return jnp.maximum(x, 0.0) + jnp.log(1.0 + jnp.exp(-jnp.abs(x)))


def _split3(x):
    h1 = _bf(x)
    r1 = x - h1.astype(F32)
    h2 = _bf(r1)
    h3 = _bf(r1 - h2.astype(F32))
    return h1, h2, h3


def _block_ones(n, seg):
    r = lax.broadcasted_iota(jnp.int32, (n, n), 0) // seg
    c = lax.broadcasted_iota(jnp.int32, (n, n), 1) // seg
    return jnp.where(r == c, 1.0, 0.0).astype(BF16)


def _seg_sum(x, ones_bd):
    w = x.shape[-1]
    step = min(w, ones_bd.shape[0])
    outs = [jnp.dot(_bf(x[:, c0:c0 + step]), ones_bd[:step, :step], preferred_element_type=F32)
            for c0 in range(0, w, step)]
    return outs[0] if len(outs) == 1 else jnp.concatenate(outs, axis=-1)


def _rmsnorm_rows(x, g):
    ms = jnp.mean(x * x, axis=-1, keepdims=True)
    return x * lax.rsqrt(ms + NORM_EPS) * g


def _swap_halves(x):
    w = x.shape[-1]
    lane = lax.broadcasted_iota(jnp.int32, x.shape, x.ndim - 1)
    fwd = pltpu.roll(x, w - HEAD_DIM // 2, x.ndim - 1)
    bwd = pltpu.roll(x, HEAD_DIM // 2, x.ndim - 1)
    return jnp.where(lane % HEAD_DIM < HEAD_DIM // 2, fwd, bwd)


def _qk_norm_rope(x, g, cos, sin_signed, ones_bd):
    ms = _seg_sum(x * x, ones_bd) * (1.0 / HEAD_DIM)
    xn = x * lax.rsqrt(ms + QK_EPS) * g
    reps = x.shape[-1] // cos.shape[-1]
    if reps > 1:
        cos = jnp.concatenate([cos] * reps, axis=-1)
        sin_signed = jnp.concatenate([sin_signed] * reps, axis=-1)
    return xn * cos + _swap_halves(xn) * sin_signed


def _rwkv_token_params(xs, rwv, lora_w, ones_bd):
    r = xs[:, 0:256]
    k = xs[:, 256:512]
    v = xs[:, 512:768]
    lo = xs[:, 768:896]
    lane = lax.broadcasted_iota(jnp.int32, lo.shape, 1)
    lo_in = jnp.where(lane < LORA, jnp.tanh(lo), lo)
    pre = _mm(lo_in, lora_w)
    w_pre = rwv[0:1, :] + pre[:, 0:256]
    a_pre = rwv[1:2, :] + pre[:, 256:512]
    w_log = -_softplus(-w_pre) - 0.5
    logw = -jnp.exp(w_log)
    a = _sigmoid(a_pre)
    kk = k * rwv[2:3, :]
    nrm = jnp.sqrt(_seg_sum(kk * kk, ones_bd))
    kk = kk / jnp.maximum(nrm, 1e-12)
    k2 = k * (1.0 + (a - 1.0) * rwv[3:4, :])
    return r, k2, v, kk, a, logw


def _rwkv_finish(o, r, k2, v, rwv, ones_bd):
    inv_n = 1.0 / RWKV_HEAD
    mu_o = _seg_sum(o, ones_bd) * inv_n
    d = o - mu_o
    var = _seg_sum(d * d, ones_bd) * inv_n
    on = d * lax.rsqrt(var + GN_EPS) * rwv[5:6, :] + rwv[6:7, :]
    bonus = _seg_sum(r * k2 * rwv[4:5, :], ones_bd) * v
    return on + bonus


def _run_interleaved(programs):
    state = [[prog, next(prog), 0] for prog in programs]
    last = 'M'
    while state:
        want = 'V' if last == 'M' else 'M'
        order = sorted(state, key=lambda s: s[2])
        pick = next((s for s in order if s[1] == want), None)
        if pick is None:
            pick = next((s for s in order if s[1] == 'X'), order[0])
        if pick[1] != 'X':
            last = pick[1]
        pick[2] += 1
        try:
            pick[1] = next(pick[0])
        except StopIteration:
            state.remove(pick)


def _prompt_kernel(sinks_ref, x_ref, cos_ref, sin_ref, ng_ref, win_ref, wout_ref, pwbd_ref,
                   lora_ref, pscale_ref, qg_ref, kg_ref, mu_ref, rwv_ref, wg_ref,
                   y_ref, pool_ref, ko_ref, vo_ref, sh_ref, wkv_ref,
                   pext_all, kprev_all, vprev_all, rext_all, s_all, *, tb, rows, layer):
    t = pl.program_id(1)

    @pl.when(t == 0)
    def _init():
        pext_all[...] = jnp.zeros(pext_all.shape, F32)
        kprev_all[...] = jnp.zeros(kprev_all.shape, BF16)
        vprev_all[...] = jnp.zeros(vprev_all.shape, BF16)
        rext_all[:, pl.ds(0, 8), :] = jnp.zeros((rows, 8, D_SHIFT_PAD), F32)
        s_all[...] = jnp.zeros(s_all.shape, F32)

    ones_bd = _block_ones(256, HEAD_DIM)
    nrow = Q_PER_KV * BLOCK
    qi = lax.broadcasted_iota(jnp.int32, (nrow, 2 * BLOCK), 0) % BLOCK
    kj = lax.broadcasted_iota(jnp.int32, (nrow, 2 * BLOCK), 1)
    band = jnp.where(kj > qi, jnp.where(kj <= qi + BLOCK, 0.0, NEG), NEG)
    first_lo = jnp.where(t == 0, BLOCK, 0)
    band_first = jnp.where(kj >= first_lo, band, NEG)
    hrow = lax.broadcasted_iota(jnp.int32, (nrow, 1), 0) // BLOCK
    ti = lax.broadcasted_iota(jnp.int32, (tb, tb), 0)
    tj = lax.broadcasted_iota(jnp.int32, (tb, tb), 1)
    tril_blk = jnp.where(ti // CHUNK == tj // CHUNK, jnp.where(tj <= ti, 1.0, 0.0), 0.0).astype(BF16)
    shared = (ones_bd, band, band_first, hrow, tril_blk)
    programs = [
        _prompt_tile(i, t, layer, shared, sinks_ref, x_ref, cos_ref, sin_ref, ng_ref, win_ref, wout_ref,
                     pwbd_ref, lora_ref, pscale_ref, qg_ref, kg_ref, mu_ref, rwv_ref, wg_ref,
                     y_ref, pool_ref, ko_ref, vo_ref, sh_ref, wkv_ref,
                     pext_all.at[i], kprev_all.at[i], vprev_all.at[i], rext_all.at[i], s_all.at[i],
                     tb=tb)
        for i in range(rows)]
    _run_interleaved(programs)


def _prompt_tile(i, t, layer, shared, sinks_ref, x_ref, cos_ref, sin_ref, ng_ref, win_ref, wout_ref,
                 pwbd_ref, lora_ref, pscale_ref, qg_ref, kg_ref, mu_ref, rwv_ref, wg_ref,
                 y_ref, pool_ref, ko_ref, vo_ref, sh_ref, wkv_ref,
                 pext, kprev, vprev, rext, s_ref, *, tb):
    nqb = tb // BLOCK
    nch = tb // CHUNK
    ones_bd, band, band_first, hrow, tril_blk = shared

    def proj(off, width):
        return jnp.dot(hb, win_ref[:, off:off + width], preferred_element_type=F32)

    yield 'V'
    x = x_ref[i]
    hb = _bf(_rmsnorm_rows(x, ng_ref[...]))

    yield 'M'
    zp = proj(OFF_POOL, 512)
    yield 'M'
    zq = proj(OFF_QKV, 768)

    yield 'V'
    u = zp[:, 0:256]
    g_pool = zp[:, 256:512]
    pext[pl.ds(24, tb), :] = u
    n_ext = tb + 16
    wins = []
    for sh in (1, 2, 4, 8):
        cur = pext[pl.ds(8, n_ext), :] + pext[pl.ds(8 - sh, n_ext), :]
        pext[pl.ds(8, n_ext), :] = cur
        wins.append(cur[16:16 + tb, :])
    pext[pl.ds(8, 16), :] = u[tb - 16:tb, :]
    lane = lax.broadcasted_iota(jnp.int32, (tb, D_POOL), 1)
    row = lax.broadcasted_iota(jnp.int32, (tb, D_POOL), 0)
    grp = lane // POOL_CG
    win_sum = jnp.where(grp == 0, wins[0], jnp.where(grp == 1, wins[1],
                        jnp.where(grp == 2, wins[2], wins[3])))
    wlen = jnp.where(grp == 0, 2, jnp.where(grp == 1, 4, jnp.where(grp == 2, 8, 16)))
    cnt = jnp.minimum(t * tb + row + 1, wlen).astype(F32)
    pooled = win_sum / cnt - u
    a_out = _mm(pooled, pwbd_ref[...]) * pscale_ref[...] * _silu(g_pool)
    pool_ref[i] = u[tb - POOL_BUF:tb, :]

    yield 'M'
    rin = proj(OFF_RIN, D_SHIFT_PAD)

    yield 'V'
    cos = cos_ref[...]
    sin = sin_ref[...]
    q = _qk_norm_rope(zq[:, 0:512], qg_ref[...], cos, sin, ones_bd) * (ATTN_SCALE * LOG2E)
    qb = _bf(q)

    yield 'M'
    g_attn = proj(OFF_GATTN, 512)
    g_rwkv = jnp.dot(hb, wg_ref[...], preferred_element_type=F32)

    yield 'V'
    k = _qk_norm_rope(zq[:, 512:640], kg_ref[...], cos, sin, ones_bd)
    v = zq[:, 640:768]
    kb = _bf(k)
    vb = _bf(v)
    ko_ref[i] = k[tb - WINDOW:tb, :]
    vo_ref[i] = v[tb - WINDOW:tb, :]
    kprev_val = kprev[...]
    vprev_val = vprev[...]
    kprev[...] = kb[tb - BLOCK:tb, :]
    vprev[...] = vb[tb - BLOCK:tb, :]

    def attn_unit(j, g):
        rs = slice(j * BLOCK, (j + 1) * BLOCK)
        if j == 0:
            kx = jnp.concatenate([kprev_val, kb[rs]], axis=0)
            vx = jnp.concatenate([vprev_val, vb[rs]], axis=0)
            valid = band_first
        else:
            kx = kb[(j - 1) * BLOCK:(j + 1) * BLOCK]
            vx = vb[(j - 1) * BLOCK:(j + 1) * BLOCK]
            valid = band
        kxg = kx[:, g * HEAD_DIM:(g + 1) * HEAD_DIM]
        vxg = vx[:, g * HEAD_DIM:(g + 1) * HEAD_DIM]
        qs = jnp.concatenate(
            [qb[rs, (g * Q_PER_KV + hq) * HEAD_DIM:(g * Q_PER_KV + hq + 1) * HEAD_DIM]
             for hq in range(Q_PER_KV)], axis=0)
        s = lax.dot_general(qs, kxg, (((1,), (1,)), ((), ())),
                            preferred_element_type=F32)
        s = s + valid
        sink = jnp.where(hrow == 0, sinks_ref[layer, g * Q_PER_KV],
                         jnp.where(hrow == 1, sinks_ref[layer, g * Q_PER_KV + 1],
                                   jnp.where(hrow == 2, sinks_ref[layer, g * Q_PER_KV + 2],
                                             sinks_ref[layer, g * Q_PER_KV + 3]))) * LOG2E
        m = jnp.maximum(jnp.max(s, axis=-1, keepdims=True), sink)
        p = jnp.exp2(s - m)
        denom = jnp.sum(p, axis=-1, keepdims=True) + jnp.exp2(sink - m)
        o = jnp.dot(_bf(p), vxg, preferred_element_type=F32) / denom
        return [o[hq * BLOCK:(hq + 1) * BLOCK, :] for hq in range(Q_PER_KV)]

    yield 'V'
    rext[pl.ds(8, tb), :] = rin
    prev = rext[pl.ds(7, tb), :]
    rext[pl.ds(7, 1), :] = rin[tb - 1:tb, :]
    xs = rin + (prev - rin) * mu_ref[...]
    sh_ref[i] = rin[tb - 1:tb, 0:D_SHIFT]
    rwv = rwv_ref[...]
    r, k2, vv, kk, a, logw = _rwkv_token_params(xs, rwv, lora_ref[...], ones_bd)

    attn_done = {}
    for jg in [(j, g) for j in range(nqb) for g in range(N_KV)]:
        yield 'X'
        attn_done[jg] = attn_unit(*jg)

    yield 'M'
    cum = None
    for piece in _split3(logw):
        c = jnp.dot(tril_blk, piece, preferred_element_type=F32)
        cum = c if cum is None else cum + c

    yield 'V'
    cum3 = cum.reshape(nch, CHUNK, D_RWKV)
    cum_last = jnp.broadcast_to(cum3[:, CHUNK - 1:CHUNK, :], cum3.shape).reshape(tb, D_RWKV)
    w_inc = jnp.exp(cum)
    w_exc = jnp.exp(cum - logw)
    w_inv = jnp.exp(-cum)
    rel = jnp.exp(cum_last - cum)
    w_last = jnp.exp(cum_last)
    bvec = kk * a

    def stack(z):
        parts = [z[:, h * RWKV_HEAD:(h + 1) * RWKV_HEAD].reshape(nch, 1, CHUNK, RWKV_HEAD)
                 for h in range(RWKV_HEADS)]
        return jnp.concatenate(parts, axis=1).reshape(nch * RWKV_HEADS, CHUNK, RWKV_HEAD)

    kap = stack(kk * w_exc)
    bt = stack(bvec * w_inv)
    kt = stack(k2 * w_inv)
    rt = stack(r * w_inc)
    bt_l = stack(bvec * rel)
    kt_l = stack(k2 * rel)
    vs = stack(vv)
    wl = stack(w_last)

    yield 'M'
    li = lax.broadcasted_iota(jnp.int32, (1, CHUNK, 2 * CHUNK), 1)
    lj = lax.broadcasted_iota(jnp.int32, (1, CHUNK, 2 * CHUNK), 2)
    lj = jnp.where(lj >= CHUNK, lj - CHUNK, lj)
    btkt = jnp.concatenate([bt, kt], axis=1)
    a_bk = jnp.where(lj < li, _bnt(kap, btkt), 0.0)
    p_bk = jnp.where(lj <= li, _bnt(rt, btkt), 0.0)
    a_b = a_bk[:, :, 0:CHUNK]
    a_k = a_bk[:, :, CHUNK:2 * CHUNK]
    xcat = jnp.concatenate([kap, -_bnn(a_k, vs)], axis=2)
    pw = -a_b
    span = 1
    while 2 * span < CHUNK:
        yield 'M'
        pwb = _bf(pw)
        both = _bnn(pwb, jnp.concatenate([_bf(xcat), pwb], axis=2))
        xcat = xcat + both[:, :, 0:2 * RWKV_HEAD]
        pw = both[:, :, 2 * RWKV_HEAD:2 * RWKV_HEAD + CHUNK]
        span *= 2
    yield 'M'
    xcat = xcat + _bnn(pw, xcat)
    yield 'M'
    xswap = pltpu.roll(xcat, RWKV_HEAD, 2)
    rhs = jnp.concatenate([xswap, jnp.concatenate([vs, jnp.zeros_like(vs)], axis=2)], axis=1)
    o0_pk = _bnn(p_bk, rhs)
    o0 = o0_pk[:, :, 0:RWKV_HEAD]
    rp = rt - o0_pk[:, :, RWKV_HEAD:2 * RWKV_HEAD]
    tn = _btn(bt_l, xcat)
    eye = jnp.where(lax.broadcasted_iota(jnp.int32, (1, CHUNK, CHUNK), 1)
                    == lax.broadcasted_iota(jnp.int32, (1, CHUNK, CHUNK), 2), 1.0, 0.0)
    m_t = eye * wl - tn[:, :, 0:RWKV_HEAD]
    c_t = tn[:, :, RWKV_HEAD:2 * RWKV_HEAD] + _btn(kt_l, vs)
    rp_mt = jnp.concatenate([rp, m_t], axis=1)

    yield 'V'
    b_blocks = [jnp.concatenate(attn_done[(j, 0)] + attn_done[(j, 1)], axis=-1) for j in range(nqb)]
    b_att = b_blocks[0] if nqb == 1 else jnp.concatenate(b_blocks, axis=0)
    b_out = b_att * _silu(g_attn)

    st_cur = s_ref[...]
    o_chunks = []
    for c in range(nch):
        yield 'M'
        sl = slice(c * RWKV_HEADS, (c + 1) * RWKV_HEADS)
        both = _bnn(rp_mt[sl], st_cur)
        o_chunks.append(both[:, 0:CHUNK, :] + o0[sl])
        st_cur = both[:, CHUNK:2 * CHUNK, :] + c_t[sl]
    s_ref[...] = st_cur
    for h in range(RWKV_HEADS):
        wkv_ref[i, pl.ds(h * RWKV_HEAD, RWKV_HEAD), :] = st_cur[h].T

    yield 'V'
    o_heads = []
    for h in range(RWKV_HEADS):
        o_heads.append(jnp.concatenate([oc[h] for oc in o_chunks], axis=0))
    o_wkv = jnp.concatenate(o_heads, axis=-1)
    c_out = _rwkv_finish(o_wkv, r, k2, vv, rwv, ones_bd) * _silu(g_rwkv)

    yield 'M'
    mix = jnp.concatenate([_bf(a_out), _bf(b_out), _bf(c_out)], axis=-1)
    y_ref[i] = x + jnp.dot(mix, wout_ref[...], preferred_element_type=F32)


def _layer_spec(shape, layer):
    nd = len(shape)
    return pl.BlockSpec((None,) + tuple(shape), lambda *_: (layer,) + (0,) * nd)


def _prompt_layer(x, cos, sin, p, layer, tb, rows):
    bsz, seq, _ = x.shape
    nt = seq // tb
    grid = (bsz // rows, nt)
    in_specs = [
        pl.BlockSpec(memory_space=pltpu.SMEM),
        pl.BlockSpec((rows, tb, D_MODEL), lambda b, t: (b, t, 0)),
        pl.BlockSpec((tb, 128), lambda b, t: (t, 0)),
        pl.BlockSpec((tb, 128), lambda b, t: (t, 0)),
        _layer_spec((1, D_MODEL), layer),
        _layer_spec((D_MODEL, OFF_GRWKV), layer),
        _layer_spec((D_MODEL, D_MODEL), layer),
        _layer_spec((D_POOL, D_POOL), layer),
        _layer_spec((128, 512), layer),
        _layer_spec((1, D_POOL), layer),
        _layer_spec((1, D_ATTN), layer),
        _layer_spec((1, D_KV), layer),
        _layer_spec((1, D_SHIFT_PAD), layer),
        _layer_spec((8, D_RWKV), layer),
        _layer_spec((D_MODEL, D_RWKV), layer),
    ]
    out_shape = (
        jax.ShapeDtypeStruct((bsz, seq, D_MODEL), F32),
        jax.ShapeDtypeStruct((bsz, POOL_BUF, D_POOL), F32),
        jax.ShapeDtypeStruct((bsz, WINDOW, D_KV), F32),
        jax.ShapeDtypeStruct((bsz, WINDOW, D_KV), F32),
        jax.ShapeDtypeStruct((bsz, 1, D_SHIFT), F32),
        jax.ShapeDtypeStruct((bsz, RWKV_HEADS * RWKV_HEAD, RWKV_HEAD), F32),
    )
    out_specs = (
        pl.BlockSpec((rows, tb, D_MODEL), lambda b, t: (b, t, 0)),
        pl.BlockSpec((rows, POOL_BUF, D_POOL), lambda b, t: (b, 0, 0)),
        pl.BlockSpec((rows, WINDOW, D_KV), lambda b, t: (b, 0, 0)),
        pl.BlockSpec((rows, WINDOW, D_KV), lambda b, t: (b, 0, 0)),
        pl.BlockSpec((rows, 1, D_SHIFT), lambda b, t: (b, 0, 0)),
        pl.BlockSpec((rows, RWKV_HEADS * RWKV_HEAD, RWKV_HEAD), lambda b, t: (b, 0, 0)),
    )
    scratch = [
        pltpu.VMEM((rows, tb + 24, D_POOL), F32),
        pltpu.VMEM((rows, BLOCK, D_KV), BF16),
        pltpu.VMEM((rows, BLOCK, D_KV), BF16),
        pltpu.VMEM((rows, tb + 8, D_SHIFT_PAD), F32),
        pltpu.VMEM((rows, RWKV_HEADS, RWKV_HEAD, RWKV_HEAD), F32),
    ]
    return pl.pallas_call(
        functools.partial(_prompt_kernel, tb=tb, rows=rows, layer=layer),
        grid=grid,
        in_specs=in_specs,
        out_specs=out_specs,
        out_shape=out_shape,
        scratch_shapes=scratch,
        compiler_params=pltpu.CompilerParams(
            dimension_semantics=("parallel", "arbitrary"),
            vmem_limit_bytes=VMEM_LIMIT_BYTES),
        name="prompt_layer",
    )(p['sinks'], x, cos, sin, p['norm_g'], p['w_in'], p['w_out'], p['pool_wbd'], p['lora_w'],
      p['pool_scale'], p['q_g'], p['k_g'], p['mu'], p['rwv'], p['w_g'])


def _sample_kernel(sinks_ref, x_ref, pool_ref, kc_ref, vc_ref, shp_ref, s_ref, cos_ref, sin_ref,
                   ng_ref, win_ref, wout_ref, pwbd_ref, lora_ref, pscale_ref, qg_ref, kg_ref,
                   mu_ref, rwv_ref, wg_ref,
                   y_ref, poolo_ref, ko_ref, vo_ref, sho_ref, so_ref,
                   ys_scr, z_scr, mix_scr, rkv_scr, vec_scr, ot_scr, *, bt, vs):
    layer = pl.program_id(0)
    step = pl.program_id(1)
    rows = pl.ds(pl.multiple_of(step * bt, bt), bt)
    ones_bd = _block_ones(256, HEAD_DIM)

    @pl.when(step == 0)
    def _project_all_sequences():
        @pl.when(layer == 0)
        def _first_layer_input():
            ys_scr[...] = x_ref[...]
        hb_all = _bf(_rmsnorm_rows(ys_scr[...], ng_ref[...]))
        z_scr[:, 0:OFF_GRWKV] = jnp.dot(hb_all, win_ref[...], preferred_element_type=F32)
        z_scr[:, OFF_GRWKV:D_IN_PAD] = jnp.dot(hb_all, wg_ref[...], preferred_element_type=F32)
        rin = z_scr[:, OFF_RIN:OFF_RIN + D_SHIFT_PAD]
        xs = rin + (shp_ref[...] - rin) * mu_ref[...]
        sho_ref[...] = rin[:, 0:D_SHIFT]
        r, k2, vv, kk, a, logw = _rwkv_token_params(xs, rwv_ref[...], lora_ref[...], ones_bd)
        rkv_scr[0] = r
        rkv_scr[1] = k2
        rkv_scr[2] = vv
        for n, vec in enumerate((kk, jnp.exp(logw), kk * a, k2, r, vv)):
            vec_scr[n] = vec.T

    zp = z_scr[rows, OFF_POOL:OFF_POOL + 512]
    u = zp[:, 0:256]
    g_pool = zp[:, 256:512]
    buf = pool_ref[...]
    ri = lax.broadcasted_iota(jnp.int32, (POOL_BUF, 1, D_POOL), 0)
    grp3 = lax.broadcasted_iota(jnp.int32, (POOL_BUF, 1, D_POOL), 2) // POOL_CG
    wlen3 = jnp.where(grp3 == 0, 2, jnp.where(grp3 == 1, 4, jnp.where(grp3 == 2, 8, 16)))
    tail = jnp.sum(jnp.where(ri >= POOL_BUF + 1 - wlen3, buf, 0.0), axis=0)
    grp = lax.broadcasted_iota(jnp.int32, (bt, D_POOL), 1) // POOL_CG
    wlen = jnp.where(grp == 0, 2, jnp.where(grp == 1, 4, jnp.where(grp == 2, 8, 16)))
    cnt = jnp.minimum(PAST_LEN + 1, wlen).astype(F32)
    pooled = (tail + u) / cnt - u
    a_out = _mm(pooled, pwbd_ref[...]) * pscale_ref[...] * _silu(g_pool)
    poolo_ref[pl.ds(0, POOL_BUF - 1)] = pool_ref[pl.ds(1, POOL_BUF - 1)]
    poolo_ref[POOL_BUF - 1] = u

    zq = z_scr[rows, OFF_QKV:OFF_QKV + 768]
    cos = cos_ref[...]
    sin = sin_ref[...]
    q = _qk_norm_rope(zq[:, 0:512], qg_ref[...], cos, sin, ones_bd) * ATTN_SCALE
    k_new = _qk_norm_rope(zq[:, 512:640], kg_ref[...], cos, sin, ones_bd)
    v_new = zq[:, 640:768]
    si = lax.broadcasted_iota(jnp.int32, (1, D_KV, WINDOW), 2)
    k_win = jnp.where(si == WINDOW - 1, k_new[:, :, None], pltpu.roll(kc_ref[...], WINDOW - 1, 2))
    v_win = jnp.where(si == WINDOW - 1, v_new[:, :, None], pltpu.roll(vc_ref[...], WINDOW - 1, 2))
    ko_ref[...] = k_win
    vo_ref[...] = v_win
    lane_g = lax.broadcasted_iota(jnp.int32, (bt, D_KV), 1) // HEAD_DIM
    q_rows = []
    for h in range(N_HEADS):
        qh = q[:, h * HEAD_DIM:(h + 1) * HEAD_DIM]
        q_rows.append(jnp.where(lane_g == h // Q_PER_KV, jnp.concatenate([qh, qh], axis=-1), 0.0))
    qe = jnp.stack(q_rows, axis=1)
    s = jnp.einsum('bhl,bls->bhs', _bf(qe), _bf(k_win), preferred_element_type=F32)
    hi = lax.broadcasted_iota(jnp.int32, (1, N_HEADS, 1), 1)
    sink = jnp.zeros((1, N_HEADS, 1), F32)
    for h in range(N_HEADS):
        sink = jnp.where(hi == h, sinks_ref[layer, h], sink)
    m = jnp.maximum(jnp.max(s, axis=-1, keepdims=True), sink)
    p = jnp.exp(s - m)
    denom = jnp.sum(p, axis=-1, keepdims=True) + jnp.exp(sink - m)
    o = jnp.einsum('bhs,bls->bhl', _bf(p), _bf(v_win), preferred_element_type=F32) / denom
    b_att = jnp.concatenate(
        [o[:, h, (h // Q_PER_KV) * HEAD_DIM:(h // Q_PER_KV + 1) * HEAD_DIM] for h in range(N_HEADS)],
        axis=-1)
    g_attn = z_scr[rows, OFF_GATTN:OFF_GATTN + 512]
    b_out = b_att * _silu(g_attn)

    mix_scr[rows, 0:D_POOL + D_ATTN] = jnp.concatenate([a_out, b_out], axis=-1)

    head_rows = pl.ds(pl.multiple_of((step * vs) // RWKV_HEAD * RWKV_HEAD, RWKV_HEAD), RWKV_HEAD)
    val_rows = pl.ds(pl.multiple_of(step * vs, vs), vs)
    kk_h = vec_scr[0, head_rows, :][None]
    w_h = vec_scr[1, head_rows, :][None]
    b_h = vec_scr[2, head_rows, :][None]
    k_h = vec_scr[3, head_rows, :][None]
    r_h = vec_scr[4, head_rows, :][None]
    v_s = vec_scr[5, val_rows, :][:, None, :]
    st = s_ref[...]
    sa = jnp.sum(st * kk_h, axis=1, keepdims=True)
    st = st * w_h - sa * b_h + v_s * k_h
    so_ref[...] = st
    ot_scr[val_rows, :] = jnp.sum(st * r_h, axis=1)

    @pl.when(step == pl.num_programs(1) - 1)
    def _output_projection_all_sequences():
        g_rwkv = z_scr[:, OFF_GRWKV:OFF_GRWKV + 256]
        c_out = _rwkv_finish(ot_scr[...].T, rkv_scr[0], rkv_scr[1], rkv_scr[2], rwv_ref[...], ones_bd)
        mix_scr[:, D_POOL + D_ATTN:D_MODEL] = c_out * _silu(g_rwkv)
        y = ys_scr[...] + jnp.dot(_bf(mix_scr[...]), wout_ref[...], preferred_element_type=F32)
        ys_scr[...] = y
        y_ref[...] = y


def _sample_layers(x, pool, kc, vc, shift, wkv, cos, sin, p, bt):
    depth = pool.shape[0]
    nb = x.shape[0]
    steps = nb // bt
    vs = RWKV_HEADS * RWKV_HEAD // steps
    assert PAST_LEN >= WINDOW and nb % bt == 0 and vs * steps == RWKV_HEADS * RWKV_HEAD
    assert vs <= RWKV_HEAD and RWKV_HEAD % vs == 0 and vs % 8 == 0

    def per_layer(shape):
        nd = len(shape)
        return pl.BlockSpec((None,) + tuple(shape), lambda l, i: (l,) + (0,) * nd)

    in_specs = [
        pl.BlockSpec(memory_space=pltpu.SMEM),
        pl.BlockSpec((nb, D_MODEL), lambda l, i: (0, 0)),
        pl.BlockSpec((None, POOL_BUF, bt, D_POOL), lambda l, i: (l, 0, i, 0)),
        pl.BlockSpec((None, bt, D_KV, WINDOW), lambda l, i: (l, i, 0, 0)),
        pl.BlockSpec((None, bt, D_KV, WINDOW), lambda l, i: (l, i, 0, 0)),
        per_layer((nb, D_SHIFT_PAD)),
        pl.BlockSpec((None, vs, RWKV_HEAD, nb), lambda l, i: (l, i, 0, 0)),
        pl.BlockSpec((1, 128), lambda l, i: (0, 0)),
        pl.BlockSpec((1, 128), lambda l, i: (0, 0)),
        per_layer((1, D_MODEL)),
        per_layer((D_MODEL, OFF_GRWKV)),
        per_layer((D_MODEL, D_MODEL)),
        per_layer((D_POOL, D_POOL)),
        per_layer((128, 512)),
        per_layer((1, D_POOL)),
        per_layer((1, D_ATTN)),
        per_layer((1, D_KV)),
        per_layer((1, D_SHIFT_PAD)),
        per_layer((8, D_RWKV)),
        per_layer((D_MODEL, D_RWKV)),
    ]
    out_shape = (
        jax.ShapeDtypeStruct((depth, nb, D_MODEL), F32),
        jax.ShapeDtypeStruct((depth, POOL_BUF, nb, D_POOL), F32),
        jax.ShapeDtypeStruct((depth, nb, D_KV, WINDOW), F32),
        jax.ShapeDtypeStruct((depth, nb, D_KV, WINDOW), F32),
        jax.ShapeDtypeStruct((depth, nb, D_SHIFT), F32),
        jax.ShapeDtypeStruct((depth, RWKV_HEADS * RWKV_HEAD, RWKV_HEAD, nb), F32),
    )
    out_specs = (
        per_layer((nb, D_MODEL)),
        pl.BlockSpec((None, POOL_BUF, bt, D_POOL), lambda l, i: (l, 0, i, 0)),
        pl.BlockSpec((None, bt, D_KV, WINDOW), lambda l, i: (l, i, 0, 0)),
        pl.BlockSpec((None, bt, D_KV, WINDOW), lambda l, i: (l, i, 0, 0)),
        per_layer((nb, D_SHIFT)),
        pl.BlockSpec((None, vs, RWKV_HEAD, nb), lambda l, i: (l, i, 0, 0)),
    )
    scratch = [
        pltpu.VMEM((nb, D_MODEL), F32),
        pltpu.VMEM((nb, D_IN_PAD), F32),
        pltpu.VMEM((nb, D_MODEL), F32),
        pltpu.VMEM((3, nb, D_RWKV), F32),
        pltpu.VMEM((6, D_RWKV, nb), F32),
        pltpu.VMEM((D_RWKV, nb), F32),
    ]
    return pl.pallas_call(
        functools.partial(_sample_kernel, bt=bt, vs=vs),
        grid=(depth, steps),
        in_specs=in_specs,
        out_specs=out_specs,
        out_shape=out_shape,
        scratch_shapes=scratch,
        compiler_params=pltpu.CompilerParams(
            dimension_semantics=("arbitrary", "arbitrary"),
            vmem_limit_bytes=VMEM_LIMIT_BYTES),
        name="sample_layers",
    )(p['sinks'], x, pool, kc, vc, shift, wkv, cos, sin, p['norm_g'], p['w_in'], p['w_out'],
      p['pool_wbd'], p['lora_w'], p['pool_scale'], p['q_g'], p['k_g'], p['mu'], p['rwv'], p['w_g'])


def _rope_tables(pos):
    half = HEAD_DIM // 2
    freqs = ROPE_THETA ** (-jnp.arange(half, dtype=F32) / half)
    ang = pos.astype(F32)[:, None] * freqs[None, :]
    c = jnp.cos(ang)
    s = jnp.sin(ang)
    cos = jnp.concatenate([c, c, c, c], axis=-1)
    sin = jnp.concatenate([-s, s, -s, s], axis=-1)
    return cos, sin


def _prep_params(norm_g, w_in, w_out, pool_w, pool_scale, q_norm_g, k_norm_g, attn_sinks,
                 rwkv_mu, rwkv_w0, rwkv_w_up, rwkv_a0, rwkv_a_up, rwkv_k_k, rwkv_k_a, rwkv_r_k,
                 rwkv_ln_g, rwkv_ln_b):
    depth = w_in.shape[0]
    d_rin_end = OFF_RIN + D_SHIFT
    w_in_p = jnp.pad(w_in[:, :, :d_rin_end].astype(BF16),
                     ((0, 0), (0, 0), (0, D_SHIFT_PAD - D_SHIFT)))
    w_gate = w_in[:, :, d_rin_end:].astype(BF16)
    groups = len(POOL_WINDOWS)
    bd = jnp.einsum('lgcd,gh->lgchd', pool_w.astype(F32), jnp.eye(groups, dtype=F32))
    bd = bd.reshape(depth, D_POOL, D_POOL)
    zl = jnp.zeros((depth, LORA, D_RWKV), F32)
    lora = jnp.concatenate(
        [jnp.concatenate([rwkv_w_up, zl], axis=2), jnp.concatenate([zl, rwkv_a_up], axis=2),
         jnp.zeros((depth, 128 - 2 * LORA, 2 * D_RWKV), F32)], axis=1)
    rwv = jnp.stack([rwkv_w0, rwkv_a0, rwkv_k_k, rwkv_k_a, rwkv_r_k, rwkv_ln_g, rwkv_ln_b,
                     jnp.zeros((depth, D_RWKV), F32)], axis=1)
    return {
        'sinks': attn_sinks.astype(F32),
        'norm_g': norm_g[:, None, :],
        'w_in': w_in_p,
        'w_g': w_gate,
        'w_out': w_out.astype(BF16),
        'pool_wbd': bd.astype(BF16),
        'lora_w': lora.astype(BF16),
        'pool_scale': pool_scale[:, None, :],
        'q_g': jnp.tile(q_norm_g, (1, N_HEADS))[:, None, :],
        'k_g': jnp.tile(k_norm_g, (1, N_KV))[:, None, :],
        'mu': jnp.pad(rwkv_mu, ((0, 0), (0, D_SHIFT_PAD - D_SHIFT)))[:, None, :],
        'rwv': rwv,
    }


def kernel(x_prompt, x_sample, state_pool, cache_swa_k, cache_swa_v, state_rwkv_shift, state_rwkv_wkv, norm_g, w_in, w_out, pool_w, pool_scale, q_norm_g, k_norm_g, attn_sinks, rwkv_mu, rwkv_w0, rwkv_w_up, rwkv_a0, rwkv_a_up, rwkv_k_k, rwkv_k_a, rwkv_r_k, rwkv_ln_g, rwkv_ln_b):
    depth = w_in.shape[0]
    bsz, seq, _ = x_prompt.shape
    tb = 256 if seq % 256 == 0 else BLOCK
    rows = 2 if bsz % 2 == 0 else 1
    nb = x_sample.shape[0]
    bt = 32 if nb % 32 == 0 else nb
    p = _prep_params(norm_g, w_in, w_out, pool_w, pool_scale, q_norm_g, k_norm_g, attn_sinks,
                     rwkv_mu, rwkv_w0, rwkv_w_up, rwkv_a0, rwkv_a_up, rwkv_k_k, rwkv_k_a,
                     rwkv_r_k, rwkv_ln_g, rwkv_ln_b)

    cos_p, sin_p = _rope_tables(jnp.arange(seq))
    yp = x_prompt
    outs_p = [[] for _ in range(5)]
    for layer in range(depth):
        yp, pool_n, k_n, v_n, sh_n, wkv_n = _prompt_layer(yp, cos_p, sin_p, p, layer, tb, rows)
        outs_p[0].append(pool_n)
        outs_p[1].append(k_n.reshape(bsz, WINDOW, N_KV, HEAD_DIM))
        outs_p[2].append(v_n.reshape(bsz, WINDOW, N_KV, HEAD_DIM))
        outs_p[3].append(sh_n.reshape(bsz, D_SHIFT))
        outs_p[4].append(wkv_n.reshape(bsz, RWKV_HEADS, RWKV_HEAD, RWKV_HEAD))
    pool_p, k_p, v_p, sh_p, wkv_p = [jnp.stack(o) for o in outs_p]

    cos_s, sin_s = _rope_tables(jnp.full((1,), PAST_LEN))
    to_kernel_cache = lambda c: jnp.transpose(c, (0, 1, 3, 4, 2)).reshape(depth, nb, D_KV, WINDOW)
    from_kernel_cache = lambda c: jnp.transpose(
        c.reshape(depth, nb, N_KV, HEAD_DIM, WINDOW), (0, 1, 4, 2, 3))
    ys, pool_s, k_s, v_s, sh_s, wkv_s = _sample_layers(
        x_sample.reshape(nb, D_MODEL),
        jnp.transpose(state_pool, (0, 2, 1, 3)),
        to_kernel_cache(cache_swa_k), to_kernel_cache(cache_swa_v),
        jnp.pad(state_rwkv_shift, ((0, 0), (0, 0), (0, D_SHIFT_PAD - D_SHIFT))),
        jnp.transpose(state_rwkv_wkv, (0, 2, 3, 4, 1)).reshape(
            depth, RWKV_HEADS * RWKV_HEAD, RWKV_HEAD, nb),
        cos_s, sin_s, p, bt)
    return (yp, ys[depth - 1].reshape(nb, 1, D_MODEL),
            pool_p, jnp.transpose(pool_s, (0, 2, 1, 3)),
            k_p, from_kernel_cache(k_s),
            v_p, from_kernel_cache(v_s),
            sh_p, sh_s,
            wkv_p, jnp.transpose(
                wkv_s.reshape(depth, RWKV_HEADS, RWKV_HEAD, RWKV_HEAD, nb), (0, 4, 1, 2, 3)))
```

```python
import functools

import numpy as np
import jax
import jax.numpy as jnp
from jax import lax
from jax.experimental import pallas as pl
from jax.experimental.pallas import tpu as pltpu

D_MODEL = 1024
D_POOL = 256
POOL_WINDOWS = (2, 4, 8, 16)
POOL_CG = 64
POOL_BUF = 15
HEAD_DIM = 64
D_ATTN = 512
N_HEADS = 8
N_KV = 2
Q_PER_KV = 4
D_KV = 128
WINDOW = 128
BLOCK = 128
ROPE_THETA = 10000.0
QK_EPS = 1e-6
D_RWKV = 256
RWKV_HEAD = 64
RWKV_HEADS = 4
LORA = 32
D_SHIFT = 832
D_SHIFT_PAD = 896
GN_EPS = 64e-5
NORM_EPS = 1e-6
PAST_LEN = 16384
ATTN_SCALE = HEAD_DIM ** -0.5
LOG2E = 1.4426950408889634

OFF_POOL = 0
OFF_QKV = 512
OFF_GATTN = 1280
OFF_RIN = 1792
OFF_GRWKV = 2688
D_IN_PAD = 2944

CHUNK = 64
NEG = -1e30
F32 = jnp.float32
BF16 = jnp.bfloat16

VMEM_LIMIT_BYTES = 56 * 1024 * 1024


def _bf(x):
    return x.astype(BF16)


def _mm(x, y):
    return jnp.dot(_bf(x), _bf(y), preferred_element_type=F32)


def _bnt(x, y):
    return jnp.einsum('bik,bjk->bij', _bf(x), _bf(y), preferred_element_type=F32)


def _bnn(x, y):
    return jnp.einsum('bik,bkj->bij', _bf(x), _bf(y), preferred_element_type=F32)


def _btn(x, y):
    return jnp.einsum('bli,blj->bij', _bf(x), _bf(y), preferred_element_type=F32)


def _sigmoid(x):
    return 0.5 + 0.5 * jnp.tanh(0.5 * x)


def _silu(x):
    half = 0.5 * x
    return half + half * jnp.tanh(half)


def _softplus(x):
    return jnp.maximum(x, 0.0) + jnp.log(1.0 + jnp.exp(-jnp.abs(x)))


def _split3(x):
    h1 = _bf(x)
    r1 = x - h1.astype(F32)
    h2 = _bf(r1)
    h3 = _bf(r1 - h2.astype(F32))
    return h1, h2, h3


def _block_ones(n, seg):
    r = lax.broadcasted_iota(jnp.int32, (n, n), 0) // seg
    c = lax.broadcasted_iota(jnp.int32, (n, n), 1) // seg
    return jnp.where(r == c, 1.0, 0.0).astype(BF16)


def _seg_sum(x, ones_bd):
    w = x.shape[-1]
    step = min(w, ones_bd.shape[0])
    outs = [jnp.dot(_bf(x[:, c0:c0 + step]), ones_bd[:step, :step], preferred_element_type=F32)
            for c0 in range(0, w, step)]
    return outs[0] if len(outs) == 1 else jnp.concatenate(outs, axis=-1)


def _rmsnorm_rows(x, g):
    ms = jnp.mean(x * x, axis=-1, keepdims=True)
    return x * lax.rsqrt(ms + NORM_EPS) * g


def _swap_halves(x):
    w = x.shape[-1]
    lane = lax.broadcasted_iota(jnp.int32, x.shape, x.ndim - 1)
    fwd = pltpu.roll(x, w - HEAD_DIM // 2, x.ndim - 1)
    bwd = pltpu.roll(x, HEAD_DIM // 2, x.ndim - 1)
    return jnp.where(lane % HEAD_DIM < HEAD_DIM // 2, fwd, bwd)


def _qk_norm_rope(x, g, cos, sin_signed, ones_bd):
    ms = _seg_sum(x * x, ones_bd) * (1.0 / HEAD_DIM)
    xn = x * lax.rsqrt(ms + QK_EPS) * g
    reps = x.shape[-1] // cos.shape[-1]
    if reps > 1:
        cos = jnp.concatenate([cos] * reps, axis=-1)
        sin_signed = jnp.concatenate([sin_signed] * reps, axis=-1)
    return xn * cos + _swap_halves(xn) * sin_signed


def _rwkv_token_params(xs, rwv, lora_w, ones_bd):
    r = xs[:, 0:256]
    k = xs[:, 256:512]
    v = xs[:, 512:768]
    lo = xs[:, 768:896]
    lane = lax.broadcasted_iota(jnp.int32, lo.shape, 1)
    lo_in = jnp.where(lane < LORA, jnp.tanh(lo), lo)
    pre = _mm(lo_in, lora_w)
    w_pre = rwv[0:1, :] + pre[:, 0:256]
    a_pre = rwv[1:2, :] + pre[:, 256:512]
    w_log = -_softplus(-w_pre) - 0.5
    logw = -jnp.exp(w_log)
    a = _sigmoid(a_pre)
    kk = k * rwv[2:3, :]
    nrm = jnp.sqrt(_seg_sum(kk * kk, ones_bd))
    kk = kk / jnp.maximum(nrm, 1e-12)
    k2 = k * (1.0 + (a - 1.0) * rwv[3:4, :])
    return r, k2, v, kk, a, logw


def _rwkv_finish(o, r, k2, v, rwv, ones_bd):
    inv_n = 1.0 / RWKV_HEAD
    mu_o = _seg_sum(o, ones_bd) * inv_n
    d = o - mu_o
    var = _seg_sum(d * d, ones_bd) * inv_n
    on = d * lax.rsqrt(var + GN_EPS) * rwv[5:6, :] + rwv[6:7, :]
    bonus = _seg_sum(r * k2 * rwv[4:5, :], ones_bd) * v
    return on + bonus


def _run_interleaved(programs):
    state = [[prog, next(prog), 0] for prog in programs]
    last = 'M'
    while state:
        want = 'V' if last == 'M' else 'M'
        order = sorted(state, key=lambda s: s[2])
        pick = next((s for s in order if s[1] == want), None)
        if pick is None:
            pick = next((s for s in order if s[1] == 'X'), order[0])
        if pick[1] != 'X':
            last = pick[1]
        pick[2] += 1
        try:
            pick[1] = next(pick[0])
        except StopIteration:
            state.remove(pick)


def _prompt_kernel(sinks_ref, x_ref, cos_ref, sin_ref, ng_ref, win_ref, wout_ref, pwbd_ref,
                   lora_ref, pscale_ref, qg_ref, kg_ref, mu_ref, rwv_ref, wg_ref,
                   y_ref, pool_ref, ko_ref, vo_ref, sh_ref, wkv_ref,
                   pext_all, kprev_all, vprev_all, rext_all, s_all, *, tb, rows, layer):
    t = pl.program_id(1)

    @pl.when(t == 0)
    def _init():
        pext_all[...] = jnp.zeros(pext_all.shape, F32)
        kprev_all[...] = jnp.zeros(kprev_all.shape, BF16)
        vprev_all[...] = jnp.zeros(vprev_all.shape, BF16)
        rext_all[:, pl.ds(0, 8), :] = jnp.zeros((rows, 8, D_SHIFT_PAD), F32)
        s_all[...] = jnp.zeros(s_all.shape, F32)

    ones_bd = _block_ones(256, HEAD_DIM)
    nrow = Q_PER_KV * BLOCK
    qi = lax.broadcasted_iota(jnp.int32, (nrow, 2 * BLOCK), 0) % BLOCK
    kj = lax.broadcasted_iota(jnp.int32, (nrow, 2 * BLOCK), 1)
    band = jnp.where(kj > qi, jnp.where(kj <= qi + BLOCK, 0.0, NEG), NEG)
    first_lo = jnp.where(t == 0, BLOCK, 0)
    band_first = jnp.where(kj >= first_lo, band, NEG)
    hrow = lax.broadcasted_iota(jnp.int32, (nrow, 1), 0) // BLOCK
    ti = lax.broadcasted_iota(jnp.int32, (tb, tb), 0)
    tj = lax.broadcasted_iota(jnp.int32, (tb, tb), 1)
    tril_blk = jnp.where(ti // CHUNK == tj // CHUNK, jnp.where(tj <= ti, 1.0, 0.0), 0.0).astype(BF16)
    shared = (ones_bd, band, band_first, hrow, tril_blk)
    programs = [
        _prompt_tile(i, t, layer, shared, sinks_ref, x_ref, cos_ref, sin_ref, ng_ref, win_ref, wout_ref,
                     pwbd_ref, lora_ref, pscale_ref, qg_ref, kg_ref, mu_ref, rwv_ref, wg_ref,
                     y_ref, pool_ref, ko_ref, vo_ref, sh_ref, wkv_ref,
                     pext_all.at[i], kprev_all.at[i], vprev_all.at[i], rext_all.at[i], s_all.at[i],
                     tb=tb)
        for i in range(rows)]
    _run_interleaved(programs)


def _prompt_tile(i, t, layer, shared, sinks_ref, x_ref, cos_ref, sin_ref, ng_ref, win_ref, wout_ref,
                 pwbd_ref, lora_ref, pscale_ref, qg_ref, kg_ref, mu_ref, rwv_ref, wg_ref,
                 y_ref, pool_ref, ko_ref, vo_ref, sh_ref, wkv_ref,
                 pext, kprev, vprev, rext, s_ref, *, tb):
    nqb = tb // BLOCK
    nch = tb // CHUNK
    ones_bd, band, band_first, hrow, tril_blk = shared

    def proj(off, width):
        return jnp.dot(hb, win_ref[:, off:off + width], preferred_element_type=F32)

    yield 'V'
    x = x_ref[i]
    hb = _bf(_rmsnorm_rows(x, ng_ref[...]))

    yield 'M'
    zp = proj(OFF_POOL, 512)
    yield 'M'
    zq = proj(OFF_QKV, 768)

    yield 'V'
    u = zp[:, 0:256]
    g_pool = zp[:, 256:512]
    pext[pl.ds(24, tb), :] = u
    n_ext = tb + 16
    wins = []
    for sh in (1, 2, 4, 8):
        cur = pext[pl.ds(8, n_ext), :] + pext[pl.ds(8 - sh, n_ext), :]
        pext[pl.ds(8, n_ext), :] = cur
        wins.append(cur[16:16 + tb, :])
    pext[pl.ds(8, 16), :] = u[tb - 16:tb, :]
    lane = lax.broadcasted_iota(jnp.int32, (tb, D_POOL), 1)
    row = lax.broadcasted_iota(jnp.int32, (tb, D_POOL), 0)
    grp = lane // POOL_CG
    win_sum = jnp.where(grp == 0, wins[0], jnp.where(grp == 1, wins[1],
                        jnp.where(grp == 2, wins[2], wins[3])))
    wlen = jnp.where(grp == 0, 2, jnp.where(grp == 1, 4, jnp.where(grp == 2, 8, 16)))
    cnt = jnp.minimum(t * tb + row + 1, wlen).astype(F32)
    pooled = win_sum / cnt - u
    a_out = _mm(pooled, pwbd_ref[...]) * pscale_ref[...] * _silu(g_pool)
    pool_ref[i] = u[tb - POOL_BUF:tb, :]

    yield 'M'
    rin = proj(OFF_RIN, D_SHIFT_PAD)

    yield 'V'
    cos = cos_ref[...]
    sin = sin_ref[...]
    q = _qk_norm_rope(zq[:, 0:512], qg_ref[...], cos, sin, ones_bd) * (ATTN_SCALE * LOG2E)
    qb = _bf(q)

    yield 'M'
    g_attn = proj(OFF_GATTN, 512)
    g_rwkv = jnp.dot(hb, wg_ref[...], preferred_element_type=F32)

    yield 'V'
    k = _qk_norm_rope(zq[:, 512:640], kg_ref[...], cos, sin, ones_bd)
    v = zq[:, 640:768]
    kb = _bf(k)
    vb = _bf(v)
    ko_ref[i] = k[tb - WINDOW:tb, :]
    vo_ref[i] = v[tb - WINDOW:tb, :]
    kprev_val = kprev[...]
    vprev_val = vprev[...]
    kprev[...] = kb[tb - BLOCK:tb, :]
    vprev[...] = vb[tb - BLOCK:tb, :]

    def attn_unit(j, g):
        rs = slice(j * BLOCK, (j + 1) * BLOCK)
        if j == 0:
            kx = jnp.concatenate([kprev_val, kb[rs]], axis=0)
            vx = jnp.concatenate([vprev_val, vb[rs]], axis=0)
            valid = band_first
        else:
            kx = kb[(j - 1) * BLOCK:(j + 1) * BLOCK]
            vx = vb[(j - 1) * BLOCK:(j + 1) * BLOCK]
            valid = band
        kxg = kx[:, g * HEAD_DIM:(g + 1) * HEAD_DIM]
        vxg = vx[:, g * HEAD_DIM:(g + 1) * HEAD_DIM]
        qs = jnp.concatenate(
            [qb[rs, (g * Q_PER_KV + hq) * HEAD_DIM:(g * Q_PER_KV + hq + 1) * HEAD_DIM]
             for hq in range(Q_PER_KV)], axis=0)
        s = lax.dot_general(qs, kxg, (((1,), (1,)), ((), ())),
                            preferred_element_type=F32)
        s = s + valid
        sink = jnp.where(hrow == 0, sinks_ref[layer, g * Q_PER_KV],
                         jnp.where(hrow == 1, sinks_ref[layer, g * Q_PER_KV + 1],
                                   jnp.where(hrow == 2, sinks_ref[layer, g * Q_PER_KV + 2],
                                             sinks_ref[layer, g * Q_PER_KV + 3]))) * LOG2E
        m = jnp.maximum(jnp.max(s, axis=-1, keepdims=True), sink)
        p = jnp.exp2(s - m)
        denom = jnp.sum(p, axis=-1, keepdims=True) + jnp.exp2(sink - m)
        o = jnp.dot(_bf(p), vxg, preferred_element_type=F32) / denom
        return [o[hq * BLOCK:(hq + 1) * BLOCK, :] for hq in range(Q_PER_KV)]

    yield 'V'
    rext[pl.ds(8, tb), :] = rin
    prev = rext[pl.ds(7, tb), :]
    rext[pl.ds(7, 1), :] = rin[tb - 1:tb, :]
    xs = rin + (prev - rin) * mu_ref[...]
    sh_ref[i] = rin[tb - 1:tb, 0:D_SHIFT]
    rwv = rwv_ref[...]
    r, k2, vv, kk, a, logw = _rwkv_token_params(xs, rwv, lora_ref[...], ones_bd)

    attn_done = {}
    for jg in [(j, g) for j in range(nqb) for g in range(N_KV)]:
        yield 'X'
        attn_done[jg] = attn_unit(*jg)

    yield 'M'
    cum = None
    for piece in _split3(logw):
        c = jnp.dot(tril_blk, piece, preferred_element_type=F32)
        cum = c if cum is None else cum + c

    yield 'V'
    cum3 = cum.reshape(nch, CHUNK, D_RWKV)
    cum_last = jnp.broadcast_to(cum3[:, CHUNK - 1:CHUNK, :], cum3.shape).reshape(tb, D_RWKV)
    w_inc = jnp.exp(cum)
    w_exc = jnp.exp(cum - logw)
    w_inv = jnp.exp(-cum)
    rel = jnp.exp(cum_last - cum)
    w_last = jnp.exp(cum_last)
    bvec = kk * a

    npair = RWKV_HEADS // 2
    pw2 = 2 * RWKV_HEAD

    def stack(z):
        parts = [z[:, p * pw2:(p + 1) * pw2].reshape(nch, 1, CHUNK, pw2) for p in range(npair)]
        return jnp.concatenate(parts, axis=1).reshape(nch * npair, CHUNK, pw2)

    low = lax.broadcasted_iota(jnp.int32, (1, 1, pw2), 2) < RWKV_HEAD

    def bdiag(m):
        zero = jnp.zeros_like(m)
        return jnp.concatenate([jnp.where(low, m, zero), jnp.where(low, zero, m)], axis=1)

    def diag_blocks(m):
        lane = lax.broadcasted_iota(jnp.int32, (1, 1, m.shape[2]), 2)
        first = jnp.bitwise_and(lane, pw2 - 1) < RWKV_HEAD
        return jnp.where(first, m[:, 0:RWKV_HEAD, :], m[:, RWKV_HEAD:pw2, :])

    kap = stack(kk * w_exc)
    bt = stack(bvec * w_inv)
    kt = stack(k2 * w_inv)
    rt = stack(r * w_inc)
    bt_l = stack(bvec * rel)
    kt_l = stack(k2 * rel)
    vs = stack(vv)
    wl = stack(w_last)

    yield 'M'
    li = lax.broadcasted_iota(jnp.int32, (1, CHUNK, 2 * pw2), 1)
    lj = jnp.bitwise_and(lax.broadcasted_iota(jnp.int32, (1, CHUNK, 2 * pw2), 2), CHUNK - 1)
    btkt = jnp.concatenate([bdiag(_bf(bt)), bdiag(_bf(kt))], axis=1)
    a_bk = jnp.where(lj < li, _bnt(kap, btkt), 0.0)
    p_bk = jnp.where(lj <= li, _bnt(rt, btkt), 0.0)
    a_b = a_bk[:, :, 0:pw2]
    a_k = a_bk[:, :, pw2:2 * pw2]
    vs_bd = bdiag(_bf(vs))
    xk = kap
    xu = -_bnn(a_k, vs_bd)
    pw = -a_b
    span = 1
    while 2 * span < CHUNK:
        yield 'M'
        pwb = _bf(pw)
        both = _bnn(pwb, jnp.concatenate([bdiag(_bf(xk)), bdiag(_bf(xu)), bdiag(pwb)], axis=2))
        xk = xk + both[:, :, 0:pw2]
        xu = xu + both[:, :, pw2:2 * pw2]
        pw = both[:, :, 2 * pw2:3 * pw2]
        span *= 2
    yield 'M'
    both = _bnn(pw, jnp.concatenate([bdiag(_bf(xk)), bdiag(_bf(xu))], axis=2))
    xk = xk + both[:, :, 0:pw2]
    xu = xu + both[:, :, pw2:2 * pw2]
    yield 'M'
    xk_bd = bdiag(_bf(xk))
    xu_bd = bdiag(_bf(xu))
    rhs = jnp.concatenate([jnp.concatenate([xu_bd, xk_bd], axis=2),
                           jnp.concatenate([vs_bd, jnp.zeros_like(vs_bd)], axis=2)], axis=1)
    o0_pk = _bnn(p_bk, rhs)
    o0 = o0_pk[:, :, 0:pw2]
    rp = rt - o0_pk[:, :, pw2:2 * pw2]
    tn = diag_blocks(_btn(bt_l, jnp.concatenate([xk, xu], axis=2)))
    eye = jnp.where(lax.broadcasted_iota(jnp.int32, (1, CHUNK, pw2), 1)
                    == jnp.bitwise_and(lax.broadcasted_iota(jnp.int32, (1, CHUNK, pw2), 2), CHUNK - 1),
                    1.0, 0.0)
    m_t = eye * wl - tn[:, :, 0:pw2]
    c_t = tn[:, :, pw2:2 * pw2] + diag_blocks(_btn(kt_l, vs))
    rp_mt = jnp.concatenate([rp, m_t], axis=1)

    yield 'V'
    b_blocks = [jnp.concatenate(attn_done[(j, 0)] + attn_done[(j, 1)], axis=-1) for j in range(nqb)]
    b_att = b_blocks[0] if nqb == 1 else jnp.concatenate(b_blocks, axis=0)
    b_out = b_att * _silu(g_attn)

    st_cur = s_ref[...]
    o_chunks = []
    for c in range(nch):
        yield 'M'
        sl = slice(c * npair, (c + 1) * npair)
        both = _bnn(rp_mt[sl], bdiag(_bf(st_cur)))
        o_chunks.append(both[:, 0:CHUNK, :] + o0[sl])
        st_cur = both[:, CHUNK:2 * CHUNK, :] + c_t[sl]
    s_ref[...] = st_cur
    for p in range(npair):
        wkv_ref[i, pl.ds(p * pw2, pw2), :] = st_cur[p].T

    yield 'V'
    o_wkv = jnp.concatenate(
        [jnp.concatenate([oc[p] for p in range(npair)], axis=-1) for oc in o_chunks], axis=0)
    c_out = _rwkv_finish(o_wkv, r, k2, vv, rwv, ones_bd) * _silu(g_rwkv)

    yield 'M'
    mix = jnp.concatenate([_bf(a_out), _bf(b_out), _bf(c_out)], axis=-1)
    y_ref[i] = x + jnp.dot(mix, wout_ref[...], preferred_element_type=F32)


def _layer_spec(shape, layer):
    nd = len(shape)
    return pl.BlockSpec((None,) + tuple(shape), lambda *_: (layer,) + (0,) * nd)


def _prompt_layer(x, cos, sin, p, layer, tb, rows):
    bsz, seq, _ = x.shape
    nt = seq // tb
    grid = (bsz // rows, nt)
    in_specs = [
        pl.BlockSpec(memory_space=pltpu.SMEM),
        pl.BlockSpec((rows, tb, D_MODEL), lambda b, t: (b, t, 0)),
        pl.BlockSpec((tb, 128), lambda b, t: (t, 0)),
        pl.BlockSpec((tb, 128), lambda b, t: (t, 0)),
        _layer_spec((1, D_MODEL), layer),
        _layer_spec((D_MODEL, OFF_GRWKV), layer),
        _layer_spec((D_MODEL, D_MODEL), layer),
        _layer_spec((D_POOL, D_POOL), layer),
        _layer_spec((128, 512), layer),
        _layer_spec((1, D_POOL), layer),
        _layer_spec((1, D_ATTN), layer),
        _layer_spec((1, D_KV), layer),
        _layer_spec((1, D_SHIFT_PAD), layer),
        _layer_spec((8, D_RWKV), layer),
        _layer_spec((D_MODEL, D_RWKV), layer),
    ]
    out_shape = (
        jax.ShapeDtypeStruct((bsz, seq, D_MODEL), F32),
        jax.ShapeDtypeStruct((bsz, POOL_BUF, D_POOL), F32),
        jax.ShapeDtypeStruct((bsz, WINDOW, D_KV), F32),
        jax.ShapeDtypeStruct((bsz, WINDOW, D_KV), F32),
        jax.ShapeDtypeStruct((bsz, 1, D_SHIFT), F32),
        jax.ShapeDtypeStruct((bsz, RWKV_HEADS * RWKV_HEAD, RWKV_HEAD), F32),
    )
    out_specs = (
        pl.BlockSpec((rows, tb, D_MODEL), lambda b, t: (b, t, 0)),
        pl.BlockSpec((rows, POOL_BUF, D_POOL), lambda b, t: (b, 0, 0)),
        pl.BlockSpec((rows, WINDOW, D_KV), lambda b, t: (b, 0, 0)),
        pl.BlockSpec((rows, WINDOW, D_KV), lambda b, t: (b, 0, 0)),
        pl.BlockSpec((rows, 1, D_SHIFT), lambda b, t: (b, 0, 0)),
        pl.BlockSpec((rows, RWKV_HEADS * RWKV_HEAD, RWKV_HEAD), lambda b, t: (b, 0, 0)),
    )
    scratch = [
        pltpu.VMEM((rows, tb + 24, D_POOL), F32),
        pltpu.VMEM((rows, BLOCK, D_KV), BF16),
        pltpu.VMEM((rows, BLOCK, D_KV), BF16),
        pltpu.VMEM((rows, tb + 8, D_SHIFT_PAD), F32),
        pltpu.VMEM((rows, RWKV_HEADS // 2, RWKV_HEAD, 2 * RWKV_HEAD), F32),
    ]
    return pl.pallas_call(
        functools.partial(_prompt_kernel, tb=tb, rows=rows, layer=layer),
        grid=grid,
        in_specs=in_specs,
        out_specs=out_specs,
        out_shape=out_shape,
        scratch_shapes=scratch,
        compiler_params=pltpu.CompilerParams(
            dimension_semantics=("parallel", "arbitrary"),
            vmem_limit_bytes=VMEM_LIMIT_BYTES),
        name="prompt_layer",
    )(p['sinks'], x, cos, sin, p['norm_g'], p['w_in'], p['w_out'], p['pool_wbd'], p['lora_w'],
      p['pool_scale'], p['q_g'], p['k_g'], p['mu'], p['rwv'], p['w_g'])


def _sample_kernel(sinks_ref, x_ref, pool_ref, kc_ref, vc_ref, shp_ref, s_ref, cos_ref, sin_ref,
                   ng_ref, win_ref, wout_ref, pwbd_ref, lora_ref, pscale_ref, qg_ref, kg_ref,
                   mu_ref, rwv_ref, wg_ref,
                   y_ref, poolo_ref, ko_ref, vo_ref, sho_ref, so_ref,
                   ys_scr, z_scr, mix_scr, rkv_scr, vec_scr, ot_scr, *, bt, vs):
    layer = pl.program_id(0)
    step = pl.program_id(1)
    rows = pl.ds(pl.multiple_of(step * bt, bt), bt)
    ones_bd = _block_ones(256, HEAD_DIM)

    @pl.when(step == 0)
    def _project_all_sequences():
        @pl.when(layer == 0)
        def _first_layer_input():
            ys_scr[...] = x_ref[...]
        hb_all = _bf(_rmsnorm_rows(ys_scr[...], ng_ref[...]))
        z_scr[:, 0:OFF_GRWKV] = jnp.dot(hb_all, win_ref[...], preferred_element_type=F32)
        z_scr[:, OFF_GRWKV:D_IN_PAD] = jnp.dot(hb_all, wg_ref[...], preferred_element_type=F32)
        rin = z_scr[:, OFF_RIN:OFF_RIN + D_SHIFT_PAD]
        xs = rin + (shp_ref[...] - rin) * mu_ref[...]
        sho_ref[...] = rin[:, 0:D_SHIFT]
        r, k2, vv, kk, a, logw = _rwkv_token_params(xs, rwv_ref[...], lora_ref[...], ones_bd)
        rkv_scr[0] = r
        rkv_scr[1] = k2
        rkv_scr[2] = vv
        for n, vec in enumerate((kk, jnp.exp(logw), kk * a, k2, r, vv)):
            vec_scr[n] = vec.T

    zp = z_scr[rows, OFF_POOL:OFF_POOL + 512]
    u = zp[:, 0:256]
    g_pool = zp[:, 256:512]
    buf = pool_ref[...]
    ri = lax.broadcasted_iota(jnp.int32, (POOL_BUF, 1, D_POOL), 0)
    grp3 = lax.broadcasted_iota(jnp.int32, (POOL_BUF, 1, D_POOL), 2) // POOL_CG
    wlen3 = jnp.where(grp3 == 0, 2, jnp.where(grp3 == 1, 4, jnp.where(grp3 == 2, 8, 16)))
    tail = jnp.sum(jnp.where(ri >= POOL_BUF + 1 - wlen3, buf, 0.0), axis=0)
    grp = lax.broadcasted_iota(jnp.int32, (bt, D_POOL), 1) // POOL_CG
    wlen = jnp.where(grp == 0, 2, jnp.where(grp == 1, 4, jnp.where(grp == 2, 8, 16)))
    cnt = jnp.minimum(PAST_LEN + 1, wlen).astype(F32)
    pooled = (tail + u) / cnt - u
    a_out = _mm(pooled, pwbd_ref[...]) * pscale_ref[...] * _silu(g_pool)
    poolo_ref[pl.ds(0, POOL_BUF - 1)] = pool_ref[pl.ds(1, POOL_BUF - 1)]
    poolo_ref[POOL_BUF - 1] = u

    zq = z_scr[rows, OFF_QKV:OFF_QKV + 768]
    cos = cos_ref[...]
    sin = sin_ref[...]
    q = _qk_norm_rope(zq[:, 0:512], qg_ref[...], cos, sin, ones_bd) * ATTN_SCALE
    k_new = _qk_norm_rope(zq[:, 512:640], kg_ref[...], cos, sin, ones_bd)
    v_new = zq[:, 640:768]
    si = lax.broadcasted_iota(jnp.int32, (1, D_KV, WINDOW), 2)
    k_win = jnp.where(si == WINDOW - 1, k_new[:, :, None], pltpu.roll(kc_ref[...], WINDOW - 1, 2))
    v_win = jnp.where(si == WINDOW - 1, v_new[:, :, None], pltpu.roll(vc_ref[...], WINDOW - 1, 2))
    ko_ref[...] = k_win
    vo_ref[...] = v_win
    lane_g = lax.broadcasted_iota(jnp.int32, (bt, D_KV), 1) // HEAD_DIM
    q_rows = []
    for h in range(N_HEADS):
        qh = q[:, h * HEAD_DIM:(h + 1) * HEAD_DIM]
        q_rows.append(jnp.where(lane_g == h // Q_PER_KV, jnp.concatenate([qh, qh], axis=-1), 0.0))
    qe = jnp.stack(q_rows, axis=1)
    s = jnp.einsum('bhl,bls->bhs', _bf(qe), _bf(k_win), preferred_element_type=F32)
    hi = lax.broadcasted_iota(jnp.int32, (1, N_HEADS, 1), 1)
    sink = jnp.zeros((1, N_HEADS, 1), F32)
    for h in range(N_HEADS):
        sink = jnp.where(hi == h, sinks_ref[layer, h], sink)
    m = jnp.maximum(jnp.max(s, axis=-1, keepdims=True), sink)
    p = jnp.exp(s - m)
    denom = jnp.sum(p, axis=-1, keepdims=True) + jnp.exp(sink - m)
    o = jnp.einsum('bhs,bls->bhl', _bf(p), _bf(v_win), preferred_element_type=F32) / denom
    b_att = jnp.concatenate(
        [o[:, h, (h // Q_PER_KV) * HEAD_DIM:(h // Q_PER_KV + 1) * HEAD_DIM] for h in range(N_HEADS)],
        axis=-1)
    g_attn = z_scr[rows, OFF_GATTN:OFF_GATTN + 512]
    b_out = b_att * _silu(g_attn)

    mix_scr[rows, 0:D_POOL + D_ATTN] = jnp.concatenate([a_out, b_out], axis=-1)

    head_rows = pl.ds(pl.multiple_of((step * vs) // RWKV_HEAD * RWKV_HEAD, RWKV_HEAD), RWKV_HEAD)
    val_rows = pl.ds(pl.multiple_of(step * vs, vs), vs)
    kk_h = vec_scr[0, head_rows, :][None]
    w_h = vec_scr[1, head_rows, :][None]
    b_h = vec_scr[2, head_rows, :][None]
    k_h = vec_scr[3, head_rows, :][None]
    r_h = vec_scr[4, head_rows, :][None]
    v_s = vec_scr[5, val_rows, :][:, None, :]
    st = s_ref[...]
    sa = jnp.sum(st * kk_h, axis=1, keepdims=True)
    st = st * w_h - sa * b_h + v_s * k_h
    so_ref[...] = st
    ot_scr[val_rows, :] = jnp.sum(st * r_h, axis=1)

    @pl.when(step == pl.num_programs(1) - 1)
    def _output_projection_all_sequences():
        g_rwkv = z_scr[:, OFF_GRWKV:OFF_GRWKV + 256]
        c_out = _rwkv_finish(ot_scr[...].T, rkv_scr[0], rkv_scr[1], rkv_scr[2], rwv_ref[...], ones_bd)
        mix_scr[:, D_POOL + D_ATTN:D_MODEL] = c_out * _silu(g_rwkv)
        y = ys_scr[...] + jnp.dot(_bf(mix_scr[...]), wout_ref[...], preferred_element_type=F32)
        ys_scr[...] = y
        y_ref[...] = y


def _sample_layers(x, pool, kc, vc, shift, wkv, cos, sin, p, bt):
    depth = pool.shape[0]
    nb = x.shape[0]
    steps = nb // bt
    vs = RWKV_HEADS * RWKV_HEAD // steps
    assert PAST_LEN >= WINDOW and nb % bt == 0 and vs * steps == RWKV_HEADS * RWKV_HEAD
    assert vs <= RWKV_HEAD and RWKV_HEAD % vs == 0 and vs % 8 == 0

    def per_layer(shape):
        nd = len(shape)
        return pl.BlockSpec((None,) + tuple(shape), lambda l, i: (l,) + (0,) * nd)

    in_specs = [
        pl.BlockSpec(memory_space=pltpu.SMEM),
        pl.BlockSpec((nb, D_MODEL), lambda l, i: (0, 0)),
        pl.BlockSpec((None, POOL_BUF, bt, D_POOL), lambda l, i: (l, 0, i, 0)),
        pl.BlockSpec((None, bt, D_KV, WINDOW), lambda l, i: (l, i, 0, 0)),
        pl.BlockSpec((None, bt, D_KV, WINDOW), lambda l, i: (l, i, 0, 0)),
        per_layer((nb, D_SHIFT_PAD)),
        pl.BlockSpec((None, vs, RWKV_HEAD, nb), lambda l, i: (l, i, 0, 0)),
        pl.BlockSpec((1, 128), lambda l, i: (0, 0)),
        pl.BlockSpec((1, 128), lambda l, i: (0, 0)),
        per_layer((1, D_MODEL)),
        per_layer((D_MODEL, OFF_GRWKV)),
        per_layer((D_MODEL, D_MODEL)),
        per_layer((D_POOL, D_POOL)),
        per_layer((128, 512)),
        per_layer((1, D_POOL)),
        per_layer((1, D_ATTN)),
        per_layer((1, D_KV)),
        per_layer((1, D_SHIFT_PAD)),
        per_layer((8, D_RWKV)),
        per_layer((D_MODEL, D_RWKV)),
    ]
    out_shape = (
        jax.ShapeDtypeStruct((depth, nb, D_MODEL), F32),
        jax.ShapeDtypeStruct((depth, POOL_BUF, nb, D_POOL), F32),
        jax.ShapeDtypeStruct((depth, nb, D_KV, WINDOW), F32),
        jax.ShapeDtypeStruct((depth, nb, D_KV, WINDOW), F32),
        jax.ShapeDtypeStruct((depth, nb, D_SHIFT), F32),
        jax.ShapeDtypeStruct((depth, RWKV_HEADS * RWKV_HEAD, RWKV_HEAD, nb), F32),
    )
    out_specs = (
        per_layer((nb, D_MODEL)),
        pl.BlockSpec((None, POOL_BUF, bt, D_POOL), lambda l, i: (l, 0, i, 0)),
        pl.BlockSpec((None, bt, D_KV, WINDOW), lambda l, i: (l, i, 0, 0)),
        pl.BlockSpec((None, bt, D_KV, WINDOW), lambda l, i: (l, i, 0, 0)),
        per_layer((nb, D_SHIFT)),
        pl.BlockSpec((None, vs, RWKV_HEAD, nb), lambda l, i: (l, i, 0, 0)),
    )
    scratch = [
        pltpu.VMEM((nb, D_MODEL), F32),
        pltpu.VMEM((nb, D_IN_PAD), F32),
        pltpu.VMEM((nb, D_MODEL), F32),
        pltpu.VMEM((3, nb, D_RWKV), F32),
        pltpu.VMEM((6, D_RWKV, nb), F32),
        pltpu.VMEM((D_RWKV, nb), F32),
    ]
    return pl.pallas_call(
        functools.partial(_sample_kernel, bt=bt, vs=vs),
        grid=(depth, steps),
        in_specs=in_specs,
        out_specs=out_specs,
        out_shape=out_shape,
        scratch_shapes=scratch,
        compiler_params=pltpu.CompilerParams(
            dimension_semantics=("arbitrary", "arbitrary"),
            vmem_limit_bytes=VMEM_LIMIT_BYTES),
        name="sample_layers",
    )(p['sinks'], x, pool, kc, vc, shift, wkv, cos, sin, p['norm_g'], p['w_in'], p['w_out'],
      p['pool_wbd'], p['lora_w'], p['pool_scale'], p['q_g'], p['k_g'], p['mu'], p['rwv'], p['w_g'])


def _rope_tables(pos):
    half = HEAD_DIM // 2
    freqs = ROPE_THETA ** (-jnp.arange(half, dtype=F32) / half)
    ang = pos.astype(F32)[:, None] * freqs[None, :]
    c = jnp.cos(ang)
    s = jnp.sin(ang)
    cos = jnp.concatenate([c, c, c, c], axis=-1)
    sin = jnp.concatenate([-s, s, -s, s], axis=-1)
    return cos, sin


def _prep_params(norm_g, w_in, w_out, pool_w, pool_scale, q_norm_g, k_norm_g, attn_sinks,
                 rwkv_mu, rwkv_w0, rwkv_w_up, rwkv_a0, rwkv_a_up, rwkv_k_k, rwkv_k_a, rwkv_r_k,
                 rwkv_ln_g, rwkv_ln_b):
    depth = w_in.shape[0]
    d_rin_end = OFF_RIN + D_SHIFT
    w_in_p = jnp.pad(w_in[:, :, :d_rin_end].astype(BF16),
                     ((0, 0), (0, 0), (0, D_SHIFT_PAD - D_SHIFT)))
    w_gate = w_in[:, :, d_rin_end:].astype(BF16)
    groups = len(POOL_WINDOWS)
    bd = jnp.einsum('lgcd,gh->lgchd', pool_w.astype(F32), jnp.eye(groups, dtype=F32))
    bd = bd.reshape(depth, D_POOL, D_POOL)
    zl = jnp.zeros((depth, LORA, D_RWKV), F32)
    lora = jnp.concatenate(
        [jnp.concatenate([rwkv_w_up, zl], axis=2), jnp.concatenate([zl, rwkv_a_up], axis=2),
         jnp.zeros((depth, 128 - 2 * LORA, 2 * D_RWKV), F32)], axis=1)
    rwv = jnp.stack([rwkv_w0, rwkv_a0, rwkv_k_k, rwkv_k_a, rwkv_r_k, rwkv_ln_g, rwkv_ln_b,
                     jnp.zeros((depth, D_RWKV), F32)], axis=1)
    return {
        'sinks': attn_sinks.astype(F32),
        'norm_g': norm_g[:, None, :],
        'w_in': w_in_p,
        'w_g': w_gate,
        'w_out': w_out.astype(BF16),
        'pool_wbd': bd.astype(BF16),
        'lora_w': lora.astype(BF16),
        'pool_scale': pool_scale[:, None, :],
        'q_g': jnp.tile(q_norm_g, (1, N_HEADS))[:, None, :],
        'k_g': jnp.tile(k_norm_g, (1, N_KV))[:, None, :],
        'mu': jnp.pad(rwkv_mu, ((0, 0), (0, D_SHIFT_PAD - D_SHIFT)))[:, None, :],
        'rwv': rwv,
    }


def kernel(x_prompt, x_sample, state_pool, cache_swa_k, cache_swa_v, state_rwkv_shift, state_rwkv_wkv, norm_g, w_in, w_out, pool_w, pool_scale, q_norm_g, k_norm_g, attn_sinks, rwkv_mu, rwkv_w0, rwkv_w_up, rwkv_a0, rwkv_a_up, rwkv_k_k, rwkv_k_a, rwkv_r_k, rwkv_ln_g, rwkv_ln_b):
    depth = w_in.shape[0]
    bsz, seq, _ = x_prompt.shape
    tb = 256 if seq % 256 == 0 else BLOCK
    rows = 2 if bsz % 2 == 0 else 1
    nb = x_sample.shape[0]
    bt = 32 if nb % 32 == 0 else nb
    p = _prep_params(norm_g, w_in, w_out, pool_w, pool_scale, q_norm_g, k_norm_g, attn_sinks,
                     rwkv_mu, rwkv_w0, rwkv_w_up, rwkv_a0, rwkv_a_up, rwkv_k_k, rwkv_k_a,
                     rwkv_r_k, rwkv_ln_g, rwkv_ln_b)

    cos_p, sin_p = _rope_tables(jnp.arange(seq))
    yp = x_prompt
    outs_p = [[] for _ in range(5)]
    for layer in range(depth):
        yp, pool_n, k_n, v_n, sh_n, wkv_n = _prompt_layer(yp, cos_p, sin_p, p, layer, tb, rows)
        outs_p[0].append(pool_n)
        outs_p[1].append(k_n.reshape(bsz, WINDOW, N_KV, HEAD_DIM))
        outs_p[2].append(v_n.reshape(bsz, WINDOW, N_KV, HEAD_DIM))
        outs_p[3].append(sh_n.reshape(bsz, D_SHIFT))
        outs_p[4].append(wkv_n.reshape(bsz, RWKV_HEADS, RWKV_HEAD, RWKV_HEAD))
    pool_p, k_p, v_p, sh_p, wkv_p = [jnp.stack(o) for o in outs_p]

    cos_s, sin_s = _rope_tables(jnp.full((1,), PAST_LEN))
    to_kernel_cache = lambda c: jnp.transpose(c, (0, 1, 3, 4, 2)).reshape(depth, nb, D_KV, WINDOW)
    from_kernel_cache = lambda c: jnp.transpose(
        c.reshape(depth, nb, N_KV, HEAD_DIM, WINDOW), (0, 1, 4, 2, 3))
    ys, pool_s, k_s, v_s, sh_s, wkv_s = _sample_layers(
        x_sample.reshape(nb, D_MODEL),
        jnp.transpose(state_pool, (0, 2, 1, 3)),
        to_kernel_cache(cache_swa_k), to_kernel_cache(cache_swa_v),
        jnp.pad(state_rwkv_shift, ((0, 0), (0, 0), (0, D_SHIFT_PAD - D_SHIFT))),
        jnp.transpose(state_rwkv_wkv, (0, 2, 3, 4, 1)).reshape(
            depth, RWKV_HEADS * RWKV_HEAD, RWKV_HEAD, nb),
        cos_s, sin_s, p, bt)
    return (yp, ys[depth - 1].reshape(nb, 1, D_MODEL),
            pool_p, jnp.transpose(pool_s, (0, 2, 1, 3)),
            k_p, from_kernel_cache(k_s),
            v_p, from_kernel_cache(v_s),
            sh_p, sh_s,
            wkv_p, jnp.transpose(
                wkv_s.reshape(depth, RWKV_HEADS, RWKV_HEAD, RWKV_HEAD, nb), (0, 4, 1, 2, 3)))
```

```python
import functools

import numpy as np
import jax
import jax.numpy as jnp
from jax import lax
from jax.experimental import pallas as pl
from jax.experimental.pallas import tpu as pltpu

D_MODEL = 1024
D_POOL = 256
POOL_WINDOWS = (2, 4, 8, 16)
POOL_CG = 64
POOL_BUF = 15
HEAD_DIM = 64
D_ATTN = 512
N_HEADS = 8
N_KV = 2
Q_PER_KV = 4
D_KV = 128
WINDOW = 128
BLOCK = 128
ROPE_THETA = 10000.0
QK_EPS = 1e-6
D_RWKV = 256
RWKV_HEAD = 64
RWKV_HEADS = 4
LORA = 32
D_SHIFT = 832
D_SHIFT_PAD = 896
GN_EPS = 64e-5
NORM_EPS = 1e-6
PAST_LEN = 16384
ATTN_SCALE = HEAD_DIM ** -0.5
LOG2E = 1.4426950408889634

OFF_POOL = 0
OFF_QKV = 512
OFF_GATTN = 1280
OFF_RIN = 1792
OFF_GRWKV = 2688
D_IN_PAD = 2944

CHUNK = 64
NEG = -1e30
F32 = jnp.float32
BF16 = jnp.bfloat16

VMEM_LIMIT_BYTES = 56 * 1024 * 1024


def _bf(x):
    return x.astype(BF16)


def _mm(x, y):
    return jnp.dot(_bf(x), _bf(y), preferred_element_type=F32)


def _bnt(x, y):
    return jnp.einsum('bik,bjk->bij', _bf(x), _bf(y), preferred_element_type=F32)


def _bnn(x, y):
    return jnp.einsum('bik,bkj->bij', _bf(x), _bf(y), preferred_element_type=F32)


def _btn(x, y):
    return jnp.einsum('bli,blj->bij', _bf(x), _bf(y), preferred_element_type=F32)


def _sigmoid(x):
    return 0.5 + 0.5 * jnp.tanh(0.5 * x)


def _silu(x):
    half = 0.5 * x
    return half + half * jnp.tanh(half)


def _softplus(x):
    return jnp.maximum(x, 0.0) + jnp.log(1.0 + jnp.exp(-jnp.abs(x)))


def _split3(x):
    h1 = _bf(x)
    r1 = x - h1.astype(F32)
    h2 = _bf(r1)
    h3 = _bf(r1 - h2.astype(F32))
    return h1, h2, h3


def _block_ones(n, seg):
    r = lax.broadcasted_iota(jnp.int32, (n, n), 0) // seg
    c = lax.broadcasted_iota(jnp.int32, (n, n), 1) // seg
    return jnp.where(r == c, 1.0, 0.0).astype(BF16)


def _seg_sum(x, ones_bd):
    w = x.shape[-1]
    step = min(w, ones_bd.shape[0])
    outs = [jnp.dot(_bf(x[:, c0:c0 + step]), ones_bd[:step, :step], preferred_element_type=F32)
            for c0 in range(0, w, step)]
    return outs[0] if len(outs) == 1 else jnp.concatenate(outs, axis=-1)


def _rmsnorm_rows(x, g):
    ms = jnp.mean(x * x, axis=-1, keepdims=True)
    return x * lax.rsqrt(ms + NORM_EPS) * g


def _swap_halves(x):
    w = x.shape[-1]
    lane = lax.broadcasted_iota(jnp.int32, x.shape, x.ndim - 1)
    fwd = pltpu.roll(x, w - HEAD_DIM // 2, x.ndim - 1)
    bwd = pltpu.roll(x, HEAD_DIM // 2, x.ndim - 1)
    return jnp.where(lane % HEAD_DIM < HEAD_DIM // 2, fwd, bwd)


def _qk_norm_rope(x, g, cos, sin_signed, ones_bd):
    ms = _seg_sum(x * x, ones_bd) * (1.0 / HEAD_DIM)
    xn = x * lax.rsqrt(ms + QK_EPS) * g
    reps = x.shape[-1] // cos.shape[-1]
    if reps > 1:
        cos = jnp.concatenate([cos] * reps, axis=-1)
        sin_signed = jnp.concatenate([sin_signed] * reps, axis=-1)
    return xn * cos + _swap_halves(xn) * sin_signed


def _rwkv_token_params(xs, rwv, lora_w, ones_bd):
    r = xs[:, 0:256]
    k = xs[:, 256:512]
    v = xs[:, 512:768]
    lo = xs[:, 768:896]
    lane = lax.broadcasted_iota(jnp.int32, lo.shape, 1)
    lo_in = jnp.where(lane < LORA, jnp.tanh(lo), lo)
    pre = _mm(lo_in, lora_w)
    w_pre = rwv[0:1, :] + pre[:, 0:256]
    a_pre = rwv[1:2, :] + pre[:, 256:512]
    w_log = -_softplus(-w_pre) - 0.5
    logw = -jnp.exp(w_log)
    a = _sigmoid(a_pre)
    kk = k * rwv[2:3, :]
    nrm = jnp.sqrt(_seg_sum(kk * kk, ones_bd))
    kk = kk / jnp.maximum(nrm, 1e-12)
    k2 = k * (1.0 + (a - 1.0) * rwv[3:4, :])
    return r, k2, v, kk, a, logw


def _rwkv_finish(o, r, k2, v, rwv, ones_bd):
    inv_n = 1.0 / RWKV_HEAD
    mu_o = _seg_sum(o, ones_bd) * inv_n
    d = o - mu_o
    var = _seg_sum(d * d, ones_bd) * inv_n
    on = d * lax.rsqrt(var + GN_EPS) * rwv[5:6, :] + rwv[6:7, :]
    bonus = _seg_sum(r * k2 * rwv[4:5, :], ones_bd) * v
    return on + bonus


def _run_interleaved(programs):
    state = [[prog, next(prog), 0] for prog in programs]
    last = 'M'
    while state:
        want = 'V' if last == 'M' else 'M'
        order = sorted(state, key=lambda s: s[2])
        pick = next((s for s in order if s[1] == want), None)
        if pick is None:
            pick = next((s for s in order if s[1] == 'X'), order[0])
        if pick[1] != 'X':
            last = pick[1]
        pick[2] += 1
        try:
            pick[1] = next(pick[0])
        except StopIteration:
            state.remove(pick)


def _prompt_kernel(sinks_ref, x_ref, cos_ref, sin_ref, ng_ref, win_ref, wout_ref, pwbd_ref,
                   lora_ref, pscale_ref, qg_ref, kg_ref, mu_ref, rwv_ref, wg_ref,
                   y_ref, pool_ref, ko_ref, vo_ref, sh_ref, wkv_ref,
                   pext_all, kprev_all, vprev_all, rext_all, s_all, *, tb, rows, layer):
    t = pl.program_id(1)

    @pl.when(t == 0)
    def _init():
        pext_all[...] = jnp.zeros(pext_all.shape, F32)
        kprev_all[...] = jnp.zeros(kprev_all.shape, BF16)
        vprev_all[...] = jnp.zeros(vprev_all.shape, BF16)
        rext_all[:, pl.ds(0, 8), :] = jnp.zeros((rows, 8, D_SHIFT_PAD), F32)
        s_all[...] = jnp.zeros(s_all.shape, F32)

    ones_bd = _block_ones(256, HEAD_DIM)
    nrow = Q_PER_KV * BLOCK
    qi = lax.broadcasted_iota(jnp.int32, (nrow, 2 * BLOCK), 0) % BLOCK
    kj = lax.broadcasted_iota(jnp.int32, (nrow, 2 * BLOCK), 1)
    band = jnp.where(kj > qi, jnp.where(kj <= qi + BLOCK, 0.0, NEG), NEG)
    first_lo = jnp.where(t == 0, BLOCK, 0)
    band_first = jnp.where(kj >= first_lo, band, NEG)
    hrow = lax.broadcasted_iota(jnp.int32, (nrow, 1), 0) // BLOCK
    ti = lax.broadcasted_iota(jnp.int32, (tb, tb), 0)
    tj = lax.broadcasted_iota(jnp.int32, (tb, tb), 1)
    tril_blk = jnp.where(ti // CHUNK == tj // CHUNK, jnp.where(tj <= ti, 1.0, 0.0), 0.0).astype(BF16)
    shared = (ones_bd, band, band_first, hrow, tril_blk)
    programs = [
        _prompt_tile(i, t, layer, shared, sinks_ref, x_ref, cos_ref, sin_ref, ng_ref, win_ref, wout_ref,
                     pwbd_ref, lora_ref, pscale_ref, qg_ref, kg_ref, mu_ref, rwv_ref, wg_ref,
                     y_ref, pool_ref, ko_ref, vo_ref, sh_ref, wkv_ref,
                     pext_all.at[i], kprev_all.at[i], vprev_all.at[i], rext_all.at[i], s_all.at[i],
                     tb=tb)
        for i in range(rows)]
    _run_interleaved(programs)


def _prompt_tile(i, t, layer, shared, sinks_ref, x_ref, cos_ref, sin_ref, ng_ref, win_ref, wout_ref,
                 pwbd_ref, lora_ref, pscale_ref, qg_ref, kg_ref, mu_ref, rwv_ref, wg_ref,
                 y_ref, pool_ref, ko_ref, vo_ref, sh_ref, wkv_ref,
                 pext, kprev, vprev, rext, s_ref, *, tb):
    nqb = tb // BLOCK
    nch = tb // CHUNK
    ones_bd, band, band_first, hrow, tril_blk = shared

    def proj(off, width):
        return jnp.dot(hb, win_ref[:, off:off + width], preferred_element_type=F32)

    yield 'V'
    x = x_ref[i]
    hb = _bf(_rmsnorm_rows(x, ng_ref[...]))

    yield 'M'
    zp = proj(OFF_POOL, 512)
    yield 'M'
    zq = proj(OFF_QKV, 768)

    yield 'V'
    u = zp[:, 0:256]
    g_pool = zp[:, 256:512]
    pext[pl.ds(24, tb), :] = u
    n_ext = tb + 16
    wins = []
    for sh in (1, 2, 4, 8):
        cur = pext[pl.ds(8, n_ext), :] + pext[pl.ds(8 - sh, n_ext), :]
        pext[pl.ds(8, n_ext), :] = cur
        wins.append(cur[16:16 + tb, :])
    pext[pl.ds(8, 16), :] = u[tb - 16:tb, :]
    lane = lax.broadcasted_iota(jnp.int32, (tb, D_POOL), 1)
    row = lax.broadcasted_iota(jnp.int32, (tb, D_POOL), 0)
    grp = lane // POOL_CG
    win_sum = jnp.where(grp == 0, wins[0], jnp.where(grp == 1, wins[1],
                        jnp.where(grp == 2, wins[2], wins[3])))
    wlen = jnp.where(grp == 0, 2, jnp.where(grp == 1, 4, jnp.where(grp == 2, 8, 16)))
    cnt = jnp.minimum(t * tb + row + 1, wlen).astype(F32)
    pooled = win_sum / cnt - u
    a_out = _mm(pooled, pwbd_ref[...]) * pscale_ref[...] * _silu(g_pool)
    pool_ref[i] = u[tb - POOL_BUF:tb, :]

    yield 'M'
    rin = proj(OFF_RIN, D_SHIFT_PAD)

    yield 'V'
    cos = cos_ref[...]
    sin = sin_ref[...]
    q = _qk_norm_rope(zq[:, 0:512], qg_ref[...], cos, sin, ones_bd) * (ATTN_SCALE * LOG2E)
    qb = _bf(q)

    yield 'M'
    g_attn = proj(OFF_GATTN, 512)
    g_rwkv = jnp.dot(hb, wg_ref[...], preferred_element_type=F32)

    yield 'V'
    k = _qk_norm_rope(zq[:, 512:640], kg_ref[...], cos, sin, ones_bd)
    v = zq[:, 640:768]
    kb = _bf(k)
    vb = _bf(v)
    ko_ref[i] = k[tb - WINDOW:tb, :]
    vo_ref[i] = v[tb - WINDOW:tb, :]
    kprev_val = kprev[...]
    vprev_val = vprev[...]
    kprev[...] = kb[tb - BLOCK:tb, :]
    vprev[...] = vb[tb - BLOCK:tb, :]

    def attn_unit(j, g):
        rs = slice(j * BLOCK, (j + 1) * BLOCK)
        if j == 0:
            kx = jnp.concatenate([kprev_val, kb[rs]], axis=0)
            vx = jnp.concatenate([vprev_val, vb[rs]], axis=0)
            valid = band_first
        else:
            kx = kb[(j - 1) * BLOCK:(j + 1) * BLOCK]
            vx = vb[(j - 1) * BLOCK:(j + 1) * BLOCK]
            valid = band
        kxg = kx[:, g * HEAD_DIM:(g + 1) * HEAD_DIM]
        vxg = vx[:, g * HEAD_DIM:(g + 1) * HEAD_DIM]
        qs = jnp.concatenate(
            [qb[rs, (g * Q_PER_KV + hq) * HEAD_DIM:(g * Q_PER_KV + hq + 1) * HEAD_DIM]
             for hq in range(Q_PER_KV)], axis=0)
        s = lax.dot_general(qs, kxg, (((1,), (1,)), ((), ())),
                            preferred_element_type=F32)
        s = s + valid
        sink = jnp.where(hrow == 0, sinks_ref[layer, g * Q_PER_KV],
                         jnp.where(hrow == 1, sinks_ref[layer, g * Q_PER_KV + 1],
                                   jnp.where(hrow == 2, sinks_ref[layer, g * Q_PER_KV + 2],
                                             sinks_ref[layer, g * Q_PER_KV + 3]))) * LOG2E
        m = jnp.maximum(jnp.max(s, axis=-1, keepdims=True), sink)
        p = jnp.exp2(s - m)
        denom = jnp.sum(p, axis=-1, keepdims=True) + jnp.exp2(sink - m)
        o = jnp.dot(_bf(p), vxg, preferred_element_type=F32) / denom
        return [o[hq * BLOCK:(hq + 1) * BLOCK, :] for hq in range(Q_PER_KV)]

    yield 'V'
    rext[pl.ds(8, tb), :] = rin
    prev = rext[pl.ds(7, tb), :]
    rext[pl.ds(7, 1), :] = rin[tb - 1:tb, :]
    xs = rin + (prev - rin) * mu_ref[...]
    sh_ref[i] = rin[tb - 1:tb, 0:D_SHIFT]
    rwv = rwv_ref[...]
    r, k2, vv, kk, a, logw = _rwkv_token_params(xs, rwv, lora_ref[...], ones_bd)

    attn_done = {}
    for jg in [(j, g) for j in range(nqb) for g in range(N_KV)]:
        yield 'X'
        attn_done[jg] = attn_unit(*jg)

    yield 'M'
    cum = None
    for piece in _split3(logw):
        c = jnp.dot(tril_blk, piece, preferred_element_type=F32)
        cum = c if cum is None else cum + c

    yield 'V'
    cum3 = cum.reshape(nch, CHUNK, D_RWKV)
    cum_last = jnp.broadcast_to(cum3[:, CHUNK - 1:CHUNK, :], cum3.shape).reshape(tb, D_RWKV)
    w_inc = jnp.exp(cum)
    w_exc = jnp.exp(cum - logw)
    w_inv = jnp.exp(-cum)
    rel = jnp.exp(cum_last - cum)
    w_last = jnp.exp(cum_last)
    bvec = kk * a

    npair = RWKV_HEADS // 2
    pw2 = 2 * RWKV_HEAD

    def stack(z):
        parts = [z[:, p * pw2:(p + 1) * pw2].reshape(nch, 1, CHUNK, pw2) for p in range(npair)]
        return jnp.concatenate(parts, axis=1).reshape(nch * npair, CHUNK, pw2)

    low = lax.broadcasted_iota(jnp.int32, (1, 1, pw2), 2) < RWKV_HEAD

    def bdiag(m):
        zero = jnp.zeros_like(m)
        return jnp.concatenate([jnp.where(low, m, zero), jnp.where(low, zero, m)], axis=1)

    def diag_blocks(m):
        lane = lax.broadcasted_iota(jnp.int32, (1, 1, m.shape[2]), 2)
        first = jnp.bitwise_and(lane, pw2 - 1) < RWKV_HEAD
        return jnp.where(first, m[:, 0:RWKV_HEAD, :], m[:, RWKV_HEAD:pw2, :])

    kap = stack(kk * w_exc)
    bt = stack(bvec * w_inv)
    kt = stack(k2 * w_inv)
    rt = stack(r * w_inc)
    bt_l = stack(bvec * rel)
    kt_l = stack(k2 * rel)
    vs = stack(vv)
    wl = stack(w_last)

    yield 'M'
    li = lax.broadcasted_iota(jnp.int32, (1, CHUNK, 2 * pw2), 1)
    lj = jnp.bitwise_and(lax.broadcasted_iota(jnp.int32, (1, CHUNK, 2 * pw2), 2), CHUNK - 1)
    btkt = jnp.concatenate([bdiag(_bf(bt)), bdiag(_bf(kt))], axis=1)
    a_bk = jnp.where(lj < li, _bnt(kap, btkt), 0.0)
    p_bk = jnp.where(lj <= li, _bnt(rt, btkt), 0.0)
    a_b = a_bk[:, :, 0:pw2]
    a_k = a_bk[:, :, pw2:2 * pw2]
    vs_bd = bdiag(_bf(vs))
    xk = kap
    xu = -_bnn(a_k, vs_bd)
    pw = -a_b
    span = 1
    while 2 * span < CHUNK:
        yield 'M'
        pwb = _bf(pw)
        both = _bnn(pwb, jnp.concatenate([bdiag(_bf(xk)), bdiag(_bf(xu)), bdiag(pwb)], axis=2))
        xk = xk + both[:, :, 0:pw2]
        xu = xu + both[:, :, pw2:2 * pw2]
        pw = both[:, :, 2 * pw2:3 * pw2]
        span *= 2
    yield 'M'
    both = _bnn(pw, jnp.concatenate([bdiag(_bf(xk)), bdiag(_bf(xu))], axis=2))
    xk = xk + both[:, :, 0:pw2]
    xu = xu + both[:, :, pw2:2 * pw2]
    yield 'M'
    xk_bd = bdiag(_bf(xk))
    xu_bd = bdiag(_bf(xu))
    rhs = jnp.concatenate([jnp.concatenate([xu_bd, xk_bd], axis=2),
                           jnp.concatenate([vs_bd, jnp.zeros_like(vs_bd)], axis=2)], axis=1)
    o0_pk = _bnn(p_bk, rhs)
    o0 = o0_pk[:, :, 0:pw2]
    rp = rt - o0_pk[:, :, pw2:2 * pw2]
    tn = diag_blocks(_btn(bt_l, jnp.concatenate([xk, xu], axis=2)))
    eye = jnp.where(lax.broadcasted_iota(jnp.int32, (1, CHUNK, pw2), 1)
                    == jnp.bitwise_and(lax.broadcasted_iota(jnp.int32, (1, CHUNK, pw2), 2), CHUNK - 1),
                    1.0, 0.0)
    m_t = eye * wl - tn[:, :, 0:pw2]
    c_t = tn[:, :, pw2:2 * pw2] + diag_blocks(_btn(kt_l, vs))
    rp_mt = jnp.concatenate([rp, m_t], axis=1)

    yield 'V'
    b_blocks = [jnp.concatenate(attn_done[(j, 0)] + attn_done[(j, 1)], axis=-1) for j in range(nqb)]
    b_att = b_blocks[0] if nqb == 1 else jnp.concatenate(b_blocks, axis=0)
    b_out = b_att * _silu(g_attn)

    st_cur = s_ref[...]
    o_chunks = []
    for c in range(nch):
        yield 'M'
        sl = slice(c * npair, (c + 1) * npair)
        both = _bnn(rp_mt[sl], bdiag(_bf(st_cur)))
        o_chunks.append(both[:, 0:CHUNK, :] + o0[sl])
        st_cur = both[:, CHUNK:2 * CHUNK, :] + c_t[sl]
    s_ref[...] = st_cur
    for p in range(npair):
        wkv_ref[i, pl.ds(p * pw2, pw2), :] = st_cur[p].T

    yield 'V'
    o_wkv = jnp.concatenate(
        [jnp.concatenate([oc[p] for p in range(npair)], axis=-1) for oc in o_chunks], axis=0)
    c_out = _rwkv_finish(o_wkv, r, k2, vv, rwv, ones_bd) * _silu(g_rwkv)

    yield 'M'
    mix = jnp.concatenate([_bf(a_out), _bf(b_out), _bf(c_out)], axis=-1)
    y_ref[i] = x + jnp.dot(mix, wout_ref[...], preferred_element_type=F32)


def _layer_spec(shape, layer):
    nd = len(shape)
    return pl.BlockSpec((None,) + tuple(shape), lambda *_: (layer,) + (0,) * nd)


def _prompt_layer(x, cos, sin, p, layer, tb, rows):
    bsz, seq, _ = x.shape
    nt = seq // tb
    grid = (bsz // rows, nt)
    in_specs = [
        pl.BlockSpec(memory_space=pltpu.SMEM),
        pl.BlockSpec((rows, tb, D_MODEL), lambda b, t: (b, t, 0)),
        pl.BlockSpec((tb, 128), lambda b, t: (t, 0)),
        pl.BlockSpec((tb, 128), lambda b, t: (t, 0)),
        _layer_spec((1, D_MODEL), layer),
        _layer_spec((D_MODEL, OFF_GRWKV), layer),
        _layer_spec((D_MODEL, D_MODEL), layer),
        _layer_spec((D_POOL, D_POOL), layer),
        _layer_spec((128, 512), layer),
        _layer_spec((1, D_POOL), layer),
        _layer_spec((1, D_ATTN), layer),
        _layer_spec((1, D_KV), layer),
        _layer_spec((1, D_SHIFT_PAD), layer),
        _layer_spec((8, D_RWKV), layer),
        _layer_spec((D_MODEL, D_RWKV), layer),
    ]
    out_shape = (
        jax.ShapeDtypeStruct((bsz, seq, D_MODEL), F32),
        jax.ShapeDtypeStruct((bsz, POOL_BUF, D_POOL), F32),
        jax.ShapeDtypeStruct((bsz, WINDOW, D_KV), F32),
        jax.ShapeDtypeStruct((bsz, WINDOW, D_KV), F32),
        jax.ShapeDtypeStruct((bsz, 1, D_SHIFT), F32),
        jax.ShapeDtypeStruct((bsz, RWKV_HEADS * RWKV_HEAD, RWKV_HEAD), F32),
    )
    out_specs = (
        pl.BlockSpec((rows, tb, D_MODEL), lambda b, t: (b, t, 0)),
        pl.BlockSpec((rows, POOL_BUF, D_POOL), lambda b, t: (b, 0, 0)),
        pl.BlockSpec((rows, WINDOW, D_KV), lambda b, t: (b, 0, 0)),
        pl.BlockSpec((rows, WINDOW, D_KV), lambda b, t: (b, 0, 0)),
        pl.BlockSpec((rows, 1, D_SHIFT), lambda b, t: (b, 0, 0)),
        pl.BlockSpec((rows, RWKV_HEADS * RWKV_HEAD, RWKV_HEAD), lambda b, t: (b, 0, 0)),
    )
    scratch = [
        pltpu.VMEM((rows, tb + 24, D_POOL), F32),
        pltpu.VMEM((rows, BLOCK, D_KV), BF16),
        pltpu.VMEM((rows, BLOCK, D_KV), BF16),
        pltpu.VMEM((rows, tb + 8, D_SHIFT_PAD), F32),
        pltpu.VMEM((rows, RWKV_HEADS // 2, RWKV_HEAD, 2 * RWKV_HEAD), F32),
    ]
    return pl.pallas_call(
        functools.partial(_prompt_kernel, tb=tb, rows=rows, layer=layer),
        grid=grid,
        in_specs=in_specs,
        out_specs=out_specs,
        out_shape=out_shape,
        scratch_shapes=scratch,
        compiler_params=pltpu.CompilerParams(
            dimension_semantics=("parallel", "arbitrary"),
            vmem_limit_bytes=VMEM_LIMIT_BYTES),
        name="prompt_layer",
    )(p['sinks'], x, cos, sin, p['norm_g'], p['w_in'], p['w_out'], p['pool_wbd'], p['lora_w'],
      p['pool_scale'], p['q_g'], p['k_g'], p['mu'], p['rwv'], p['w_g'])


def _sample_kernel(sinks_ref, x_ref, pool_ref, kc_ref, vc_ref, shp_ref, s_ref, cos_ref, sin_ref,
                   ng_ref, win_ref, wout_ref, pwbd_ref, lora_ref, pscale_ref, qg_ref, kg_ref,
                   mu_ref, rwv_ref, wg_ref,
                   y_ref, poolo_ref, ko_ref, vo_ref, sho_ref, so_ref,
                   ys_scr, z_scr, mix_scr, rkv_scr, vec_scr, ot_scr, *, bt, vs):
    layer = pl.program_id(0)
    step = pl.program_id(1)
    rows = pl.ds(pl.multiple_of(step * bt, bt), bt)
    ones_bd = _block_ones(256, HEAD_DIM)

    @pl.when(step == 0)
    def _project_all_sequences():
        @pl.when(layer == 0)
        def _first_layer_input():
            ys_scr[...] = x_ref[...]
        hb_all = _bf(_rmsnorm_rows(ys_scr[...], ng_ref[...]))
        z_scr[:, 0:OFF_GRWKV] = jnp.dot(hb_all, win_ref[...], preferred_element_type=F32)
        z_scr[:, OFF_GRWKV:D_IN_PAD] = jnp.dot(hb_all, wg_ref[...], preferred_element_type=F32)
        rin = z_scr[:, OFF_RIN:OFF_RIN + D_SHIFT_PAD]
        xs = rin + (shp_ref[...] - rin) * mu_ref[...]
        sho_ref[...] = rin[:, 0:D_SHIFT]
        r, k2, vv, kk, a, logw = _rwkv_token_params(xs, rwv_ref[...], lora_ref[...], ones_bd)
        rkv_scr[0] = r
        rkv_scr[1] = k2
        rkv_scr[2] = vv
        for n, vec in enumerate((kk, jnp.exp(logw), kk * a, k2, r, vv)):
            vec_scr[n] = vec.T

    zp = z_scr[rows, OFF_POOL:OFF_POOL + 512]
    u = zp[:, 0:256]
    g_pool = zp[:, 256:512]
    buf = pool_ref[...]
    ri = lax.broadcasted_iota(jnp.int32, (POOL_BUF, 1, D_POOL), 0)
    grp3 = lax.broadcasted_iota(jnp.int32, (POOL_BUF, 1, D_POOL), 2) // POOL_CG
    wlen3 = jnp.where(grp3 == 0, 2, jnp.where(grp3 == 1, 4, jnp.where(grp3 == 2, 8, 16)))
    tail = jnp.sum(jnp.where(ri >= POOL_BUF + 1 - wlen3, buf, 0.0), axis=0)
    grp = lax.broadcasted_iota(jnp.int32, (bt, D_POOL), 1) // POOL_CG
    wlen = jnp.where(grp == 0, 2, jnp.where(grp == 1, 4, jnp.where(grp == 2, 8, 16)))
    cnt = jnp.minimum(PAST_LEN + 1, wlen).astype(F32)
    pooled = (tail + u) / cnt - u
    a_out = _mm(pooled, pwbd_ref[...]) * pscale_ref[...] * _silu(g_pool)
    poolo_ref[pl.ds(0, POOL_BUF - 1)] = pool_ref[pl.ds(1, POOL_BUF - 1)]
    poolo_ref[POOL_BUF - 1] = u

    zq = z_scr[rows, OFF_QKV:OFF_QKV + 768]
    cos = cos_ref[...]
    sin = sin_ref[...]
    q = _qk_norm_rope(zq[:, 0:512], qg_ref[...], cos, sin, ones_bd) * ATTN_SCALE
    k_new = _qk_norm_rope(zq[:, 512:640], kg_ref[...], cos, sin, ones_bd)
    v_new = zq[:, 640:768]
    si = lax.broadcasted_iota(jnp.int32, (1, D_KV, WINDOW), 2)
    k_win = jnp.where(si == WINDOW - 1, k_new[:, :, None], pltpu.roll(kc_ref[...], WINDOW - 1, 2))
    v_win = jnp.where(si == WINDOW - 1, v_new[:, :, None], pltpu.roll(vc_ref[...], WINDOW - 1, 2))
    ko_ref[...] = k_win
    vo_ref[...] = v_win
    lane_g = lax.broadcasted_iota(jnp.int32, (bt, D_KV), 1) // HEAD_DIM
    q_rows = []
    for h in range(N_HEADS):
        qh = q[:, h * HEAD_DIM:(h + 1) * HEAD_DIM]
        q_rows.append(jnp.where(lane_g == h // Q_PER_KV, jnp.concatenate([qh, qh], axis=-1), 0.0))
    qe = jnp.stack(q_rows, axis=1)
    s = jnp.einsum('bhl,bls->bhs', _bf(qe), _bf(k_win), preferred_element_type=F32)
    hi = lax.broadcasted_iota(jnp.int32, (1, N_HEADS, 1), 1)
    sink = jnp.zeros((1, N_HEADS, 1), F32)
    for h in range(N_HEADS):
        sink = jnp.where(hi == h, sinks_ref[layer, h], sink)
    m = jnp.maximum(jnp.max(s, axis=-1, keepdims=True), sink)
    p = jnp.exp(s - m)
    denom = jnp.sum(p, axis=-1, keepdims=True) + jnp.exp(sink - m)
    o = jnp.einsum('bhs,bls->bhl', _bf(p), _bf(v_win), preferred_element_type=F32) / denom
    b_att = jnp.concatenate(
        [o[:, h, (h // Q_PER_KV) * HEAD_DIM:(h // Q_PER_KV + 1) * HEAD_DIM] for h in range(N_HEADS)],
        axis=-1)
    g_attn = z_scr[rows, OFF_GATTN:OFF_GATTN + 512]
    b_out = b_att * _silu(g_attn)

    mix_scr[rows, 0:D_POOL + D_ATTN] = jnp.concatenate([a_out, b_out], axis=-1)

    head_rows = pl.ds(pl.multiple_of((step * vs) // RWKV_HEAD * RWKV_HEAD, RWKV_HEAD), RWKV_HEAD)
    val_rows = pl.ds(pl.multiple_of(step * vs, vs), vs)
    kk_h = vec_scr[0, head_rows, :][None]
    w_h = vec_scr[1, head_rows, :][None]
    b_h = vec_scr[2, head_rows, :][None]
    k_h = vec_scr[3, head_rows, :][None]
    r_h = vec_scr[4, head_rows, :][None]
    v_s = vec_scr[5, val_rows, :][:, None, :]
    st = s_ref[...]
    sa = jnp.sum(st * kk_h, axis=1, keepdims=True)
    st = st * w_h - sa * b_h + v_s * k_h
    so_ref[...] = st
    ot_scr[val_rows, :] = jnp.sum(st * r_h, axis=1)

    @pl.when(step == pl.num_programs(1) - 1)
    def _output_projection_all_sequences():
        g_rwkv = z_scr[:, OFF_GRWKV:OFF_GRWKV + 256]
        c_out = _rwkv_finish(ot_scr[...].T, rkv_scr[0], rkv_scr[1], rkv_scr[2], rwv_ref[...], ones_bd)
        mix_scr[:, D_POOL + D_ATTN:D_MODEL] = c_out * _silu(g_rwkv)
        y = ys_scr[...] + jnp.dot(_bf(mix_scr[...]), wout_ref[...], preferred_element_type=F32)
        ys_scr[...] = y
        y_ref[...] = y


def _sample_layers(x, pool, kc, vc, shift, wkv, cos, sin, p, bt):
    depth = pool.shape[0]
    nb = x.shape[0]
    steps = nb // bt
    vs = RWKV_HEADS * RWKV_HEAD // steps
    assert PAST_LEN >= WINDOW and nb % bt == 0 and vs * steps == RWKV_HEADS * RWKV_HEAD
    assert vs <= RWKV_HEAD and RWKV_HEAD % vs == 0 and vs % 8 == 0

    def per_layer(shape):
        nd = len(shape)
        return pl.BlockSpec((None,) + tuple(shape), lambda l, i: (l,) + (0,) * nd)

    in_specs = [
        pl.BlockSpec(memory_space=pltpu.SMEM),
        pl.BlockSpec((nb, D_MODEL), lambda l, i: (0, 0)),
        pl.BlockSpec((None, POOL_BUF, bt, D_POOL), lambda l, i: (l, 0, i, 0)),
        pl.BlockSpec((None, bt, D_KV, WINDOW), lambda l, i: (l, i, 0, 0)),
        pl.BlockSpec((None, bt, D_KV, WINDOW), lambda l, i: (l, i, 0, 0)),
        per_layer((nb, D_SHIFT_PAD)),
        pl.BlockSpec((None, vs, RWKV_HEAD, nb), lambda l, i: (l, i, 0, 0)),
        pl.BlockSpec((1, 128), lambda l, i: (0, 0)),
        pl.BlockSpec((1, 128), lambda l, i: (0, 0)),
        per_layer((1, D_MODEL)),
        per_layer((D_MODEL, OFF_GRWKV)),
        per_layer((D_MODEL, D_MODEL)),
        per_layer((D_POOL, D_POOL)),
        per_layer((128, 512)),
        per_layer((1, D_POOL)),
        per_layer((1, D_ATTN)),
        per_layer((1, D_KV)),
        per_layer((1, D_SHIFT_PAD)),
        per_layer((8, D_RWKV)),
        per_layer((D_MODEL, D_RWKV)),
    ]
    out_shape = (
        jax.ShapeDtypeStruct((depth, nb, D_MODEL), F32),
        jax.ShapeDtypeStruct((depth, POOL_BUF, nb, D_POOL), F32),
        jax.ShapeDtypeStruct((depth, nb, D_KV, WINDOW), F32),
        jax.ShapeDtypeStruct((depth, nb, D_KV, WINDOW), F32),
        jax.ShapeDtypeStruct((depth, nb, D_SHIFT), F32),
        jax.ShapeDtypeStruct((depth, RWKV_HEADS * RWKV_HEAD, RWKV_HEAD, nb), F32),
    )
    out_specs = (
        per_layer((nb, D_MODEL)),
        pl.BlockSpec((None, POOL_BUF, bt, D_POOL), lambda l, i: (l, 0, i, 0)),
        pl.BlockSpec((None, bt, D_KV, WINDOW), lambda l, i: (l, i, 0, 0)),
        pl.BlockSpec((None, bt, D_KV, WINDOW), lambda l, i: (l, i, 0, 0)),
        per_layer((nb, D_SHIFT)),
        pl.BlockSpec((None, vs, RWKV_HEAD, nb), lambda l, i: (l, i, 0, 0)),
    )
    scratch = [
        pltpu.VMEM((nb, D_MODEL), F32),
        pltpu.VMEM((nb, D_IN_PAD), F32),
        pltpu.VMEM((nb, D_MODEL), F32),
        pltpu.VMEM((3, nb, D_RWKV), F32),
        pltpu.VMEM((6, D_RWKV, nb), F32),
        pltpu.VMEM((D_RWKV, nb), F32),
    ]
    return pl.pallas_call(
        functools.partial(_sample_kernel, bt=bt, vs=vs),
        grid=(depth, steps),
        in_specs=in_specs,
        out_specs=out_specs,
        out_shape=out_shape,
        scratch_shapes=scratch,
        compiler_params=pltpu.CompilerParams(
            dimension_semantics=("arbitrary", "arbitrary"),
            vmem_limit_bytes=VMEM_LIMIT_BYTES),
        name="sample_layers",
    )(p['sinks'], x, pool, kc, vc, shift, wkv, cos, sin, p['norm_g'], p['w_in'], p['w_out'],
      p['pool_wbd'], p['lora_w'], p['pool_scale'], p['q_g'], p['k_g'], p['mu'], p['rwv'], p['w_g'])


def _rope_tables(pos):
    half = HEAD_DIM // 2
    freqs = ROPE_THETA ** (-jnp.arange(half, dtype=F32) / half)
    ang = pos.astype(F32)[:, None] * freqs[None, :]
    c = jnp.cos(ang)
    s = jnp.sin(ang)
    cos = jnp.concatenate([c, c, c, c], axis=-1)
    sin = jnp.concatenate([-s, s, -s, s], axis=-1)
    return cos, sin


def _prep_params(norm_g, w_in, w_out, pool_w, pool_scale, q_norm_g, k_norm_g, attn_sinks,
                 rwkv_mu, rwkv_w0, rwkv_w_up, rwkv_a0, rwkv_a_up, rwkv_k_k, rwkv_k_a, rwkv_r_k,
                 rwkv_ln_g, rwkv_ln_b):
    depth = w_in.shape[0]
    d_rin_end = OFF_RIN + D_SHIFT
    w_t = lax.optimization_barrier(jnp.swapaxes(w_in, 1, 2).astype(BF16))
    w_in_p = jnp.pad(jnp.swapaxes(w_t[:, :d_rin_end, :], 1, 2),
                     ((0, 0), (0, 0), (0, D_SHIFT_PAD - D_SHIFT)))
    w_gate = jnp.swapaxes(w_t[:, d_rin_end:, :], 1, 2)
    groups = len(POOL_WINDOWS)
    bd = jnp.einsum('lgcd,gh->lgchd', pool_w.astype(F32), jnp.eye(groups, dtype=F32))
    bd = bd.reshape(depth, D_POOL, D_POOL)
    zl = jnp.zeros((depth, LORA, D_RWKV), F32)
    lora = jnp.concatenate(
        [jnp.concatenate([rwkv_w_up, zl], axis=2), jnp.concatenate([zl, rwkv_a_up], axis=2),
         jnp.zeros((depth, 128 - 2 * LORA, 2 * D_RWKV), F32)], axis=1)
    rwv = jnp.stack([rwkv_w0, rwkv_a0, rwkv_k_k, rwkv_k_a, rwkv_r_k, rwkv_ln_g, rwkv_ln_b,
                     jnp.zeros((depth, D_RWKV), F32)], axis=1)
    return {
        'sinks': attn_sinks.astype(F32),
        'norm_g': norm_g[:, None, :],
        'w_in': w_in_p,
        'w_g': w_gate,
        'w_out': w_out.astype(BF16),
        'pool_wbd': bd.astype(BF16),
        'lora_w': lora.astype(BF16),
        'pool_scale': pool_scale[:, None, :],
        'q_g': jnp.tile(q_norm_g, (1, N_HEADS))[:, None, :],
        'k_g': jnp.tile(k_norm_g, (1, N_KV))[:, None, :],
        'mu': jnp.pad(rwkv_mu, ((0, 0), (0, D_SHIFT_PAD - D_SHIFT)))[:, None, :],
        'rwv': rwv,
    }


def kernel(x_prompt, x_sample, state_pool, cache_swa_k, cache_swa_v, state_rwkv_shift, state_rwkv_wkv, norm_g, w_in, w_out, pool_w, pool_scale, q_norm_g, k_norm_g, attn_sinks, rwkv_mu, rwkv_w0, rwkv_w_up, rwkv_a0, rwkv_a_up, rwkv_k_k, rwkv_k_a, rwkv_r_k, rwkv_ln_g, rwkv_ln_b):
    depth = w_in.shape[0]
    bsz, seq, _ = x_prompt.shape
    tb = 256 if seq % 256 == 0 else BLOCK
    rows = 2 if bsz % 2 == 0 else 1
    nb = x_sample.shape[0]
    bt = 32 if nb % 32 == 0 else nb
    p = _prep_params(norm_g, w_in, w_out, pool_w, pool_scale, q_norm_g, k_norm_g, attn_sinks,
                     rwkv_mu, rwkv_w0, rwkv_w_up, rwkv_a0, rwkv_a_up, rwkv_k_k, rwkv_k_a,
                     rwkv_r_k, rwkv_ln_g, rwkv_ln_b)

    cos_p, sin_p = _rope_tables(jnp.arange(seq))
    yp = x_prompt
    outs_p = [[] for _ in range(5)]
    for layer in range(depth):
        yp, pool_n, k_n, v_n, sh_n, wkv_n = _prompt_layer(yp, cos_p, sin_p, p, layer, tb, rows)
        outs_p[0].append(pool_n)
        outs_p[1].append(k_n.reshape(bsz, WINDOW, N_KV, HEAD_DIM))
        outs_p[2].append(v_n.reshape(bsz, WINDOW, N_KV, HEAD_DIM))
        outs_p[3].append(sh_n.reshape(bsz, D_SHIFT))
        outs_p[4].append(wkv_n.reshape(bsz, RWKV_HEADS, RWKV_HEAD, RWKV_HEAD))
    pool_p, k_p, v_p, sh_p, wkv_p = [jnp.stack(o) for o in outs_p]

    cos_s, sin_s = _rope_tables(jnp.full((1,), PAST_LEN))
    to_kernel_cache = lambda c: jnp.transpose(c, (0, 1, 3, 4, 2)).reshape(depth, nb, D_KV, WINDOW)
    from_kernel_cache = lambda c: jnp.transpose(
        c.reshape(depth, nb, N_KV, HEAD_DIM, WINDOW), (0, 1, 4, 2, 3))
    ys, pool_s, k_s, v_s, sh_s, wkv_s = _sample_layers(
        x_sample.reshape(nb, D_MODEL),
        jnp.transpose(state_pool, (0, 2, 1, 3)),
        to_kernel_cache(cache_swa_k), to_kernel_cache(cache_swa_v),
        jnp.pad(state_rwkv_shift, ((0, 0), (0, 0), (0, D_SHIFT_PAD - D_SHIFT))),
        jnp.transpose(state_rwkv_wkv, (0, 2, 3, 4, 1)).reshape(
            depth, RWKV_HEADS * RWKV_HEAD, RWKV_HEAD, nb),
        cos_s, sin_s, p, bt)
    return (yp, ys[depth - 1].reshape(nb, 1, D_MODEL),
            pool_p, jnp.transpose(pool_s, (0, 2, 1, 3)),
            k_p, from_kernel_cache(k_s),
            v_p, from_kernel_cache(v_s),
            sh_p, sh_s,
            wkv_p, jnp.transpose(
                wkv_s.reshape(depth, RWKV_HEADS, RWKV_HEAD, RWKV_HEAD, nb), (0, 4, 1, 2, 3)))
```

```python
import functools

import numpy as np
import jax
import jax.numpy as jnp
from jax import lax
from jax.experimental import pallas as pl
from jax.experimental.pallas import tpu as pltpu

D_MODEL = 1024
D_POOL = 256
POOL_WINDOWS = (2, 4, 8, 16)
POOL_CG = 64
POOL_BUF = 15
HEAD_DIM = 64
D_ATTN = 512
N_HEADS = 8
N_KV = 2
Q_PER_KV = 4
D_KV = 128
WINDOW = 128
BLOCK = 128
ROPE_THETA = 10000.0
QK_EPS = 1e-6
D_RWKV = 256
RWKV_HEAD = 64
RWKV_HEADS = 4
LORA = 32
D_SHIFT = 832
D_SHIFT_PAD = 896
GN_EPS = 64e-5
NORM_EPS = 1e-6
PAST_LEN = 16384
ATTN_SCALE = HEAD_DIM ** -0.5
LOG2E = 1.4426950408889634

OFF_POOL = 0
OFF_QKV = 512
OFF_GATTN = 1280
OFF_RIN = 1792
OFF_GRWKV = 2688
D_IN_PAD = 2944

CHUNK = 64
NEG = -1e30
F32 = jnp.float32
BF16 = jnp.bfloat16

VMEM_LIMIT_BYTES = 56 * 1024 * 1024


def _bf(x):
    return x.astype(BF16)


def _mm(x, y):
    return jnp.dot(_bf(x), _bf(y), preferred_element_type=F32)


def _bnt(x, y):
    return jnp.einsum('bik,bjk->bij', _bf(x), _bf(y), preferred_element_type=F32)


def _bnn(x, y):
    return jnp.einsum('bik,bkj->bij', _bf(x), _bf(y), preferred_element_type=F32)


def _btn(x, y):
    return jnp.einsum('bli,blj->bij', _bf(x), _bf(y), preferred_element_type=F32)


def _sigmoid(x):
    return 0.5 + 0.5 * jnp.tanh(0.5 * x)


def _silu(x):
    half = 0.5 * x
    return half + half * jnp.tanh(half)


def _softplus(x):
    return jnp.maximum(x, 0.0) + jnp.log(1.0 + jnp.exp(-jnp.abs(x)))


def _split3(x):
    h1 = _bf(x)
    r1 = x - h1.astype(F32)
    h2 = _bf(r1)
    h3 = _bf(r1 - h2.astype(F32))
    return h1, h2, h3


def _block_ones(n, seg):
    r = lax.broadcasted_iota(jnp.int32, (n, n), 0) // seg
    c = lax.broadcasted_iota(jnp.int32, (n, n), 1) // seg
    return jnp.where(r == c, 1.0, 0.0).astype(BF16)


def _seg_sum(x, ones_bd):
    w = x.shape[-1]
    step = min(w, ones_bd.shape[0])
    outs = [jnp.dot(_bf(x[:, c0:c0 + step]), ones_bd[:step, :step], preferred_element_type=F32)
            for c0 in range(0, w, step)]
    return outs[0] if len(outs) == 1 else jnp.concatenate(outs, axis=-1)


def _rmsnorm_rows(x, g):
    ms = jnp.mean(x * x, axis=-1, keepdims=True)
    return x * lax.rsqrt(ms + NORM_EPS) * g


def _swap_halves(x):
    w = x.shape[-1]
    lane = lax.broadcasted_iota(jnp.int32, x.shape, x.ndim - 1)
    fwd = pltpu.roll(x, w - HEAD_DIM // 2, x.ndim - 1)
    bwd = pltpu.roll(x, HEAD_DIM // 2, x.ndim - 1)
    return jnp.where(lane % HEAD_DIM < HEAD_DIM // 2, fwd, bwd)


def _qk_norm_rope(x, g, cos, sin_signed, ones_bd):
    ms = _seg_sum(x * x, ones_bd) * (1.0 / HEAD_DIM)
    xn = x * lax.rsqrt(ms + QK_EPS) * g
    reps = x.shape[-1] // cos.shape[-1]
    if reps > 1:
        cos = jnp.concatenate([cos] * reps, axis=-1)
        sin_signed = jnp.concatenate([sin_signed] * reps, axis=-1)
    return xn * cos + _swap_halves(xn) * sin_signed


def _rwkv_token_params(xs, rwv, lora_w, ones_bd):
    r = xs[:, 0:256]
    k = xs[:, 256:512]
    v = xs[:, 512:768]
    lo = xs[:, 768:896]
    lane = lax.broadcasted_iota(jnp.int32, lo.shape, 1)
    lo_in = jnp.where(lane < LORA, jnp.tanh(lo), lo)
    pre = _mm(lo_in, lora_w)
    w_pre = rwv[0:1, :] + pre[:, 0:256]
    a_pre = rwv[1:2, :] + pre[:, 256:512]
    w_log = -_softplus(-w_pre) - 0.5
    logw = -jnp.exp(w_log)
    a = _sigmoid(a_pre)
    kk = k * rwv[2:3, :]
    nrm = jnp.sqrt(_seg_sum(kk * kk, ones_bd))
    kk = kk / jnp.maximum(nrm, 1e-12)
    k2 = k * (1.0 + (a - 1.0) * rwv[3:4, :])
    return r, k2, v, kk, a, logw


def _rwkv_finish(o, r, k2, v, rwv, ones_bd):
    inv_n = 1.0 / RWKV_HEAD
    mu_o = _seg_sum(o, ones_bd) * inv_n
    d = o - mu_o
    var = _seg_sum(d * d, ones_bd) * inv_n
    on = d * lax.rsqrt(var + GN_EPS) * rwv[5:6, :] + rwv[6:7, :]
    bonus = _seg_sum(r * k2 * rwv[4:5, :], ones_bd) * v
    return on + bonus


def _run_interleaved(programs):
    state = [[prog, next(prog), 0] for prog in programs]
    last = 'M'
    while state:
        want = 'V' if last == 'M' else 'M'
        order = sorted(state, key=lambda s: s[2])
        pick = next((s for s in order if s[1] == want), None)
        if pick is None:
            pick = next((s for s in order if s[1] == 'X'), order[0])
        if pick[1] != 'X':
            last = pick[1]
        pick[2] += 1
        try:
            pick[1] = next(pick[0])
        except StopIteration:
            state.remove(pick)


def _prompt_kernel(sinks_ref, x_ref, cos_ref, sin_ref, ng_ref, win_ref, wout_ref, pwbd_ref,
                   lora_ref, pscale_ref, qg_ref, kg_ref, mu_ref, rwv_ref, wg_ref,
                   y_ref, pool_ref, ko_ref, vo_ref, sh_ref, wkv_ref,
                   pext_all, kprev_all, vprev_all, rext_all, s_all, *, tb, rows, layer):
    t = pl.program_id(1)

    @pl.when(t == 0)
    def _init():
        pext_all[...] = jnp.zeros(pext_all.shape, F32)
        kprev_all[...] = jnp.zeros(kprev_all.shape, BF16)
        vprev_all[...] = jnp.zeros(vprev_all.shape, BF16)
        rext_all[:, pl.ds(0, 8), :] = jnp.zeros((rows, 8, D_SHIFT_PAD), F32)
        s_all[...] = jnp.zeros(s_all.shape, F32)

    ones_bd = _block_ones(256, HEAD_DIM)
    nrow = Q_PER_KV * BLOCK
    qi = lax.broadcasted_iota(jnp.int32, (nrow, 2 * BLOCK), 0) % BLOCK
    kj = lax.broadcasted_iota(jnp.int32, (nrow, 2 * BLOCK), 1)
    band = jnp.where(kj > qi, jnp.where(kj <= qi + BLOCK, 0.0, NEG), NEG)
    first_lo = jnp.where(t == 0, BLOCK, 0)
    band_first = jnp.where(kj >= first_lo, band, NEG)
    hrow = lax.broadcasted_iota(jnp.int32, (nrow, 1), 0) // BLOCK
    ti = lax.broadcasted_iota(jnp.int32, (tb, tb), 0)
    tj = lax.broadcasted_iota(jnp.int32, (tb, tb), 1)
    tril_blk = jnp.where(ti // CHUNK == tj // CHUNK, jnp.where(tj <= ti, 1.0, 0.0), 0.0).astype(BF16)
    shared = (ones_bd, band, band_first, hrow, tril_blk)
    programs = [
        _prompt_tile(i, t, layer, shared, sinks_ref, x_ref, cos_ref, sin_ref, ng_ref, win_ref, wout_ref,
                     pwbd_ref, lora_ref, pscale_ref, qg_ref, kg_ref, mu_ref, rwv_ref, wg_ref,
                     y_ref, pool_ref, ko_ref, vo_ref, sh_ref, wkv_ref,
                     pext_all.at[i], kprev_all.at[i], vprev_all.at[i], rext_all.at[i], s_all.at[i],
                     tb=tb)
        for i in range(rows)]
    _run_interleaved(programs)


def _prompt_tile(i, t, layer, shared, sinks_ref, x_ref, cos_ref, sin_ref, ng_ref, win_ref, wout_ref,
                 pwbd_ref, lora_ref, pscale_ref, qg_ref, kg_ref, mu_ref, rwv_ref, wg_ref,
                 y_ref, pool_ref, ko_ref, vo_ref, sh_ref, wkv_ref,
                 pext, kprev, vprev, rext, s_ref, *, tb):
    nqb = tb // BLOCK
    nch = tb // CHUNK
    ones_bd, band, band_first, hrow, tril_blk = shared

    def proj(off, width):
        return jnp.dot(hb, win_ref[:, off:off + width], preferred_element_type=F32)

    yield 'V'
    x = x_ref[i]
    hb = _bf(_rmsnorm_rows(x, ng_ref[...]))

    yield 'M'
    zp = proj(OFF_POOL, 512)
    yield 'M'
    zq = proj(OFF_QKV, 768)

    yield 'V'
    u = zp[:, 0:256]
    g_pool = zp[:, 256:512]
    pext[pl.ds(24, tb), :] = u
    n_ext = tb + 16
    wins = []
    for sh in (1, 2, 4, 8):
        cur = pext[pl.ds(8, n_ext), :] + pext[pl.ds(8 - sh, n_ext), :]
        pext[pl.ds(8, n_ext), :] = cur
        wins.append(cur[16:16 + tb, :])
    pext[pl.ds(8, 16), :] = u[tb - 16:tb, :]
    lane = lax.broadcasted_iota(jnp.int32, (tb, D_POOL), 1)
    row = lax.broadcasted_iota(jnp.int32, (tb, D_POOL), 0)
    grp = lane // POOL_CG
    win_sum = jnp.where(grp == 0, wins[0], jnp.where(grp == 1, wins[1],
                        jnp.where(grp == 2, wins[2], wins[3])))
    wlen = jnp.where(grp == 0, 2, jnp.where(grp == 1, 4, jnp.where(grp == 2, 8, 16)))
    cnt = jnp.minimum(t * tb + row + 1, wlen).astype(F32)
    pooled = win_sum / cnt - u
    a_out = _mm(pooled, pwbd_ref[...]) * pscale_ref[...] * _silu(g_pool)
    pool_ref[i] = u[tb - POOL_BUF:tb, :]

    yield 'M'
    rin = proj(OFF_RIN, D_SHIFT_PAD)

    yield 'V'
    cos = cos_ref[...]
    sin = sin_ref[...]
    q = _qk_norm_rope(zq[:, 0:512], qg_ref[...], cos, sin, ones_bd) * (ATTN_SCALE * LOG2E)
    qb = _bf(q)

    yield 'M'
    g_attn = proj(OFF_GATTN, 512)
    g_rwkv = jnp.dot(hb, wg_ref[...], preferred_element_type=F32)

    yield 'V'
    k = _qk_norm_rope(zq[:, 512:640], kg_ref[...], cos, sin, ones_bd)
    v = zq[:, 640:768]
    kb = _bf(k)
    vb = _bf(v)
    ko_ref[i] = k[tb - WINDOW:tb, :]
    vo_ref[i] = v[tb - WINDOW:tb, :]
    kprev_val = kprev[...]
    vprev_val = vprev[...]
    kprev[...] = kb[tb - BLOCK:tb, :]
    vprev[...] = vb[tb - BLOCK:tb, :]

    def attn_unit(j, g):
        rs = slice(j * BLOCK, (j + 1) * BLOCK)
        if j == 0:
            kx = jnp.concatenate([kprev_val, kb[rs]], axis=0)
            vx = jnp.concatenate([vprev_val, vb[rs]], axis=0)
            valid = band_first
        else:
            kx = kb[(j - 1) * BLOCK:(j + 1) * BLOCK]
            vx = vb[(j - 1) * BLOCK:(j + 1) * BLOCK]
            valid = band
        kxg = kx[:, g * HEAD_DIM:(g + 1) * HEAD_DIM]
        vxg = vx[:, g * HEAD_DIM:(g + 1) * HEAD_DIM]
        qs = jnp.concatenate(
            [qb[rs, (g * Q_PER_KV + hq) * HEAD_DIM:(g * Q_PER_KV + hq + 1) * HEAD_DIM]
             for hq in range(Q_PER_KV)], axis=0)
        s = lax.dot_general(qs, kxg, (((1,), (1,)), ((), ())),
                            preferred_element_type=F32)
        s = s + valid
        sink = jnp.where(hrow == 0, sinks_ref[layer, g * Q_PER_KV],
                         jnp.where(hrow == 1, sinks_ref[layer, g * Q_PER_KV + 1],
                                   jnp.where(hrow == 2, sinks_ref[layer, g * Q_PER_KV + 2],
                                             sinks_ref[layer, g * Q_PER_KV + 3]))) * LOG2E
        m = jnp.maximum(jnp.max(s, axis=-1, keepdims=True), sink)
        p = jnp.exp2(s - m)
        denom = jnp.sum(p, axis=-1, keepdims=True) + jnp.exp2(sink - m)
        o = jnp.dot(_bf(p), vxg, preferred_element_type=F32) / denom
        return [o[hq * BLOCK:(hq + 1) * BLOCK, :] for hq in range(Q_PER_KV)]

    yield 'V'
    rext[pl.ds(8, tb), :] = rin
    prev = rext[pl.ds(7, tb), :]
    rext[pl.ds(7, 1), :] = rin[tb - 1:tb, :]
    xs = rin + (prev - rin) * mu_ref[...]
    sh_ref[i] = rin[tb - 1:tb, 0:D_SHIFT]
    rwv = rwv_ref[...]
    r, k2, vv, kk, a, logw = _rwkv_token_params(xs, rwv, lora_ref[...], ones_bd)

    yield 'M'
    cum = None
    for piece in _split3(logw):
        c = jnp.dot(tril_blk, piece, preferred_element_type=F32)
        cum = c if cum is None else cum + c

    attn_done = {}
    for jg in [(j, g) for j in range(nqb) for g in range(N_KV)]:
        yield 'X'
        attn_done[jg] = attn_unit(*jg)

    yield 'V'
    cum3 = cum.reshape(nch, CHUNK, D_RWKV)
    cum_last = jnp.broadcast_to(cum3[:, CHUNK - 1:CHUNK, :], cum3.shape).reshape(tb, D_RWKV)
    w_inc = jnp.exp(cum)
    w_exc = jnp.exp(cum - logw)
    w_inv = jnp.exp(-cum)
    rel = jnp.exp(cum_last - cum)
    w_last = jnp.exp(cum_last)
    bvec = kk * a

    npair = RWKV_HEADS // 2
    pw2 = 2 * RWKV_HEAD

    def stack(z):
        parts = [z[:, p * pw2:(p + 1) * pw2].reshape(nch, 1, CHUNK, pw2) for p in range(npair)]
        return jnp.concatenate(parts, axis=1).reshape(nch * npair, CHUNK, pw2)

    low = lax.broadcasted_iota(jnp.int32, (1, 1, pw2), 2) < RWKV_HEAD

    def bdiag(m):
        zero = jnp.zeros_like(m)
        return jnp.concatenate([jnp.where(low, m, zero), jnp.where(low, zero, m)], axis=1)

    def diag_blocks(m):
        lane = lax.broadcasted_iota(jnp.int32, (1, 1, m.shape[2]), 2)
        first = jnp.bitwise_and(lane, pw2 - 1) < RWKV_HEAD
        return jnp.where(first, m[:, 0:RWKV_HEAD, :], m[:, RWKV_HEAD:pw2, :])

    kap = stack(kk * w_exc)
    bt = stack(bvec * w_inv)
    kt = stack(k2 * w_inv)
    rt = stack(r * w_inc)
    bt_l = stack(bvec * rel)
    kt_l = stack(k2 * rel)
    vs = stack(vv)
    wl = stack(w_last)

    yield 'M'
    li = lax.broadcasted_iota(jnp.int32, (1, CHUNK, 2 * pw2), 1)
    lj = jnp.bitwise_and(lax.broadcasted_iota(jnp.int32, (1, CHUNK, 2 * pw2), 2), CHUNK - 1)
    btkt = jnp.concatenate([bdiag(_bf(bt)), bdiag(_bf(kt))], axis=1)
    a_bk = jnp.where(lj < li, _bnt(kap, btkt), 0.0)
    p_bk = jnp.where(lj <= li, _bnt(rt, btkt), 0.0)
    a_b = a_bk[:, :, 0:pw2]
    a_k = a_bk[:, :, pw2:2 * pw2]
    vs_bd = bdiag(_bf(vs))
    xk = kap
    xu = -_bnn(a_k, vs_bd)
    pw = -a_b
    span = 1
    while 2 * span < CHUNK:
        yield 'M'
        pwb = _bf(pw)
        both = _bnn(pwb, jnp.concatenate([bdiag(_bf(xk)), bdiag(_bf(xu)), bdiag(pwb)], axis=2))
        xk = xk + both[:, :, 0:pw2]
        xu = xu + both[:, :, pw2:2 * pw2]
        pw = both[:, :, 2 * pw2:3 * pw2]
        span *= 2
    yield 'M'
    both = _bnn(pw, jnp.concatenate([bdiag(_bf(xk)), bdiag(_bf(xu))], axis=2))
    xk = xk + both[:, :, 0:pw2]
    xu = xu + both[:, :, pw2:2 * pw2]
    yield 'M'
    xk_bd = bdiag(_bf(xk))
    xu_bd = bdiag(_bf(xu))
    rhs = jnp.concatenate([jnp.concatenate([xu_bd, xk_bd], axis=2),
                           jnp.concatenate([vs_bd, jnp.zeros_like(vs_bd)], axis=2)], axis=1)
    o0_pk = _bnn(p_bk, rhs)
    o0 = o0_pk[:, :, 0:pw2]
    rp = rt - o0_pk[:, :, pw2:2 * pw2]
    tn = diag_blocks(_btn(bt_l, jnp.concatenate([xk, xu], axis=2)))
    eye = jnp.where(lax.broadcasted_iota(jnp.int32, (1, CHUNK, pw2), 1)
                    == jnp.bitwise_and(lax.broadcasted_iota(jnp.int32, (1, CHUNK, pw2), 2), CHUNK - 1),
                    1.0, 0.0)
    m_t = eye * wl - tn[:, :, 0:pw2]
    c_t = tn[:, :, pw2:2 * pw2] + diag_blocks(_btn(kt_l, vs))
    rp_mt = jnp.concatenate([rp, m_t], axis=1)

    yield 'V'
    b_blocks = [jnp.concatenate(attn_done[(j, 0)] + attn_done[(j, 1)], axis=-1) for j in range(nqb)]
    b_att = b_blocks[0] if nqb == 1 else jnp.concatenate(b_blocks, axis=0)
    b_out = b_att * _silu(g_attn)

    st_cur = s_ref[...]
    o_chunks = []
    for c in range(nch):
        yield 'M'
        sl = slice(c * npair, (c + 1) * npair)
        both = _bnn(rp_mt[sl], bdiag(_bf(st_cur)))
        o_chunks.append(both[:, 0:CHUNK, :] + o0[sl])
        st_cur = both[:, CHUNK:2 * CHUNK, :] + c_t[sl]
    s_ref[...] = st_cur
    for p in range(npair):
        wkv_ref[i, pl.ds(p * pw2, pw2), :] = st_cur[p].T

    yield 'V'
    o_wkv = jnp.concatenate(
        [jnp.concatenate([oc[p] for p in range(npair)], axis=-1) for oc in o_chunks], axis=0)
    c_out = _rwkv_finish(o_wkv, r, k2, vv, rwv, ones_bd) * _silu(g_rwkv)

    yield 'M'
    mix = jnp.concatenate([_bf(a_out), _bf(b_out), _bf(c_out)], axis=-1)
    y_ref[i] = x + jnp.dot(mix, wout_ref[...], preferred_element_type=F32)


def _layer_spec(shape, layer):
    nd = len(shape)
    return pl.BlockSpec((None,) + tuple(shape), lambda *_: (layer,) + (0,) * nd)


def _prompt_layer(x, cos, sin, p, layer, tb, rows):
    bsz, seq, _ = x.shape
    nt = seq // tb
    grid = (bsz // rows, nt)
    in_specs = [
        pl.BlockSpec(memory_space=pltpu.SMEM),
        pl.BlockSpec((rows, tb, D_MODEL), lambda b, t: (b, t, 0)),
        pl.BlockSpec((tb, 128), lambda b, t: (t, 0)),
        pl.BlockSpec((tb, 128), lambda b, t: (t, 0)),
        _layer_spec((1, D_MODEL), layer),
        _layer_spec((D_MODEL, OFF_GRWKV), layer),
        _layer_spec((D_MODEL, D_MODEL), layer),
        _layer_spec((D_POOL, D_POOL), layer),
        _layer_spec((128, 512), layer),
        _layer_spec((1, D_POOL), layer),
        _layer_spec((1, D_ATTN), layer),
        _layer_spec((1, D_KV), layer),
        _layer_spec((1, D_SHIFT_PAD), layer),
        _layer_spec((8, D_RWKV), layer),
        _layer_spec((D_MODEL, D_RWKV), layer),
    ]
    out_shape = (
        jax.ShapeDtypeStruct((bsz, seq, D_MODEL), F32),
        jax.ShapeDtypeStruct((bsz, POOL_BUF, D_POOL), F32),
        jax.ShapeDtypeStruct((bsz, WINDOW, D_KV), F32),
        jax.ShapeDtypeStruct((bsz, WINDOW, D_KV), F32),
        jax.ShapeDtypeStruct((bsz, 1, D_SHIFT), F32),
        jax.ShapeDtypeStruct((bsz, RWKV_HEADS * RWKV_HEAD, RWKV_HEAD), F32),
    )
    out_specs = (
        pl.BlockSpec((rows, tb, D_MODEL), lambda b, t: (b, t, 0)),
        pl.BlockSpec((rows, POOL_BUF, D_POOL), lambda b, t: (b, 0, 0)),
        pl.BlockSpec((rows, WINDOW, D_KV), lambda b, t: (b, 0, 0)),
        pl.BlockSpec((rows, WINDOW, D_KV), lambda b, t: (b, 0, 0)),
        pl.BlockSpec((rows, 1, D_SHIFT), lambda b, t: (b, 0, 0)),
        pl.BlockSpec((rows, RWKV_HEADS * RWKV_HEAD, RWKV_HEAD), lambda b, t: (b, 0, 0)),
    )
    scratch = [
        pltpu.VMEM((rows, tb + 24, D_POOL), F32),
        pltpu.VMEM((rows, BLOCK, D_KV), BF16),
        pltpu.VMEM((rows, BLOCK, D_KV), BF16),
        pltpu.VMEM((rows, tb + 8, D_SHIFT_PAD), F32),
        pltpu.VMEM((rows, RWKV_HEADS // 2, RWKV_HEAD, 2 * RWKV_HEAD), F32),
    ]
    return pl.pallas_call(
        functools.partial(_prompt_kernel, tb=tb, rows=rows, layer=layer),
        grid=grid,
        in_specs=in_specs,
        out_specs=out_specs,
        out_shape=out_shape,
        scratch_shapes=scratch,
        compiler_params=pltpu.CompilerParams(
            dimension_semantics=("parallel", "arbitrary"),
            vmem_limit_bytes=VMEM_LIMIT_BYTES),
        name="prompt_layer",
    )(p['sinks'], x, cos, sin, p['norm_g'], p['w_in'], p['w_out'], p['pool_wbd'], p['lora_w'],
      p['pool_scale'], p['q_g'], p['k_g'], p['mu'], p['rwv'], p['w_g'])


def _sample_kernel(sinks_ref, x_ref, pool_ref, kc_ref, vc_ref, shp_ref, s_ref, cos_ref, sin_ref,
                   ng_ref, win_ref, wout_ref, pwbd_ref, lora_ref, pscale_ref, qg_ref, kg_ref,
                   mu_ref, rwv_ref, wg_ref,
                   y_ref, poolo_ref, ko_ref, vo_ref, sho_ref, so_ref,
                   ys_scr, z_scr, mix_scr, rkv_scr, vec_scr, ot_scr, *, bt, vs):
    layer = pl.program_id(0)
    step = pl.program_id(1)
    rows = pl.ds(pl.multiple_of(step * bt, bt), bt)
    ones_bd = _block_ones(256, HEAD_DIM)

    @pl.when(step == 0)
    def _project_all_sequences():
        @pl.when(layer == 0)
        def _first_layer_input():
            ys_scr[...] = x_ref[...]
        hb_all = _bf(_rmsnorm_rows(ys_scr[...], ng_ref[...]))
        z_scr[:, 0:OFF_GRWKV] = jnp.dot(hb_all, win_ref[...], preferred_element_type=F32)
        z_scr[:, OFF_GRWKV:D_IN_PAD] = jnp.dot(hb_all, wg_ref[...], preferred_element_type=F32)
        rin = z_scr[:, OFF_RIN:OFF_RIN + D_SHIFT_PAD]
        xs = rin + (shp_ref[...] - rin) * mu_ref[...]
        sho_ref[...] = rin[:, 0:D_SHIFT]
        r, k2, vv, kk, a, logw = _rwkv_token_params(xs, rwv_ref[...], lora_ref[...], ones_bd)
        rkv_scr[0] = r
        rkv_scr[1] = k2
        rkv_scr[2] = vv
        for n, vec in enumerate((kk, jnp.exp(logw), kk * a, k2, r, vv)):
            vec_scr[n] = vec.T

    zp = z_scr[rows, OFF_POOL:OFF_POOL + 512]
    u = zp[:, 0:256]
    g_pool = zp[:, 256:512]
    buf = pool_ref[...]
    ri = lax.broadcasted_iota(jnp.int32, (POOL_BUF, 1, D_POOL), 0)
    grp3 = lax.broadcasted_iota(jnp.int32, (POOL_BUF, 1, D_POOL), 2) // POOL_CG
    wlen3 = jnp.where(grp3 == 0, 2, jnp.where(grp3 == 1, 4, jnp.where(grp3 == 2, 8, 16)))
    tail = jnp.sum(jnp.where(ri >= POOL_BUF + 1 - wlen3, buf, 0.0), axis=0)
    grp = lax.broadcasted_iota(jnp.int32, (bt, D_POOL), 1) // POOL_CG
    wlen = jnp.where(grp == 0, 2, jnp.where(grp == 1, 4, jnp.where(grp == 2, 8, 16)))
    cnt = jnp.minimum(PAST_LEN + 1, wlen).astype(F32)
    pooled = (tail + u) / cnt - u
    a_out = _mm(pooled, pwbd_ref[...]) * pscale_ref[...] * _silu(g_pool)
    poolo_ref[pl.ds(0, POOL_BUF - 1)] = pool_ref[pl.ds(1, POOL_BUF - 1)]
    poolo_ref[POOL_BUF - 1] = u

    zq = z_scr[rows, OFF_QKV:OFF_QKV + 768]
    cos = cos_ref[...]
    sin = sin_ref[...]
    q = _qk_norm_rope(zq[:, 0:512], qg_ref[...], cos, sin, ones_bd) * ATTN_SCALE
    k_new = _qk_norm_rope(zq[:, 512:640], kg_ref[...], cos, sin, ones_bd)
    v_new = zq[:, 640:768]
    si = lax.broadcasted_iota(jnp.int32, (1, D_KV, WINDOW), 2)
    k_win = jnp.where(si == WINDOW - 1, k_new[:, :, None], pltpu.roll(kc_ref[...], WINDOW - 1, 2))
    v_win = jnp.where(si == WINDOW - 1, v_new[:, :, None], pltpu.roll(vc_ref[...], WINDOW - 1, 2))
    ko_ref[...] = k_win
    vo_ref[...] = v_win
    lane_g = lax.broadcasted_iota(jnp.int32, (bt, D_KV), 1) // HEAD_DIM
    q_rows = []
    for h in range(N_HEADS):
        qh = q[:, h * HEAD_DIM:(h + 1) * HEAD_DIM]
        q_rows.append(jnp.where(lane_g == h // Q_PER_KV, jnp.concatenate([qh, qh], axis=-1), 0.0))
    qe = jnp.stack(q_rows, axis=1)
    s = jnp.einsum('bhl,bls->bhs', _bf(qe), _bf(k_win), preferred_element_type=F32)
    hi = lax.broadcasted_iota(jnp.int32, (1, N_HEADS, 1), 1)
    sink = jnp.zeros((1, N_HEADS, 1), F32)
    for h in range(N_HEADS):
        sink = jnp.where(hi == h, sinks_ref[layer, h], sink)
    m = jnp.maximum(jnp.max(s, axis=-1, keepdims=True), sink)
    p = jnp.exp(s - m)
    denom = jnp.sum(p, axis=-1, keepdims=True) + jnp.exp(sink - m)
    o = jnp.einsum('bhs,bls->bhl', _bf(p), _bf(v_win), preferred_element_type=F32) / denom
    b_att = jnp.concatenate(
        [o[:, h, (h // Q_PER_KV) * HEAD_DIM:(h // Q_PER_KV + 1) * HEAD_DIM] for h in range(N_HEADS)],
        axis=-1)
    g_attn = z_scr[rows, OFF_GATTN:OFF_GATTN + 512]
    b_out = b_att * _silu(g_attn)

    mix_scr[rows, 0:D_POOL + D_ATTN] = jnp.concatenate([a_out, b_out], axis=-1)

    head_rows = pl.ds(pl.multiple_of((step * vs) // RWKV_HEAD * RWKV_HEAD, RWKV_HEAD), RWKV_HEAD)
    val_rows = pl.ds(pl.multiple_of(step * vs, vs), vs)
    kk_h = vec_scr[0, head_rows, :][None]
    w_h = vec_scr[1, head_rows, :][None]
    b_h = vec_scr[2, head_rows, :][None]
    k_h = vec_scr[3, head_rows, :][None]
    r_h = vec_scr[4, head_rows, :][None]
    v_s = vec_scr[5, val_rows, :][:, None, :]
    st = s_ref[...]
    sa = jnp.sum(st * kk_h, axis=1, keepdims=True)
    st = st * w_h - sa * b_h + v_s * k_h
    so_ref[...] = st
    ot_scr[val_rows, :] = jnp.sum(st * r_h, axis=1)

    @pl.when(step == pl.num_programs(1) - 1)
    def _output_projection_all_sequences():
        g_rwkv = z_scr[:, OFF_GRWKV:OFF_GRWKV + 256]
        c_out = _rwkv_finish(ot_scr[...].T, rkv_scr[0], rkv_scr[1], rkv_scr[2], rwv_ref[...], ones_bd)
        mix_scr[:, D_POOL + D_ATTN:D_MODEL] = c_out * _silu(g_rwkv)
        y = ys_scr[...] + jnp.dot(_bf(mix_scr[...]), wout_ref[...], preferred_element_type=F32)
        ys_scr[...] = y
        y_ref[...] = y


def _sample_layers(x, pool, kc, vc, shift, wkv, cos, sin, p, bt):
    depth = pool.shape[0]
    nb = x.shape[0]
    steps = nb // bt
    vs = RWKV_HEADS * RWKV_HEAD // steps
    assert PAST_LEN >= WINDOW and nb % bt == 0 and vs * steps == RWKV_HEADS * RWKV_HEAD
    assert vs <= RWKV_HEAD and RWKV_HEAD % vs == 0 and vs % 8 == 0

    def per_layer(shape):
        nd = len(shape)
        return pl.BlockSpec((None,) + tuple(shape), lambda l, i: (l,) + (0,) * nd)

    in_specs = [
        pl.BlockSpec(memory_space=pltpu.SMEM),
        pl.BlockSpec((nb, D_MODEL), lambda l, i: (0, 0)),
        pl.BlockSpec((None, POOL_BUF, bt, D_POOL), lambda l, i: (l, 0, i, 0)),
        pl.BlockSpec((None, bt, D_KV, WINDOW), lambda l, i: (l, i, 0, 0)),
        pl.BlockSpec((None, bt, D_KV, WINDOW), lambda l, i: (l, i, 0, 0)),
        per_layer((nb, D_SHIFT_PAD)),
        pl.BlockSpec((None, vs, RWKV_HEAD, nb), lambda l, i: (l, i, 0, 0)),
        pl.BlockSpec((1, 128), lambda l, i: (0, 0)),
        pl.BlockSpec((1, 128), lambda l, i: (0, 0)),
        per_layer((1, D_MODEL)),
        per_layer((D_MODEL, OFF_GRWKV)),
        per_layer((D_MODEL, D_MODEL)),
        per_layer((D_POOL, D_POOL)),
        per_layer((128, 512)),
        per_layer((1, D_POOL)),
        per_layer((1, D_ATTN)),
        per_layer((1, D_KV)),
        per_layer((1, D_SHIFT_PAD)),
        per_layer((8, D_RWKV)),
        per_layer((D_MODEL, D_RWKV)),
    ]
    out_shape = (
        jax.ShapeDtypeStruct((depth, nb, D_MODEL), F32),
        jax.ShapeDtypeStruct((depth, POOL_BUF, nb, D_POOL), F32),
        jax.ShapeDtypeStruct((depth, nb, D_KV, WINDOW), F32),
        jax.ShapeDtypeStruct((depth, nb, D_KV, WINDOW), F32),
        jax.ShapeDtypeStruct((depth, nb, D_SHIFT), F32),
        jax.ShapeDtypeStruct((depth, RWKV_HEADS * RWKV_HEAD, RWKV_HEAD, nb), F32),
    )
    out_specs = (
        per_layer((nb, D_MODEL)),
        pl.BlockSpec((None, POOL_BUF, bt, D_POOL), lambda l, i: (l, 0, i, 0)),
        pl.BlockSpec((None, bt, D_KV, WINDOW), lambda l, i: (l, i, 0, 0)),
        pl.BlockSpec((None, bt, D_KV, WINDOW), lambda l, i: (l, i, 0, 0)),
        per_layer((nb, D_SHIFT)),
        pl.BlockSpec((None, vs, RWKV_HEAD, nb), lambda l, i: (l, i, 0, 0)),
    )
    scratch = [
        pltpu.VMEM((nb, D_MODEL), F32),
        pltpu.VMEM((nb, D_IN_PAD), F32),
        pltpu.VMEM((nb, D_MODEL), F32),
        pltpu.VMEM((3, nb, D_RWKV), F32),
        pltpu.VMEM((6, D_RWKV, nb), F32),
        pltpu.VMEM((D_RWKV, nb), F32),
    ]
    return pl.pallas_call(
        functools.partial(_sample_kernel, bt=bt, vs=vs),
        grid=(depth, steps),
        in_specs=in_specs,
        out_specs=out_specs,
        out_shape=out_shape,
        scratch_shapes=scratch,
        compiler_params=pltpu.CompilerParams(
            dimension_semantics=("arbitrary", "arbitrary"),
            vmem_limit_bytes=VMEM_LIMIT_BYTES),
        name="sample_layers",
    )(p['sinks'], x, pool, kc, vc, shift, wkv, cos, sin, p['norm_g'], p['w_in'], p['w_out'],
      p['pool_wbd'], p['lora_w'], p['pool_scale'], p['q_g'], p['k_g'], p['mu'], p['rwv'], p['w_g'])


def _rope_tables(pos):
    half = HEAD_DIM // 2
    freqs = ROPE_THETA ** (-jnp.arange(half, dtype=F32) / half)
    ang = pos.astype(F32)[:, None] * freqs[None, :]
    c = jnp.cos(ang)
    s = jnp.sin(ang)
    cos = jnp.concatenate([c, c, c, c], axis=-1)
    sin = jnp.concatenate([-s, s, -s, s], axis=-1)
    return cos, sin


def _prep_params(norm_g, w_in, w_out, pool_w, pool_scale, q_norm_g, k_norm_g, attn_sinks,
                 rwkv_mu, rwkv_w0, rwkv_w_up, rwkv_a0, rwkv_a_up, rwkv_k_k, rwkv_k_a, rwkv_r_k,
                 rwkv_ln_g, rwkv_ln_b):
    depth = w_in.shape[0]
    d_rin_end = OFF_RIN + D_SHIFT
    w_in_p = jnp.pad(w_in[:, :, :d_rin_end].astype(BF16),
                     ((0, 0), (0, 0), (0, D_SHIFT_PAD - D_SHIFT)))
    w_gate = w_in[:, :, d_rin_end:].astype(BF16)
    groups = len(POOL_WINDOWS)
    bd = jnp.einsum('lgcd,gh->lgchd', pool_w.astype(F32), jnp.eye(groups, dtype=F32))
    bd = bd.reshape(depth, D_POOL, D_POOL)
    zl = jnp.zeros((depth, LORA, D_RWKV), F32)
    lora = jnp.concatenate(
        [jnp.concatenate([rwkv_w_up, zl], axis=2), jnp.concatenate([zl, rwkv_a_up], axis=2),
         jnp.zeros((depth, 128 - 2 * LORA, 2 * D_RWKV), F32)], axis=1)
    rwv = jnp.stack([rwkv_w0, rwkv_a0, rwkv_k_k, rwkv_k_a, rwkv_r_k, rwkv_ln_g, rwkv_ln_b,
                     jnp.zeros((depth, D_RWKV), F32)], axis=1)
    return {
        'sinks': attn_sinks.astype(F32),
        'norm_g': norm_g[:, None, :],
        'w_in': w_in_p,
        'w_g': w_gate,
        'w_out': w_out.astype(BF16),
        'pool_wbd': bd.astype(BF16),
        'lora_w': lora.astype(BF16),
        'pool_scale': pool_scale[:, None, :],
        'q_g': jnp.tile(q_norm_g, (1, N_HEADS))[:, None, :],
        'k_g': jnp.tile(k_norm_g, (1, N_KV))[:, None, :],
        'mu': jnp.pad(rwkv_mu, ((0, 0), (0, D_SHIFT_PAD - D_SHIFT)))[:, None, :],
        'rwv': rwv,
    }


def kernel(x_prompt, x_sample, state_pool, cache_swa_k, cache_swa_v, state_rwkv_shift, state_rwkv_wkv, norm_g, w_in, w_out, pool_w, pool_scale, q_norm_g, k_norm_g, attn_sinks, rwkv_mu, rwkv_w0, rwkv_w_up, rwkv_a0, rwkv_a_up, rwkv_k_k, rwkv_k_a, rwkv_r_k, rwkv_ln_g, rwkv_ln_b):
    depth = w_in.shape[0]
    bsz, seq, _ = x_prompt.shape
    tb = 256 if seq % 256 == 0 else BLOCK
    rows = 2 if bsz % 2 == 0 else 1
    nb = x_sample.shape[0]
    bt = 32 if nb % 32 == 0 else nb
    p = _prep_params(norm_g, w_in, w_out, pool_w, pool_scale, q_norm_g, k_norm_g, attn_sinks,
                     rwkv_mu, rwkv_w0, rwkv_w_up, rwkv_a0, rwkv_a_up, rwkv_k_k, rwkv_k_a,
                     rwkv_r_k, rwkv_ln_g, rwkv_ln_b)

    cos_p, sin_p = _rope_tables(jnp.arange(seq))
    yp = x_prompt
    outs_p = [[] for _ in range(5)]
    for layer in range(depth):
        yp, pool_n, k_n, v_n, sh_n, wkv_n = _prompt_layer(yp, cos_p, sin_p, p, layer, tb, rows)
        outs_p[0].append(pool_n)
        outs_p[1].append(k_n.reshape(bsz, WINDOW, N_KV, HEAD_DIM))
        outs_p[2].append(v_n.reshape(bsz, WINDOW, N_KV, HEAD_DIM))
        outs_p[3].append(sh_n.reshape(bsz, D_SHIFT))
        outs_p[4].append(wkv_n.reshape(bsz, RWKV_HEADS, RWKV_HEAD, RWKV_HEAD))
    pool_p, k_p, v_p, sh_p, wkv_p = [jnp.stack(o) for o in outs_p]

    cos_s, sin_s = _rope_tables(jnp.full((1,), PAST_LEN))
    to_kernel_cache = lambda c: jnp.transpose(c, (0, 1, 3, 4, 2)).reshape(depth, nb, D_KV, WINDOW)
    from_kernel_cache = lambda c: jnp.transpose(
        c.reshape(depth, nb, N_KV, HEAD_DIM, WINDOW), (0, 1, 4, 2, 3))
    ys, pool_s, k_s, v_s, sh_s, wkv_s = _sample_layers(
        x_sample.reshape(nb, D_MODEL),
        jnp.transpose(state_pool, (0, 2, 1, 3)),
        to_kernel_cache(cache_swa_k), to_kernel_cache(cache_swa_v),
        jnp.pad(state_rwkv_shift, ((0, 0), (0, 0), (0, D_SHIFT_PAD - D_SHIFT))),
        jnp.transpose(state_rwkv_wkv, (0, 2, 3, 4, 1)).reshape(
            depth, RWKV_HEADS * RWKV_HEAD, RWKV_HEAD, nb),
        cos_s, sin_s, p, bt)
    return (yp, ys[depth - 1].reshape(nb, 1, D_MODEL),
            pool_p, jnp.transpose(pool_s, (0, 2, 1, 3)),
            k_p, from_kernel_cache(k_s),
            v_p, from_kernel_cache(v_s),
            sh_p, sh_s,
            wkv_p, jnp.transpose(
                wkv_s.reshape(depth, RWKV_HEADS, RWKV_HEAD, RWKV_HEAD, nb), (0, 4, 1, 2, 3)))
```
